```python
import jax, jax.numpy as jnp
from jax import lax
import numpy as np

D_MODEL = 1024
BATCH = 8
SEQ = 2048
DEPTH = 2
DEC_BATCH = 128
DEC_SEQ = 1
PAST_LEN = 16384
PAGE_SIZE = 128

N_MIXERS = 2
N_A_LAYERS = (DEPTH + 1) // 2
N_B_LAYERS = DEPTH // 2
CHUNK = 128
GMLP_WIDTH = D_MODEL
GMLP_GROUPS = 8
GMLP_GROUP_DIM = GMLP_WIDTH // GMLP_GROUPS
CONV_WIDTH = 31
CONV_STATE = CONV_WIDTH - 1
N_MEM = 256
XATTN_HEADS = 4
XATTN_HEAD_DIM = D_MODEL // XATTN_HEADS
FFN_DIM = 2816
N_EXPERTS = 8
TOP_K = 2
EXPERT_DIM = 3584
RMS_EPS = 1e-6
LN_EPS = 1e-5

kernel_name = "hybrid_gmlp_conformer_xattn_moe_step"


def rmsnorm(x, g):
    xf = x.astype(jnp.float32)
    y = xf * lax.rsqrt(jnp.mean(xf * xf, axis=-1, keepdims=True) + RMS_EPS)
    return (y * g.astype(jnp.float32)).astype(x.dtype)


def layernorm(x, g, b):
    xf = x.astype(jnp.float32)
    xc = xf - jnp.mean(xf, axis=-1, keepdims=True)
    var = jnp.mean(xc * xc, axis=-1, keepdims=True)
    y = xc * lax.rsqrt(var + LN_EPS) * g.astype(jnp.float32) + b.astype(jnp.float32)
    return y.astype(x.dtype)


def gmlp_mixer(h, w_in, ln_g, ln_b, w_s, b_s, w_out):
    bsz, s, _ = h.shape
    z = jax.nn.gelu(h @ w_in)
    u, v = jnp.split(z, 2, axis=-1)
    v = layernorm(v, ln_g, ln_b)
    lc = min(s, CHUNK)
    pad = (-s) % lc
    vp = jnp.pad(v, ((0, 0), (0, pad), (0, 0)))
    nc = (s + pad) // lc
    vr = vp.reshape(bsz, nc, lc, GMLP_GROUPS, GMLP_GROUP_DIM)
    causal = jnp.tril(jnp.ones((lc, lc), dtype=bool))
    w = jnp.where(causal[None], w_s[:, :lc, :lc], jnp.zeros((), w_s.dtype))
    bias = jnp.transpose(b_s[:, :lc])[None, None, :, :, None]
    mixed = jnp.einsum('gts,bcsgd->bctgd', w, vr) + bias
    mixed = mixed.reshape(bsz, nc * lc, GMLP_WIDTH)[:, :s]
    return (u * mixed) @ w_out, v


def causal_depthwise(ext, w_dw):
    return lax.conv_general_dilated(ext, w_dw[:, None, :], window_strides=(1,), padding='VALID',
                                    dimension_numbers=('NWC', 'WIO', 'NWC'),
                                    feature_group_count=ext.shape[-1])


def conformer_conv(h, buf, w_pw1, w_dw, b_dw, ln_g, ln_b, w_pw2):
    a, gate = jnp.split(h @ w_pw1, 2, axis=-1)
    c = a * jax.nn.sigmoid(gate)
    ext = jnp.concatenate([buf.astype(c.dtype), c], axis=1)
    y = causal_depthwise(ext, w_dw) + b_dw
    y = jax.nn.silu(layernorm(y, ln_g, ln_b))
    return y @ w_pw2, ext[:, -CONV_STATE:]


def memory_kv(mem, g, w_k, w_v):
    b, m, _ = mem.shape
    mn = rmsnorm(mem, g)
    k = (mn @ w_k).reshape(b, m, XATTN_HEADS, XATTN_HEAD_DIM)
    v = (mn @ w_v).reshape(b, m, XATTN_HEADS, XATTN_HEAD_DIM)
    return k, v


def cross_attention(h, k, v, w_q, w_o):
    b, s, _ = h.shape
    q = (h @ w_q).reshape(b, s, XATTN_HEADS, XATTN_HEAD_DIM)
    scores = jnp.einsum('bshd,bmhd->bhsm', q, k, preferred_element_type=jnp.float32) * (XATTN_HEAD_DIM ** -0.5)
    p = jax.nn.softmax(scores, axis=-1).astype(v.dtype)
    o = jnp.einsum('bhsm,bmhd->bshd', p, v).reshape(b, s, D_MODEL)
    return o @ w_o


def swiglu(h, w_gate, w_up, w_down):
    return (jax.nn.silu(h @ w_gate) * (h @ w_up)) @ w_down


def moe_swiglu(h, w_router, w_gate, w_up, w_down):
    logits = (h @ w_router).astype(jnp.float32)
    top_v, top_i = lax.top_k(logits, TOP_K)
    top_w = jax.nn.softmax(top_v, axis=-1)
    gates = jnp.sum(jax.nn.one_hot(top_i, N_EXPERTS, dtype=jnp.float32) * top_w[..., None], axis=-2)
    gates = gates.astype(h.dtype)
    out = jnp.zeros_like(h)
    for e in range(N_EXPERTS):
        out = out + gates[..., e:e + 1] * swiglu(h, w_gate[e], w_up[e], w_down[e])
    return out


def trunk(x, mem_k, mem_v, conv_buf, norm_mix, norm_xattn, norm_ffn, norm_final, w_xq, w_xo,
          a_w_in, a_ln_g, a_ln_b, a_w_s, a_b_s, a_w_out,
          b_w_pw1, b_w_dw, b_b_dw, b_ln_g, b_ln_b, b_w_pw2,
          ffn_w_gate, ffn_w_up, ffn_w_down, moe_w_router, moe_w_gate, moe_w_up, moe_w_down):
    h = x
    conv_new, v_rows = [], []
    for i in range(DEPTH):
        j = i // N_MIXERS
        hn = rmsnorm(h, norm_mix[i])
        if i % N_MIXERS == 0:
            mix, vr = gmlp_mixer(hn, a_w_in[j], a_ln_g[j], a_ln_b[j], a_w_s[j], a_b_s[j], a_w_out[j])
            v_rows.append(vr)
        else:
            mix, nb = conformer_conv(hn, conv_buf[j], b_w_pw1[j], b_w_dw[j], b_b_dw[j],
                                     b_ln_g[j], b_ln_b[j], b_w_pw2[j])
            conv_new.append(nb)
        h = h + mix
        h = h + cross_attention(rmsnorm(h, norm_xattn[i]), mem_k[i], mem_v[i], w_xq[i], w_xo[i])
        hn = rmsnorm(h, norm_ffn[i])
        if i % N_MIXERS == 0:
            h = h + swiglu(hn, ffn_w_gate[j], ffn_w_up[j], ffn_w_down[j])
        else:
            h = h + moe_swiglu(hn, moe_w_router[j], moe_w_gate[j], moe_w_up[j], moe_w_down[j])
    return rmsnorm(h, norm_final), conv_new, v_rows


def setup_inputs(seed: int = 0) -> dict:
    key = jax.random.key(seed)
    ks = iter(jax.random.split(key, 40))

    def nrm(shape, scale):
        return jax.random.normal(next(ks), shape, jnp.float32) * scale

    def gain(shape):
        return 1.0 + nrm(shape, 0.05)

    d = D_MODEL
    return {
        "x_prompt": nrm((BATCH, SEQ, d), 1.0),
        "x_sample": nrm((DEC_BATCH, DEC_SEQ, d), 1.0),
        "mem_prompt": nrm((BATCH, N_MEM, d), 1.0),
        "cache_mem_k": nrm((DEPTH, DEC_BATCH, N_MEM, XATTN_HEADS, XATTN_HEAD_DIM), 1.0),
        "cache_mem_v": nrm((DEPTH, DEC_BATCH, N_MEM, XATTN_HEADS, XATTN_HEAD_DIM), 1.0),
        "state_conv": nrm((N_B_LAYERS, DEC_BATCH, CONV_STATE, d), 0.5),
        "norm_mix": gain((DEPTH, d)),
        "norm_xattn": gain((DEPTH, d)),
        "norm_ffn": gain((DEPTH, d)),
        "norm_mem": gain((DEPTH, d)),
        "norm_final": gain((d,)),
        "w_xq": nrm((DEPTH, d, d), d ** -0.5),
        "w_xk": nrm((DEPTH, d, d), d ** -0.5),
        "w_xv": nrm((DEPTH, d, d), d ** -0.5),
        "w_xo": nrm((DEPTH, d, d), d ** -0.5),
        "a_w_in": nrm((N_A_LAYERS, d, 2 * GMLP_WIDTH), d ** -0.5),
        "a_ln_g": gain((N_A_LAYERS, GMLP_WIDTH)),
        "a_ln_b": nrm((N_A_LAYERS, GMLP_WIDTH), 0.02),
        "a_w_s": nrm((N_A_LAYERS, GMLP_GROUPS, CHUNK, CHUNK), CHUNK ** -0.5),
        "a_b_s": 1.0 + nrm((N_A_LAYERS, GMLP_GROUPS, CHUNK), 0.1),
        "a_w_out": nrm((N_A_LAYERS, GMLP_WIDTH, d), GMLP_WIDTH ** -0.5),
        "b_w_pw1": nrm((N_B_LAYERS, d, 2 * d), d ** -0.5),
        "b_w_dw": nrm((N_B_LAYERS, CONV_WIDTH, d), CONV_WIDTH ** -0.5),
        "b_b_dw": nrm((N_B_LAYERS, d), 0.02),
        "b_ln_g": gain((N_B_LAYERS, d)),
        "b_ln_b": nrm((N_B_LAYERS, d), 0.02),
        "b_w_pw2": nrm((N_B_LAYERS, d, d), d ** -0.5),
        "ffn_w_gate": nrm((N_A_LAYERS, d, FFN_DIM), d ** -0.5),
        "ffn_w_up": nrm((N_A_LAYERS, d, FFN_DIM), d ** -0.5),
        "ffn_w_down": nrm((N_A_LAYERS, FFN_DIM, d), FFN_DIM ** -0.5),
        "moe_w_router": nrm((N_B_LAYERS, d, N_EXPERTS), d ** -0.5),
        "moe_w_gate": nrm((N_B_LAYERS, N_EXPERTS, d, EXPERT_DIM), d ** -0.5),
        "moe_w_up": nrm((N_B_LAYERS, N_EXPERTS, d, EXPERT_DIM), d ** -0.5),
        "moe_w_down": nrm((N_B_LAYERS, N_EXPERTS, EXPERT_DIM, d), EXPERT_DIM ** -0.5),
    }


def reference(x_prompt, x_sample, mem_prompt, cache_mem_k, cache_mem_v, state_conv,
              norm_mix, norm_xattn, norm_ffn, norm_mem, norm_final, w_xq, w_xk, w_xv, w_xo,
              a_w_in, a_ln_g, a_ln_b, a_w_s, a_b_s, a_w_out,
              b_w_pw1, b_w_dw, b_b_dw, b_ln_g, b_ln_b, b_w_pw2,
              ffn_w_gate, ffn_w_up, ffn_w_down, moe_w_router, moe_w_gate, moe_w_up, moe_w_down):
    weights = (norm_mix, norm_xattn, norm_ffn, norm_final, w_xq, w_xo,
               a_w_in, a_ln_g, a_ln_b, a_w_s, a_b_s, a_w_out,
               b_w_pw1, b_w_dw, b_b_dw, b_ln_g, b_ln_b, b_w_pw2,
               ffn_w_gate, ffn_w_up, ffn_w_down, moe_w_router, moe_w_gate, moe_w_up, moe_w_down)

    mk, mv = [], []
    for i in range(DEPTH):
        k, v = memory_kv(mem_prompt, norm_mem[i], w_xk[i], w_xv[i])
        mk.append(k)
        mv.append(v)
    mem_k_prompt = jnp.stack(mk)
    mem_v_prompt = jnp.stack(mv)
    zero_buf = jnp.zeros((N_B_LAYERS, x_prompt.shape[0], CONV_STATE, D_MODEL), x_prompt.dtype)
    y_prompt, conv_p, _ = trunk(x_prompt, mem_k_prompt, mem_v_prompt, zero_buf, *weights)
    conv_state_prompt = jnp.stack(conv_p)

    y_sample, conv_s, v_s = trunk(x_sample, cache_mem_k, cache_mem_v, state_conv, *weights)
    conv_state_sample = jnp.stack(conv_s)
    chunk_v_sample = jnp.stack(v_s)

    return (y_prompt, y_sample, mem_k_prompt, mem_v_prompt, conv_state_prompt, conv_state_sample, chunk_v_sample)
```

```python
import functools

import jax
import jax.numpy as jnp
from jax import lax
from jax.experimental import pallas as pl
from jax.experimental.pallas import tpu as pltpu

F32 = jnp.float32
BF16 = jnp.bfloat16

RMS_EPS = 1e-6
LN_EPS = 1e-5
CHUNK = 128
GROUPS = 8
HEADS = 4
TOP_K = 2
LANES = 128
V7X_VMEM_LIMIT = 56 * 1024 * 1024

TOKEN_TILE = 512
SAMPLE_ATTN_BLOCK = 8
SAMPLE_CONV_BLOCK = 16
SLOT_ROWS = 2048
SUB_ROWS = 256
EXPERT_F_TILE = 512
FFN_F_CHUNK = 256


def _params(n_axes, vmem=V7X_VMEM_LIMIT):
    return pltpu.CompilerParams(dimension_semantics=("arbitrary",) * n_axes,
                                vmem_limit_bytes=vmem)


def _resident(shape):
    nd = len(shape)
    return pl.BlockSpec(shape, lambda *_: (0,) * nd, pipeline_mode=pl.Buffered(1))


def _rmsnorm(x, g):
    return x * lax.rsqrt(jnp.mean(x * x, axis=-1, keepdims=True) + RMS_EPS) * g


def _layernorm(x, g, b):
    xc = x - jnp.mean(x, axis=-1, keepdims=True)
    var = jnp.mean(xc * xc, axis=-1, keepdims=True)
    return xc * lax.rsqrt(var + LN_EPS) * g + b


def _dot(a, b):
    return jnp.dot(a, b, preferred_element_type=F32)


def _cast_kernel(x_ref, o_ref):
    o_ref[...] = x_ref[...].astype(o_ref.dtype)


def _cast_bf16(w, rows_per_step):
    r, c = w.shape
    return pl.pallas_call(
        _cast_kernel,
        grid=(r // rows_per_step,),
        in_specs=[pl.BlockSpec((rows_per_step, c), lambda i: (i, 0))],
        out_specs=pl.BlockSpec((rows_per_step, c), lambda i: (i, 0)),
        out_shape=jax.ShapeDtypeStruct((r, c), BF16),
        compiler_params=_params(1),
        name="cast_bf16",
    )(w)


def _memkv_kernel(mem_ref, g_ref, wk_ref, wv_ref, k_ref, v_ref, mn_ref):
    @pl.when(pl.program_id(1) == 0)
    def _():
        mn_ref[...] = _rmsnorm(mem_ref[...], g_ref[0]).astype(BF16)

    mn = mn_ref[...]
    k_ref[0] = _dot(mn, wk_ref[0].astype(BF16))
    v_ref[0] = _dot(mn, wv_ref[0].astype(BF16))


def _memory_kv(mem, norm_mem, w_xk, w_xv, tn=256):
    depth, d, _ = w_xk.shape
    m = mem.shape[0]
    out = jax.ShapeDtypeStruct((depth, m, d), F32)
    w_spec = pl.BlockSpec((1, d, tn), lambda l, j: (l, 0, j))
    o_spec = pl.BlockSpec((1, m, tn), lambda l, j: (l, 0, j))
    return pl.pallas_call(
        _memkv_kernel,
        grid=(depth, d // tn),
        in_specs=[pl.BlockSpec((m, d), lambda l, j: (0, 0)),
                  pl.BlockSpec((1, 1, d), lambda l, j: (l, 0, 0)),
                  w_spec, w_spec],
        out_specs=(o_spec, o_spec),
        out_shape=(out, out),
        scratch_shapes=[pltpu.VMEM((m, d), BF16)],
        compiler_params=_params(2),
        name="memory_kv",
    )(mem, norm_mem.reshape(depth, 1, d), w_xk, w_xv)


def _gmlp_kernel(xp_ref, xs_ref, gn_ref, win_ref, lng_ref, lnb_ref, wmix_ref, bias_ref, wout_ref,
                 op_ref, os_ref, vs_ref, win_bf, wout_bf, *, n_prompt):
    i = pl.program_id(0)
    width = wout_ref.shape[0]
    gdim = width // GROUPS

    @pl.when(i == 0)
    def _():
        win_bf[...] = win_ref[...].astype(BF16)
        wout_bf[...] = wout_ref[...].astype(BF16)

    def mixer(x, mode):
        rows = x.shape[0]
        xn = _rmsnorm(x, gn_ref[...]).astype(BF16)
        z = jax.nn.gelu(_dot(xn, win_bf[...]))
        u = z[:, :width]
        v = _layernorm(z[:, width:], lng_ref[...], lnb_ref[...])
        vb = v.astype(BF16)
        chunks = []
        for c in range(rows // CHUNK):
            cols = [_dot(wmix_ref[mode, g], vb[c * CHUNK:(c + 1) * CHUNK, g * gdim:(g + 1) * gdim])
                    for g in range(GROUPS)]
            chunks.append(jnp.concatenate(cols, axis=1) + bias_ref[mode])
        mixed = chunks[0] if len(chunks) == 1 else jnp.concatenate(chunks, axis=0)
        gated = (u * mixed).astype(BF16)
        return x + _dot(gated, wout_bf[...]), v

    @pl.when(i < n_prompt)
    def _():
        op_ref[...] = mixer(xp_ref[...], 0)[0]

    @pl.when(i == n_prompt)
    def _():
        out, v = mixer(xs_ref[...], 1)
        os_ref[...] = out
        vs_ref[...] = v


def _gmlp(xp, xs, g_norm, w_in, ln_g, ln_b, w_s, b_s, w_out):
    tp, d = xp.shape
    ns = xs.shape[0]
    width = w_out.shape[0]
    gdim = width // GROUPS
    assert tp % TOKEN_TILE == 0 and TOKEN_TILE % CHUNK == 0 and ns == CHUNK
    n_prompt = tp // TOKEN_TILE
    causal = jnp.tril(jnp.ones((CHUNK, CHUNK), dtype=bool))
    w_prompt = jnp.where(causal[None], w_s, 0.0)
    w_sample = w_s[:, 0, 0][:, None, None] * jnp.eye(CHUNK, dtype=F32)[None]
    wmix = jnp.stack([w_prompt, w_sample]).astype(BF16)
    b_prompt = jnp.repeat(jnp.transpose(b_s), gdim, axis=1)
    b_sample = jnp.broadcast_to(jnp.repeat(b_s[:, 0], gdim)[None], (CHUNK, width))
    bias = jnp.stack([b_prompt, b_sample])

    tile = lambda i: (jnp.minimum(i, n_prompt - 1), 0)
    row = lambda a: a.reshape(1, -1)
    return pl.pallas_call(
        functools.partial(_gmlp_kernel, n_prompt=n_prompt),
        grid=(n_prompt + 1,),
        in_specs=[pl.BlockSpec((TOKEN_TILE, d), tile),
                  _resident((ns, d)), _resident((1, d)), _resident(w_in.shape),
                  _resident((1, width)), _resident((1, width)),
                  _resident(wmix.shape), _resident(bias.shape), _resident(w_out.shape)],
        out_specs=(pl.BlockSpec((TOKEN_TILE, d), tile),
                   pl.BlockSpec((ns, d), lambda i: (0, 0)),
                   pl.BlockSpec((ns, width), lambda i: (0, 0))),
        out_shape=(jax.ShapeDtypeStruct((tp, d), F32),
                   jax.ShapeDtypeStruct((ns, d), F32),
                   jax.ShapeDtypeStruct((ns, width), F32)),
        scratch_shapes=[pltpu.VMEM(w_in.shape, BF16), pltpu.VMEM(w_out.shape, BF16)],
        compiler_params=_params(1),
        name="gmlp_mixer",
    )(xp, xs, row(g_norm), w_in, row(ln_g), row(ln_b), wmix, bias, w_out)


def _softmax_rows(s, axis):
    m = jnp.max(s, axis=axis, keepdims=True)
    e = jnp.exp(s - m)
    return e / jnp.sum(e, axis=axis, keepdims=True)


def _xattn_kernel(xp_ref, xs_ref, gn_ref, wq_ref, wo_ref, k_ref, v_ref,
                  op_ref, qs_ref, wq_bf, wo_bf, *, n_prompt):
    i = pl.program_id(0)
    d = wq_ref.shape[0]
    hd = d // HEADS
    scale = hd ** -0.5

    @pl.when(i == 0)
    def _():
        wq_bf[...] = wq_ref[...].astype(BF16)
        wo_bf[...] = wo_ref[...].astype(BF16)

    @pl.when(i < n_prompt)
    def _():
        x = xp_ref[...]
        q = _dot(_rmsnorm(x, gn_ref[...]).astype(BF16), wq_bf[...])
        heads = []
        for h in range(HEADS):
            cols = slice(h * hd, (h + 1) * hd)
            qh = q[:, cols].astype(BF16)
            kh = k_ref[0, :, cols].astype(BF16)
            s = lax.dot_general(qh, kh, (((1,), (1,)), ((), ())),
                                preferred_element_type=F32) * scale
            p = _softmax_rows(s, -1).astype(BF16)
            heads.append(_dot(p, v_ref[0, :, cols].astype(BF16)))
        o = jnp.concatenate(heads, axis=1).astype(BF16)
        op_ref[...] = x + _dot(o, wo_bf[...])

    @pl.when(i == n_prompt)
    def _():
        qs_ref[...] = _dot(_rmsnorm(xs_ref[...], gn_ref[...]).astype(BF16), wq_bf[...])


def _xattn_prompt(xp, xs, g_norm, w_q, w_o, mem_k, mem_v):
    tp, d = xp.shape
    ns = xs.shape[0]
    nb, n_mem, _ = mem_k.shape
    n_prompt = tp // TOKEN_TILE
    tiles_per_seq = n_prompt // nb
    assert tiles_per_seq * nb == n_prompt
    tile = lambda i: (jnp.minimum(i, n_prompt - 1), 0)
    kv = lambda i: (jnp.minimum(i, n_prompt - 1) // tiles_per_seq, 0, 0)
    return pl.pallas_call(
        functools.partial(_xattn_kernel, n_prompt=n_prompt),
        grid=(n_prompt + 1,),
        in_specs=[pl.BlockSpec((TOKEN_TILE, d), tile),
                  _resident((ns, d)), _resident((1, d)), _resident((d, d)), _resident((d, d)),
                  pl.BlockSpec((1, n_mem, d), kv), pl.BlockSpec((1, n_mem, d), kv)],
        out_specs=(pl.BlockSpec((TOKEN_TILE, d), tile),
                   pl.BlockSpec((ns, d), lambda i: (0, 0))),
        out_shape=(jax.ShapeDtypeStruct((tp, d), F32), jax.ShapeDtypeStruct((ns, d), F32)),
        scratch_shapes=[pltpu.VMEM((d, d), BF16), pltpu.VMEM((d, d), BF16)],
        compiler_params=_params(1),
        name="xattn_prompt",
    )(xp, xs, g_norm.reshape(1, d), w_q, w_o, mem_k, mem_v)


def _xattn_sample_kernel(q_ref, xs_ref, wo_ref, hsum_ref, hexp_ref, k_ref, v_ref,
                         os_ref, o_acc, *, block, scale):
    i = pl.program_id(0)
    for b in range(block):
        r = i * block + b
        qb = q_ref[pl.ds(r, 1), :] * scale
        prod = (k_ref[0, b] * qb).astype(BF16)
        s = _dot(prod, hsum_ref[...])
        p = _softmax_rows(s, 0).astype(BF16)
        pe = _dot(p, hexp_ref[...])
        o_acc[pl.ds(r, 1), :] = jnp.sum(pe * v_ref[0, b], axis=0, keepdims=True)

    @pl.when(i == pl.num_programs(0) - 1)
    def _():
        os_ref[...] = xs_ref[...] + _dot(o_acc[...].astype(BF16), wo_ref[...].astype(BF16))


def _xattn_sample(qs, xs, w_o, cache_k, cache_v, layer):
    ns, d = xs.shape
    n_mem = cache_k.shape[2]
    hd = d // HEADS
    block = SAMPLE_ATTN_BLOCK
    assert ns % block == 0
    head_of_col = jnp.arange(d) // hd
    hsum = (head_of_col[:, None] == jnp.arange(LANES)[None, :]).astype(BF16)
    hexp = jnp.transpose(hsum)
    kv = pl.BlockSpec((1, block, n_mem, d), lambda i: (layer, i, 0, 0))
    return pl.pallas_call(
        functools.partial(_xattn_sample_kernel, block=block, scale=hd ** -0.5),
        grid=(ns // block,),
        in_specs=[_resident((ns, d)), _resident((ns, d)), _resident((d, d)),
                  _resident((d, LANES)), _resident((LANES, d)), kv, kv],
        out_specs=pl.BlockSpec((ns, d), lambda i: (0, 0)),
        out_shape=jax.ShapeDtypeStruct((ns, d), F32),
        scratch_shapes=[pltpu.VMEM((ns, d), F32)],
        compiler_params=_params(1),
        name="xattn_sample",
    )(qs, xs, w_o, hsum, hexp, cache_k, cache_v)


def _ffn_kernel(xp_ref, xs_ref, gn_ref, wg_ref, wu_ref, wd_ref, op_ref, os_ref, *, n_prompt):
    i = pl.program_id(0)
    f_dim = wg_ref.shape[1]

    def ffn(x):
        xn = _rmsnorm(x, gn_ref[...]).astype(BF16)
        acc = x
        for f in range(0, f_dim, FFN_F_CHUNK):
            cols = slice(f, f + FFN_F_CHUNK)
            hid = jax.nn.silu(_dot(xn, wg_ref[:, cols])) * _dot(xn, wu_ref[:, cols])
            acc = acc + _dot(hid.astype(BF16), wd_ref[cols, :])
        return acc

    @pl.when(i < n_prompt)
    def _():
        op_ref[...] = ffn(xp_ref[...])

    @pl.when(i == n_prompt)
    def _():
        os_ref[...] = ffn(xs_ref[...])


def _ffn(xp, xs, g_norm, w_gate, w_up, w_down):
    tp, d = xp.shape
    ns = xs.shape[0]
    f_dim = w_gate.shape[1]
    assert f_dim % FFN_F_CHUNK == 0
    n_prompt = tp // TOKEN_TILE
    wg = _cast_bf16(w_gate, d // 4)
    wu = _cast_bf16(w_up, d // 4)
    wd = _cast_bf16(w_down, f_dim // 4)
    tile = lambda i: (jnp.minimum(i, n_prompt - 1), 0)
    return pl.pallas_call(
        functools.partial(_ffn_kernel, n_prompt=n_prompt),
        grid=(n_prompt + 1,),
        in_specs=[pl.BlockSpec((TOKEN_TILE, d), tile),
                  _resident((ns, d)), _resident((1, d)),
                  _resident(wg.shape), _resident(wu.shape), _resident(wd.shape)],
        out_specs=(pl.BlockSpec((TOKEN_TILE, d), tile),
                   pl.BlockSpec((ns, d), lambda i: (0, 0))),
        out_shape=(jax.ShapeDtypeStruct((tp, d), F32), jax.ShapeDtypeStruct((ns, d), F32)),
        compiler_params=_params(1),
        name="dense_swiglu",
    )(xp, xs, g_norm.reshape(1, d), wg, wu, wd)


CONV_HALO = 32


def _conv_prompt_kernel(xp_ref, gn_ref, w1_ref, wdw_ref, bdw_ref, lng_ref, lnb_ref, w2_ref,
                        op_ref, st_ref, w1_bf, w2_bf, cbuf, *, tiles_per_seq):
    i = pl.program_id(0)
    d = w2_ref.shape[0]
    taps = wdw_ref.shape[0]
    tm = xp_ref.shape[0]

    @pl.when(i == 0)
    def _():
        w1_bf[...] = w1_ref[...].astype(BF16)
        w2_bf[...] = w2_ref[...].astype(BF16)

    @pl.when(i % tiles_per_seq == 0)
    def _():
        cbuf[0:CONV_HALO, :] = jnp.zeros((CONV_HALO, d), F32)

    x = xp_ref[...]
    ag = _dot(_rmsnorm(x, gn_ref[...]).astype(BF16), w1_bf[...])
    c = ag[:, :d] * jax.nn.sigmoid(ag[:, d:])
    cbuf[CONV_HALO:CONV_HALO + tm, :] = c
    first = CONV_HALO - (taps - 1)
    y = jnp.zeros((tm, d), F32) + bdw_ref[...]
    for k in range(taps):
        y = y + wdw_ref[k:k + 1, :] * cbuf[first + k:first + k + tm, :]
    t = jax.nn.silu(_layernorm(y, lng_ref[...], lnb_ref[...])).astype(BF16)
    op_ref[...] = x + _dot(t, w2_bf[...])
    cbuf[0:CONV_HALO, :] = cbuf[tm:tm + CONV_HALO, :]

    @pl.when(i % tiles_per_seq == tiles_per_seq - 1)
    def _():
        st_ref[0] = cbuf[first:CONV_HALO, :]


def _conv_prompt(xp, n_seq, g_norm, w_pw1, w_dw, b_dw, ln_g, ln_b, w_pw2):
    tp, d = xp.shape
    taps = w_dw.shape[0]
    n_prompt = tp // TOKEN_TILE
    tiles_per_seq = n_prompt // n_seq
    assert tiles_per_seq * n_seq == n_prompt and taps - 1 <= CONV_HALO <= TOKEN_TILE
    row = lambda a: a.reshape(1, -1)
    return pl.pallas_call(
        functools.partial(_conv_prompt_kernel, tiles_per_seq=tiles_per_seq),
        grid=(n_prompt,),
        in_specs=[pl.BlockSpec((TOKEN_TILE, d), lambda i: (i, 0)),
                  _resident((1, d)), _resident(w_pw1.shape), _resident(w_dw.shape),
                  _resident((1, d)), _resident((1, d)), _resident((1, d)), _resident(w_pw2.shape)],
        out_specs=(pl.BlockSpec((TOKEN_TILE, d), lambda i: (i, 0)),
                   pl.BlockSpec((1, taps - 1, d), lambda i: (i // tiles_per_seq, 0, 0))),
        out_shape=(jax.ShapeDtypeStruct((tp, d), F32),
                   jax.ShapeDtypeStruct((n_seq, taps - 1, d), F32)),
        scratch_shapes=[pltpu.VMEM(w_pw1.shape, BF16), pltpu.VMEM(w_pw2.shape, BF16),
                        pltpu.VMEM((TOKEN_TILE + CONV_HALO, d), F32)],
        compiler_params=_params(1),
        name="conv_prompt",
    )(xp, row(g_norm), w_pw1, w_dw, row(b_dw), row(ln_g), row(ln_b), w_pw2)


def _conv_sample_kernel(xs_ref, gn_ref, w1_ref, wdw_ref, bdw_ref, lng_ref, lnb_ref, w2_ref, st_ref,
                        os_ref, sto_ref, c_all, t_all, *, block):
    i = pl.program_id(0)
    d = w2_ref.shape[0]
    taps = wdw_ref.shape[0]

    @pl.when(i == 0)
    def _():
        ag = _dot(_rmsnorm(xs_ref[...], gn_ref[...]).astype(BF16), w1_ref[...].astype(BF16))
        c_all[...] = ag[:, :d] * jax.nn.sigmoid(ag[:, d:])

    w_hist = wdw_ref[0:taps - 1, :]
    w_last = wdw_ref[taps - 1:taps, :]
    for b in range(block):
        r = i * block + b
        hist = st_ref[0, b]
        cb = c_all[pl.ds(r, 1), :]
        y = jnp.sum(hist * w_hist, axis=0, keepdims=True) + cb * w_last + bdw_ref[...]
        t_all[pl.ds(r, 1), :] = jax.nn.silu(_layernorm(y, lng_ref[...], lnb_ref[...]))
        sto_ref[0, b, 0:taps - 2, :] = hist[1:taps - 1, :]
        sto_ref[0, b, taps - 2:taps - 1, :] = cb

    @pl.when(i == pl.num_programs(0) - 1)
    def _():
        os_ref[...] = xs_ref[...] + _dot(t_all[...].astype(BF16), w2_ref[...].astype(BF16))


def _conv_sample(xs, state, g_norm, w_pw1, w_dw, b_dw, ln_g, ln_b, w_pw2):
    ns, d = xs.shape
    taps = w_dw.shape[0]
    block = SAMPLE_CONV_BLOCK
    assert ns % block == 0
    row = lambda a: a.reshape(1, -1)
    st = pl.BlockSpec((1, block, taps - 1, d), lambda i: (0, i, 0, 0))
    return pl.pallas_call(
        functools.partial(_conv_sample_kernel, block=block),
        grid=(ns // block,),
        in_specs=[_resident((ns, d)), _resident((1, d)), _resident(w_pw1.shape),
                  _resident(w_dw.shape), _resident((1, d)), _resident((1, d)), _resident((1, d)),
                  _resident(w_pw2.shape), st],
        out_specs=(pl.BlockSpec((ns, d), lambda i: (0, 0)), st),
        out_shape=(jax.ShapeDtypeStruct((ns, d), F32), jax.ShapeDtypeStruct(state.shape, F32)),
        scratch_shapes=[pltpu.VMEM((ns, d), F32), pltpu.VMEM((ns, d), F32)],
        compiler_params=_params(1),
        name="conv_sample",
    )(xs, row(g_norm), w_pw1, w_dw, row(b_dw), row(ln_g), row(ln_b), w_pw2, state)


def _router_kernel(xp_ref, xs_ref, gn_ref, wr_ref, hn_ref, gates_ref, *, n_prompt, n_experts):
    i = pl.program_id(0)

    def route(x):
        xn = _rmsnorm(x, gn_ref[...])
        logits = jnp.dot(xn, wr_ref[...], precision=lax.Precision.HIGHEST,
                         preferred_element_type=F32)
        lane = lax.broadcasted_iota(jnp.int32, logits.shape, 1).astype(F32)
        neg = jnp.float32(-jnp.inf)
        logits = jnp.where(lane < n_experts, logits, neg)
        v1 = jnp.max(logits, axis=-1, keepdims=True)
        i1 = jnp.min(jnp.where(logits == v1, lane, float(LANES)), axis=-1, keepdims=True)
        rest = jnp.where(lane == i1, neg, logits)
        v2 = jnp.max(rest, axis=-1, keepdims=True)
        i2 = jnp.min(jnp.where(rest == v2, lane, float(LANES)), axis=-1, keepdims=True)
        e2 = jnp.exp(v2 - v1)
        denom = 1.0 + e2
        gates = jnp.where(lane == i1, 1.0 / denom, jnp.where(lane == i2, e2 / denom, 0.0))
        return xn.astype(BF16), gates

    @pl.when(i < n_prompt)
    def _():
        hn, gates = route(xp_ref[...])
        hn_ref[...] = hn
        gates_ref[...] = gates

    @pl.when(i >= n_prompt)
    def _():
        hn_ref[...] = jnp.zeros(hn_ref.shape, BF16)
        gates_ref[...] = jnp.zeros(gates_ref.shape, F32)

    @pl.when(i == n_prompt)
    def _():
        ns = xs_ref.shape[0]
        hn, gates = route(xs_ref[...])
        hn_ref[0:ns, :] = hn
        gates_ref[0:ns, :] = gates


def _router(xp, xs, g_norm, w_router, t_pad):
    tp, d = xp.shape
    ns = xs.shape[0]
    n_experts = w_router.shape[1]
    n_prompt = tp // TOKEN_TILE
    wr = jnp.pad(w_router, ((0, 0), (0, LANES - n_experts)))
    tile = lambda i: (jnp.minimum(i, n_prompt - 1), 0)
    return pl.pallas_call(
        functools.partial(_router_kernel, n_prompt=n_prompt, n_experts=n_experts),
        grid=(t_pad // TOKEN_TILE,),
        in_specs=[pl.BlockSpec((TOKEN_TILE, d), tile),
                  _resident((ns, d)), _resident((1, d)), _resident((d, LANES))],
        out_specs=(pl.BlockSpec((TOKEN_TILE, d), lambda i: (i, 0)),
                   pl.BlockSpec((TOKEN_TILE, LANES), lambda i: (i, 0))),
        out_shape=(jax.ShapeDtypeStruct((t_pad, d), BF16),
                   jax.ShapeDtypeStruct((t_pad, LANES), F32)),
        compiler_params=_params(1),
        name="moe_router",
    )(xp, xs, g_norm.reshape(1, d), wr)


def _expert_kernel(slot_expert, slot_xblk, slot_nsub, x_ref, wg_ref, wu_ref, wd_ref, y_ref,
                   wg_bf, wu_bf, wd_bf, acc):
    s = pl.program_id(0)
    f = pl.program_id(1)
    n_sub = slot_nsub[s]

    @pl.when(n_sub > 0)
    def _():
        wg_bf[...] = wg_ref[0].astype(BF16)
        wu_bf[...] = wu_ref[0].astype(BF16)
        wd_bf[...] = wd_ref[0].astype(BF16)

    def sub_block(j, carry):
        rows = pl.ds(pl.multiple_of(j * SUB_ROWS, SUB_ROWS), SUB_ROWS)
        xg = x_ref[rows, :]
        hid = jax.nn.silu(_dot(xg, wg_bf[...])) * _dot(xg, wu_bf[...])
        part = _dot(hid.astype(BF16), wd_bf[...])

        @pl.when(f == 0)
        def _():
            acc[rows, :] = part

        @pl.when(f > 0)
        def _():
            acc[rows, :] += part

        @pl.when(f == pl.num_programs(1) - 1)
        def _():
            y_ref[rows, :] = acc[rows, :].astype(y_ref.dtype)

        return carry

    lax.fori_loop(0, n_sub, sub_block, 0)


def _expert_ffn(x_slots, slot_expert, slot_xblk, slot_nsub, w_gate, w_up, w_down):
    n_slots = slot_expert.shape[0]
    d = x_slots.shape[1]
    f_dim = w_gate.shape[2]
    tf = EXPERT_F_TILE
    assert f_dim % tf == 0 and SLOT_ROWS % SUB_ROWS == 0
    grid_spec = pltpu.PrefetchScalarGridSpec(
        num_scalar_prefetch=3,
        grid=(n_slots, f_dim // tf),
        in_specs=[pl.BlockSpec((SLOT_ROWS, d), lambda s, f, se, sx, sn: (sx[s], 0)),
                  pl.BlockSpec((1, d, tf), lambda s, f, se, sx, sn: (se[s], 0, f)),
                  pl.BlockSpec((1, d, tf), lambda s, f, se, sx, sn: (se[s], 0, f)),
                  pl.BlockSpec((1, tf, d), lambda s, f, se, sx, sn: (se[s], f, 0))],
        out_specs=pl.BlockSpec((SLOT_ROWS, d), lambda s, f, se, sx, sn: (s, 0)),
        scratch_shapes=[pltpu.VMEM((d, tf), BF16), pltpu.VMEM((d, tf), BF16),
                        pltpu.VMEM((tf, d), BF16), pltpu.VMEM((SLOT_ROWS, d), F32)],
    )
    return pl.pallas_call(
        _expert_kernel,
        grid_spec=grid_spec,
        out_shape=jax.ShapeDtypeStruct((n_slots * SLOT_ROWS, d), BF16),
        compiler_params=_params(2),
        name="expert_swiglu",
    )(slot_expert, slot_xblk, slot_nsub, x_slots, w_gate, w_up, w_down)


def _combine_kernel(xp_ref, xs_ref, gates_ref, y_ref, gf_ref, op_ref, os_ref, *, n_prompt, n_experts):
    i = pl.program_id(0)

    def combine(x, rows):
        out = x
        for e in range(n_experts):
            out = out + gates_ref[0:rows, e:e + 1] * y_ref[e, 0:rows, :].astype(F32)
        return _rmsnorm(out, gf_ref[...])

    @pl.when(i < n_prompt)
    def _():
        op_ref[...] = combine(xp_ref[...], TOKEN_TILE)

    @pl.when(i == n_prompt)
    def _():
        os_ref[...] = combine(xs_ref[...], xs_ref.shape[0])


def _combine(xp, xs, gates, y, g_final, n_experts):
    tp, d = xp.shape
    ns = xs.shape[0]
    n_prompt = tp // TOKEN_TILE
    tile = lambda i: (jnp.minimum(i, n_prompt - 1), 0)
    return pl.pallas_call(
        functools.partial(_combine_kernel, n_prompt=n_prompt, n_experts=n_experts),
        grid=(n_prompt + 1,),
        in_specs=[pl.BlockSpec((TOKEN_TILE, d), tile), _resident((ns, d)),
                  pl.BlockSpec((TOKEN_TILE, LANES), lambda i: (i, 0)),
                  pl.BlockSpec((n_experts, TOKEN_TILE, d), lambda i: (0, i, 0)),
                  _resident((1, d))],
        out_specs=(pl.BlockSpec((TOKEN_TILE, d), tile),
                   pl.BlockSpec((ns, d), lambda i: (0, 0))),
        out_shape=(jax.ShapeDtypeStruct((tp, d), F32), jax.ShapeDtypeStruct((ns, d), F32)),
        compiler_params=_params(1),
        name="moe_combine",
    )(xp, xs, gates, y, g_final.reshape(1, d))


def _moe(xp, xs, g_norm, w_router, w_gate, w_up, w_down, g_final):
    tp, d = xp.shape
    ns = xs.shape[0]
    n_experts = w_router.shape[1]
    assert ns <= TOKEN_TILE
    t_all = tp + TOKEN_TILE
    t_pad = -(-t_all // SLOT_ROWS) * SLOT_ROWS
    hn, gates = _router(xp, xs, g_norm, w_router, t_pad)
    slots_per_expert = t_pad // SLOT_ROWS
    sid = jnp.arange(n_experts * slots_per_expert, dtype=jnp.int32)
    slot_expert = sid // slots_per_expert
    slot_xblk = sid % slots_per_expert
    slot_nsub = jnp.full_like(sid, SLOT_ROWS // SUB_ROWS)
    y = _expert_ffn(hn, slot_expert, slot_xblk, slot_nsub, w_gate, w_up, w_down)
    return _combine(xp, xs, gates, y.reshape(n_experts, t_pad, d), g_final, n_experts)


def kernel(x_prompt, x_sample, mem_prompt, cache_mem_k, cache_mem_v, state_conv, norm_mix, norm_xattn, norm_ffn, norm_mem, norm_final, w_xq, w_xk, w_xv, w_xo, a_w_in, a_ln_g, a_ln_b, a_w_s, a_b_s, a_w_out, b_w_pw1, b_w_dw, b_b_dw, b_ln_g, b_ln_b, b_w_pw2, ffn_w_gate, ffn_w_up, ffn_w_down, moe_w_router, moe_w_gate, moe_w_up, moe_w_down):
    nb, seq, d = x_prompt.shape
    ns = x_sample.shape[0]
    depth = norm_mix.shape[0]
    n_mem = mem_prompt.shape[1]
    assert depth == 2 and x_sample.shape[1] == 1

    mem_k, mem_v = _memory_kv(mem_prompt.reshape(nb * n_mem, d), norm_mem, w_xk, w_xv)
    mem_k = mem_k.reshape(depth, nb, n_mem, d)
    mem_v = mem_v.reshape(depth, nb, n_mem, d)
    cache_k = cache_mem_k.reshape(depth, ns, n_mem, d)
    cache_v = cache_mem_v.reshape(depth, ns, n_mem, d)

    hp = x_prompt.reshape(nb * seq, d)
    hs = x_sample.reshape(ns, d)

    def cross_attention(hp, hs, layer):
        hp_new, qs = _xattn_prompt(hp, hs, norm_xattn[layer], w_xq[layer], w_xo[layer],
                                   mem_k[layer], mem_v[layer])
        hs_new = _xattn_sample(qs, hs, w_xo[layer], cache_k, cache_v, layer)
        return hp_new, hs_new

    hp, hs, v_sample = _gmlp(hp, hs, norm_mix[0], a_w_in[0], a_ln_g[0], a_ln_b[0],
                             a_w_s[0], a_b_s[0], a_w_out[0])
    hp, hs = cross_attention(hp, hs, 0)
    hp, hs = _ffn(hp, hs, norm_ffn[0], ffn_w_gate[0], ffn_w_up[0], ffn_w_down[0])

    conv_w = (norm_mix[1], b_w_pw1[0], b_w_dw[0], b_b_dw[0], b_ln_g[0], b_ln_b[0], b_w_pw2[0])
    hp, conv_state_prompt = _conv_prompt(hp, nb, *conv_w)
    hs, conv_state_sample = _conv_sample(hs, state_conv, *conv_w)
    hp, hs = cross_attention(hp, hs, 1)
    yp, ys = _moe(hp, hs, norm_ffn[1], moe_w_router[0], moe_w_gate[0], moe_w_up[0], moe_w_down[0],
                  norm_final)

    hd = d // HEADS
    return (yp.reshape(nb, seq, d),
            ys.reshape(ns, 1, d),
            mem_k.reshape(depth, nb, n_mem, HEADS, hd),
            mem_v.reshape(depth, nb, n_mem, HEADS, hd),
            conv_state_prompt[None],
            conv_state_sample,
            v_sample.reshape(1, ns, 1, -1))
```

```python
import functools

import jax
import jax.numpy as jnp
from jax import lax
from jax.experimental import pallas as pl
from jax.experimental.pallas import tpu as pltpu

F32 = jnp.float32
BF16 = jnp.bfloat16

RMS_EPS = 1e-6
LN_EPS = 1e-5
CHUNK = 128
GROUPS = 8
HEADS = 4
TOP_K = 2
LANES = 128
V7X_VMEM_LIMIT = 56 * 1024 * 1024

TOKEN_TILE = 512
SAMPLE_ATTN_BLOCK = 8
SAMPLE_CONV_BLOCK = 16
SLOT_ROWS = 2048
SUB_ROWS = 256
EXPERT_F_TILE = 512
DISPATCH_WIN = 256
COMBINE_BLOCK = 256
FFN_F_CHUNK = 256


def _params(n_axes, vmem=V7X_VMEM_LIMIT):
    return pltpu.CompilerParams(dimension_semantics=("arbitrary",) * n_axes,
                                vmem_limit_bytes=vmem)


def _resident(shape):
    nd = len(shape)
    return pl.BlockSpec(shape, lambda *_: (0,) * nd, pipeline_mode=pl.Buffered(1))


def _rmsnorm(x, g):
    return x * lax.rsqrt(jnp.mean(x * x, axis=-1, keepdims=True) + RMS_EPS) * g


def _layernorm(x, g, b):
    xc = x - jnp.mean(x, axis=-1, keepdims=True)
    var = jnp.mean(xc * xc, axis=-1, keepdims=True)
    return xc * lax.rsqrt(var + LN_EPS) * g + b


def _dot(a, b):
    return jnp.dot(a, b, preferred_element_type=F32)


def _cast_kernel(x_ref, o_ref):
    o_ref[...] = x_ref[...].astype(o_ref.dtype)


def _cast_bf16(w, rows_per_step):
    r, c = w.shape
    return pl.pallas_call(
        _cast_kernel,
        grid=(r // rows_per_step,),
        in_specs=[pl.BlockSpec((rows_per_step, c), lambda i: (i, 0))],
        out_specs=pl.BlockSpec((rows_per_step, c), lambda i: (i, 0)),
        out_shape=jax.ShapeDtypeStruct((r, c), BF16),
        compiler_params=_params(1),
        name="cast_bf16",
    )(w)


def _memkv_kernel(mem_ref, g_ref, wk_ref, wv_ref, k_ref, v_ref, kh_ref, vh_ref, wk_bf, wv_bf):
    @pl.when(pl.program_id(1) == 0)
    def _():
        wk_bf[...] = wk_ref[0].astype(BF16)
        wv_bf[...] = wv_ref[0].astype(BF16)

    hd = kh_ref.shape[-1]
    mn = _rmsnorm(mem_ref[0], g_ref[0]).astype(BF16)
    k = _dot(mn, wk_bf[...])
    v = _dot(mn, wv_bf[...])
    k_ref[0, 0] = k
    v_ref[0, 0] = v
    for h in range(HEADS):
        kh_ref[0, 0, :, h, :] = k[:, h * hd:(h + 1) * hd]
        vh_ref[0, 0, :, h, :] = v[:, h * hd:(h + 1) * hd]


def _memory_kv(mem, norm_mem, w_xk, w_xv):
    depth, d, _ = w_xk.shape
    nb, n_mem, _ = mem.shape
    hd = d // HEADS
    flat = jax.ShapeDtypeStruct((depth, nb, n_mem, d), F32)
    heads = jax.ShapeDtypeStruct((depth, nb, n_mem, HEADS, hd), F32)
    w_spec = pl.BlockSpec((1, d, d), lambda l, b: (l, 0, 0))
    flat_spec = pl.BlockSpec((1, 1, n_mem, d), lambda l, b: (l, b, 0, 0))
    heads_spec = pl.BlockSpec((1, 1, n_mem, HEADS, hd), lambda l, b: (l, b, 0, 0, 0))
    return pl.pallas_call(
        _memkv_kernel,
        grid=(depth, nb),
        in_specs=[pl.BlockSpec((1, n_mem, d), lambda l, b: (b, 0, 0)),
                  pl.BlockSpec((1, 1, d), lambda l, b: (l, 0, 0)),
                  w_spec, w_spec],
        out_specs=(flat_spec, flat_spec, heads_spec, heads_spec),
        out_shape=(flat, flat, heads, heads),
        scratch_shapes=[pltpu.VMEM((d, d), BF16), pltpu.VMEM((d, d), BF16)],
        compiler_params=_params(2),
        name="memory_kv",
    )(mem, norm_mem.reshape(depth, 1, d), w_xk, w_xv)


def _gmlp_kernel(xp_ref, xs_ref, gn_ref, win_ref, lng_ref, lnb_ref, wmix_ref, bias_ref, wout_ref,
                 op_ref, os_ref, vs_ref, win_bf, wout_bf, *, n_prompt):
    i = pl.program_id(0)
    width = wout_ref.shape[0]
    gdim = width // GROUPS

    @pl.when(i == 0)
    def _():
        win_bf[...] = win_ref[...].astype(BF16)
        wout_bf[...] = wout_ref[...].astype(BF16)

    def mixer(x, mode):
        rows = x.shape[0]
        xn = _rmsnorm(x, gn_ref[...]).astype(BF16)
        z = jax.nn.gelu(_dot(xn, win_bf[...]))
        u = z[:, :width]
        v = _layernorm(z[:, width:], lng_ref[...], lnb_ref[...])
        vb = v.astype(BF16)
        chunks = []
        for c in range(rows // CHUNK):
            cols = [_dot(wmix_ref[mode, g], vb[c * CHUNK:(c + 1) * CHUNK, g * gdim:(g + 1) * gdim])
                    for g in range(GROUPS)]
            chunks.append(jnp.concatenate(cols, axis=1) + bias_ref[mode])
        mixed = chunks[0] if len(chunks) == 1 else jnp.concatenate(chunks, axis=0)
        gated = (u * mixed).astype(BF16)
        return x + _dot(gated, wout_bf[...]), v

    @pl.when(i < n_prompt)
    def _():
        op_ref[...] = mixer(xp_ref[...], 0)[0]

    @pl.when(i == n_prompt)
    def _():
        out, v = mixer(xs_ref[...], 1)
        os_ref[...] = out
        vs_ref[...] = v


def _gmlp(xp, xs, g_norm, w_in, ln_g, ln_b, w_s, b_s, w_out):
    tp, d = xp.shape
    ns = xs.shape[0]
    width = w_out.shape[0]
    gdim = width // GROUPS
    assert tp % TOKEN_TILE == 0 and TOKEN_TILE % CHUNK == 0 and ns == CHUNK
    n_prompt = tp // TOKEN_TILE
    causal = jnp.tril(jnp.ones((CHUNK, CHUNK), dtype=bool))
    w_prompt = jnp.where(causal[None], w_s, 0.0)
    w_sample = w_s[:, 0, 0][:, None, None] * jnp.eye(CHUNK, dtype=F32)[None]
    wmix = jnp.stack([w_prompt, w_sample]).astype(BF16)
    b_prompt = jnp.repeat(jnp.transpose(b_s), gdim, axis=1)
    b_sample = jnp.broadcast_to(jnp.repeat(b_s[:, 0], gdim)[None], (CHUNK, width))
    bias = jnp.stack([b_prompt, b_sample])

    tile = lambda i: (jnp.minimum(i, n_prompt - 1), 0)
    row = lambda a: a.reshape(1, -1)
    return pl.pallas_call(
        functools.partial(_gmlp_kernel, n_prompt=n_prompt),
        grid=(n_prompt + 1,),
        in_specs=[pl.BlockSpec((TOKEN_TILE, d), tile),
                  _resident((ns, d)), _resident((1, d)), _resident(w_in.shape),
                  _resident((1, width)), _resident((1, width)),
                  _resident(wmix.shape), _resident(bias.shape), _resident(w_out.shape)],
        out_specs=(pl.BlockSpec((TOKEN_TILE, d), tile),
                   pl.BlockSpec((ns, d), lambda i: (0, 0)),
                   pl.BlockSpec((ns, width), lambda i: (0, 0))),
        out_shape=(jax.ShapeDtypeStruct((tp, d), F32),
                   jax.ShapeDtypeStruct((ns, d), F32),
                   jax.ShapeDtypeStruct((ns, width), F32)),
        scratch_shapes=[pltpu.VMEM(w_in.shape, BF16), pltpu.VMEM(w_out.shape, BF16)],
        compiler_params=_params(1),
        name="gmlp_mixer",
    )(xp, xs, row(g_norm), w_in, row(ln_g), row(ln_b), wmix, bias, w_out)


def _softmax_rows(s, axis):
    m = jnp.max(s, axis=axis, keepdims=True)
    e = jnp.exp(s - m)
    return e / jnp.sum(e, axis=axis, keepdims=True)


def _xattn_kernel(xp_ref, xs_ref, gn_ref, wq_ref, wo_ref, k_ref, v_ref,
                  op_ref, qs_ref, wq_bf, wo_bf, *, n_prompt):
    i = pl.program_id(0)
    d = wq_ref.shape[-1]
    hd = d // HEADS
    scale = hd ** -0.5

    @pl.when(i == 0)
    def _():
        wq_bf[...] = wq_ref[0].astype(BF16)
        wo_bf[...] = wo_ref[0].astype(BF16)

    @pl.when(i < n_prompt)
    def _():
        x = xp_ref[...]
        q = _dot(_rmsnorm(x, gn_ref[0]).astype(BF16), wq_bf[...])
        heads = []
        for h in range(HEADS):
            cols = slice(h * hd, (h + 1) * hd)
            qh = q[:, cols].astype(BF16)
            kh = k_ref[0, 0, :, cols].astype(BF16)
            s = lax.dot_general(qh, kh, (((1,), (1,)), ((), ())),
                                preferred_element_type=F32) * scale
            p = _softmax_rows(s, -1).astype(BF16)
            heads.append(_dot(p, v_ref[0, 0, :, cols].astype(BF16)))
        o = jnp.concatenate(heads, axis=1).astype(BF16)
        op_ref[...] = x + _dot(o, wo_bf[...])

    @pl.when(i == n_prompt)
    def _():
        qs_ref[...] = _dot(_rmsnorm(xs_ref[...], gn_ref[0]).astype(BF16), wq_bf[...])


def _xattn_prompt(xp, xs, norm_xattn, w_xq, w_xo, mem_k, mem_v, layer):
    tp, d = xp.shape
    ns = xs.shape[0]
    depth, nb, n_mem, _ = mem_k.shape
    n_prompt = tp // TOKEN_TILE
    tiles_per_seq = n_prompt // nb
    assert tiles_per_seq * nb == n_prompt
    tile = lambda i: (jnp.minimum(i, n_prompt - 1), 0)
    kv = lambda i: (layer, jnp.minimum(i, n_prompt - 1) // tiles_per_seq, 0, 0)
    of_layer = lambda shape: pl.BlockSpec((1,) + shape, lambda i: (layer,) + (0,) * len(shape),
                                          pipeline_mode=pl.Buffered(1))
    return pl.pallas_call(
        functools.partial(_xattn_kernel, n_prompt=n_prompt),
        grid=(n_prompt + 1,),
        in_specs=[pl.BlockSpec((TOKEN_TILE, d), tile),
                  _resident((ns, d)), of_layer((1, d)), of_layer((d, d)), of_layer((d, d)),
                  pl.BlockSpec((1, 1, n_mem, d), kv), pl.BlockSpec((1, 1, n_mem, d), kv)],
        out_specs=(pl.BlockSpec((TOKEN_TILE, d), tile),
                   pl.BlockSpec((ns, d), lambda i: (0, 0))),
        out_shape=(jax.ShapeDtypeStruct((tp, d), F32), jax.ShapeDtypeStruct((ns, d), F32)),
        scratch_shapes=[pltpu.VMEM((d, d), BF16), pltpu.VMEM((d, d), BF16)],
        compiler_params=_params(1),
        name="xattn_prompt",
    )(xp, xs, norm_xattn.reshape(depth, 1, d), w_xq, w_xo, mem_k, mem_v)


def _xattn_sample_kernel(q_ref, k_ref, v_ref, o_ref, *, block, scale):
    i = pl.program_id(0)
    for b in range(block):
        r = i * block + b
        s = jnp.sum(k_ref[0, b] * q_ref[r][None], axis=-1, keepdims=True) * scale
        p = _softmax_rows(s, 0)
        o_ref[r] = jnp.sum(p * v_ref[0, b], axis=0)


def _xattn_sample(qs, cache_k, cache_v, layer):
    _, ns, n_mem, heads, hd = cache_k.shape
    block = SAMPLE_ATTN_BLOCK
    assert ns % block == 0
    kv = pl.BlockSpec((1, block, n_mem, heads, hd), lambda i: (layer, i, 0, 0, 0))
    return pl.pallas_call(
        functools.partial(_xattn_sample_kernel, block=block, scale=hd ** -0.5),
        grid=(ns // block,),
        in_specs=[_resident((ns, heads, hd)), kv, kv],
        out_specs=pl.BlockSpec((ns, heads, hd), lambda i: (0, 0, 0)),
        out_shape=jax.ShapeDtypeStruct((ns, heads, hd), F32),
        compiler_params=_params(1),
        name="xattn_sample",
    )(qs.reshape(ns, heads, hd), cache_k, cache_v)


def _attn_out_kernel(o_ref, xs_ref, wo_ref, os_ref):
    os_ref[...] = xs_ref[...] + _dot(o_ref[...].astype(BF16), wo_ref[0].astype(BF16))


def _attn_out_sample(o, xs, w_xo, layer):
    ns, d = xs.shape
    return pl.pallas_call(
        _attn_out_kernel,
        grid=(1,),
        in_specs=[_resident((ns, d)), _resident((ns, d)),
                  pl.BlockSpec((1, d, d), lambda i: (layer, 0, 0))],
        out_specs=pl.BlockSpec((ns, d), lambda i: (0, 0)),
        out_shape=jax.ShapeDtypeStruct((ns, d), F32),
        compiler_params=_params(1),
        name="attn_out_sample",
    )(o, xs, w_xo)


def _ffn_kernel(xp_ref, xs_ref, gn_ref, wg_ref, wu_ref, wd_ref, op_ref, os_ref, *, n_prompt):
    i = pl.program_id(0)
    f_dim = wg_ref.shape[1]

    def ffn(x):
        xn = _rmsnorm(x, gn_ref[...]).astype(BF16)
        acc = x
        for f in range(0, f_dim, FFN_F_CHUNK):
            cols = slice(f, f + FFN_F_CHUNK)
            hid = jax.nn.silu(_dot(xn, wg_ref[:, cols])) * _dot(xn, wu_ref[:, cols])
            acc = acc + _dot(hid.astype(BF16), wd_ref[cols, :])
        return acc

    @pl.when(i < n_prompt)
    def _():
        op_ref[...] = ffn(xp_ref[...])

    @pl.when(i == n_prompt)
    def _():
        os_ref[...] = ffn(xs_ref[...])


def _ffn(xp, xs, g_norm, w_gate, w_up, w_down):
    tp, d = xp.shape
    ns = xs.shape[0]
    f_dim = w_gate.shape[1]
    assert f_dim % FFN_F_CHUNK == 0
    n_prompt = tp // TOKEN_TILE
    wg = _cast_bf16(w_gate, d // 4)
    wu = _cast_bf16(w_up, d // 4)
    wd = _cast_bf16(w_down, f_dim // 4)
    tile = lambda i: (jnp.minimum(i, n_prompt - 1), 0)
    return pl.pallas_call(
        functools.partial(_ffn_kernel, n_prompt=n_prompt),
        grid=(n_prompt + 1,),
        in_specs=[pl.BlockSpec((TOKEN_TILE, d), tile),
                  _resident((ns, d)), _resident((1, d)),
                  _resident(wg.shape), _resident(wu.shape), _resident(wd.shape)],
        out_specs=(pl.BlockSpec((TOKEN_TILE, d), tile),
                   pl.BlockSpec((ns, d), lambda i: (0, 0))),
        out_shape=(jax.ShapeDtypeStruct((tp, d), F32), jax.ShapeDtypeStruct((ns, d), F32)),
        compiler_params=_params(1),
        name="dense_swiglu",
    )(xp, xs, g_norm.reshape(1, d), wg, wu, wd)


CONV_HALO = 32


def _conv_prompt_kernel(xp_ref, gn_ref, w1_ref, wdw_ref, bdw_ref, lng_ref, lnb_ref, w2_ref,
                        op_ref, st_ref, w1_bf, w2_bf, cbuf, *, tiles_per_seq):
    i = pl.program_id(0)
    d = w2_ref.shape[0]
    taps = wdw_ref.shape[0]
    tm = xp_ref.shape[0]

    @pl.when(i == 0)
    def _():
        w1_bf[...] = w1_ref[...].astype(BF16)
        w2_bf[...] = w2_ref[...].astype(BF16)

    @pl.when(i % tiles_per_seq == 0)
    def _():
        cbuf[0:CONV_HALO, :] = jnp.zeros((CONV_HALO, d), F32)

    x = xp_ref[...]
    ag = _dot(_rmsnorm(x, gn_ref[...]).astype(BF16), w1_bf[...])
    c = ag[:, :d] * jax.nn.sigmoid(ag[:, d:])
    cbuf[CONV_HALO:CONV_HALO + tm, :] = c
    first = CONV_HALO - (taps - 1)
    y = jnp.zeros((tm, d), F32) + bdw_ref[...]
    for k in range(taps):
        y = y + wdw_ref[k:k + 1, :] * cbuf[first + k:first + k + tm, :]
    t = jax.nn.silu(_layernorm(y, lng_ref[...], lnb_ref[...])).astype(BF16)
    op_ref[...] = x + _dot(t, w2_bf[...])
    cbuf[0:CONV_HALO, :] = cbuf[tm:tm + CONV_HALO, :]

    @pl.when(i % tiles_per_seq == tiles_per_seq - 1)
    def _():
        st_ref[0] = cbuf[first:CONV_HALO, :]


def _conv_prompt(xp, n_seq, g_norm, w_pw1, w_dw, b_dw, ln_g, ln_b, w_pw2):
    tp, d = xp.shape
    taps = w_dw.shape[0]
    n_prompt = tp // TOKEN_TILE
    tiles_per_seq = n_prompt // n_seq
    assert tiles_per_seq * n_seq == n_prompt and taps - 1 <= CONV_HALO <= TOKEN_TILE
    row = lambda a: a.reshape(1, -1)
    return pl.pallas_call(
        functools.partial(_conv_prompt_kernel, tiles_per_seq=tiles_per_seq),
        grid=(n_prompt,),
        in_specs=[pl.BlockSpec((TOKEN_TILE, d), lambda i: (i, 0)),
                  _resident((1, d)), _resident(w_pw1.shape), _resident(w_dw.shape),
                  _resident((1, d)), _resident((1, d)), _resident((1, d)), _resident(w_pw2.shape)],
        out_specs=(pl.BlockSpec((TOKEN_TILE, d), lambda i: (i, 0)),
                   pl.BlockSpec((1, taps - 1, d), lambda i: (i // tiles_per_seq, 0, 0))),
        out_shape=(jax.ShapeDtypeStruct((tp, d), F32),
                   jax.ShapeDtypeStruct((n_seq, taps - 1, d), F32)),
        scratch_shapes=[pltpu.VMEM(w_pw1.shape, BF16), pltpu.VMEM(w_pw2.shape, BF16),
                        pltpu.VMEM((TOKEN_TILE + CONV_HALO, d), F32)],
        compiler_params=_params(1),
        name="conv_prompt",
    )(xp, row(g_norm), w_pw1, w_dw, row(b_dw), row(ln_g), row(ln_b), w_pw2)


def _conv_sample_kernel(xs_ref, gn_ref, w1_ref, wdw_ref, bdw_ref, lng_ref, lnb_ref, w2_ref, st_ref,
                        os_ref, sto_ref, c_all, t_all, *, block):
    i = pl.program_id(0)
    d = w2_ref.shape[0]
    taps = wdw_ref.shape[0]

    @pl.when(i == 0)
    def _():
        ag = _dot(_rmsnorm(xs_ref[...], gn_ref[...]).astype(BF16), w1_ref[...].astype(BF16))
        c_all[...] = ag[:, :d] * jax.nn.sigmoid(ag[:, d:])

    w_hist = wdw_ref[0:taps - 1, :]
    w_last = wdw_ref[taps - 1:taps, :]
    for b in range(block):
        r = i * block + b
        hist = st_ref[0, b]
        cb = c_all[pl.ds(r, 1), :]
        y = jnp.sum(hist * w_hist, axis=0, keepdims=True) + cb * w_last + bdw_ref[...]
        t_all[pl.ds(r, 1), :] = jax.nn.silu(_layernorm(y, lng_ref[...], lnb_ref[...]))
        sto_ref[0, b, 0:taps - 2, :] = hist[1:taps - 1, :]
        sto_ref[0, b, taps - 2:taps - 1, :] = cb

    @pl.when(i == pl.num_programs(0) - 1)
    def _():
        os_ref[...] = xs_ref[...] + _dot(t_all[...].astype(BF16), w2_ref[...].astype(BF16))


def _conv_sample(xs, state, g_norm, w_pw1, w_dw, b_dw, ln_g, ln_b, w_pw2):
    ns, d = xs.shape
    taps = w_dw.shape[0]
    block = SAMPLE_CONV_BLOCK
    assert ns % block == 0
    row = lambda a: a.reshape(1, -1)
    st = pl.BlockSpec((1, block, taps - 1, d), lambda i: (0, i, 0, 0))
    return pl.pallas_call(
        functools.partial(_conv_sample_kernel, block=block),
        grid=(ns // block,),
        in_specs=[_resident((ns, d)), _resident((1, d)), _resident(w_pw1.shape),
                  _resident(w_dw.shape), _resident((1, d)), _resident((1, d)), _resident((1, d)),
                  _resident(w_pw2.shape), st],
        out_specs=(pl.BlockSpec((ns, d), lambda i: (0, 0)), st),
        out_shape=(jax.ShapeDtypeStruct((ns, d), F32), jax.ShapeDtypeStruct(state.shape, F32)),
        scratch_shapes=[pltpu.VMEM((ns, d), F32), pltpu.VMEM((ns, d), F32)],
        compiler_params=_params(1),
        name="conv_sample",
    )(xs, row(g_norm), w_pw1, w_dw, row(b_dw), row(ln_g), row(ln_b), w_pw2, state)


def _router_kernel(xp_ref, xs_ref, gn_ref, wr_ref, hn_ref, experts_ref, gates_ref, *, n_prompt, n_experts):
    i = pl.program_id(0)

    def route(x):
        xn = _rmsnorm(x, gn_ref[...])
        logits = jnp.dot(xn, wr_ref[...], precision=lax.Precision.HIGHEST,
                         preferred_element_type=F32)
        lane = lax.broadcasted_iota(jnp.int32, logits.shape, 1).astype(F32)
        neg = jnp.float32(-jnp.inf)
        logits = jnp.where(lane < n_experts, logits, neg)
        v1 = jnp.max(logits, axis=-1, keepdims=True)
        i1 = jnp.min(jnp.where(logits == v1, lane, float(LANES)), axis=-1, keepdims=True)
        rest = jnp.where(lane == i1, neg, logits)
        v2 = jnp.max(rest, axis=-1, keepdims=True)
        i2 = jnp.min(jnp.where(rest == v2, lane, float(LANES)), axis=-1, keepdims=True)
        e2 = jnp.exp(v2 - v1)
        denom = 1.0 + e2
        experts = jnp.where(lane == 0.0, i1, jnp.where(lane == 1.0, i2, 0.0)).astype(jnp.int32)
        gates = jnp.where(lane == 0.0, 1.0 / denom, jnp.where(lane == 1.0, e2 / denom, 0.0))
        return xn.astype(BF16), experts, gates

    @pl.when(i < n_prompt)
    def _():
        hn, experts, gates = route(xp_ref[...])
        hn_ref[...] = hn
        experts_ref[...] = experts
        gates_ref[...] = gates

    @pl.when(i == n_prompt)
    def _():
        ns = xs_ref.shape[0]
        hn, experts, gates = route(xs_ref[...])
        hn_ref[...] = jnp.zeros(hn_ref.shape, BF16)
        experts_ref[...] = jnp.zeros(experts_ref.shape, jnp.int32)
        gates_ref[...] = jnp.zeros(gates_ref.shape, F32)
        hn_ref[0:ns, :] = hn
        experts_ref[0:ns, :] = experts
        gates_ref[0:ns, :] = gates


def _router(xp, xs, g_norm, w_router):
    tp, d = xp.shape
    ns = xs.shape[0]
    n_experts = w_router.shape[1]
    n_prompt = tp // TOKEN_TILE
    t_pad = tp + TOKEN_TILE
    wr = jnp.pad(w_router, ((0, 0), (0, LANES - n_experts)))
    tile = lambda i: (jnp.minimum(i, n_prompt - 1), 0)
    lanes_spec = pl.BlockSpec((TOKEN_TILE, LANES), lambda i: (i, 0))
    return pl.pallas_call(
        functools.partial(_router_kernel, n_prompt=n_prompt, n_experts=n_experts),
        grid=(n_prompt + 1,),
        in_specs=[pl.BlockSpec((TOKEN_TILE, d), tile),
                  _resident((ns, d)), _resident((1, d)), _resident((d, LANES))],
        out_specs=(pl.BlockSpec((TOKEN_TILE, d), lambda i: (i, 0)), lanes_spec, lanes_spec),
        out_shape=(jax.ShapeDtypeStruct((t_pad, d), BF16),
                   jax.ShapeDtypeStruct((t_pad, LANES), jnp.int32),
                   jax.ShapeDtypeStruct((t_pad, LANES), F32)),
        compiler_params=_params(1),
        name="moe_router",
    )(xp, xs, g_norm.reshape(1, d), wr)


def _lane_tile(a, width):
    return jnp.concatenate([a] * (width // LANES), axis=1)


def _dispatch_kernel(slot_nsub, win_lo, win_hi, src_ref, hn_ref, x_ref):
    s = pl.program_id(0)
    n_sub = slot_nsub[s]
    subs_per_slot = SLOT_ROWS // SUB_ROWS
    d = hn_ref.shape[1]
    lane = lax.broadcasted_iota(jnp.int32, (SUB_ROWS, DISPATCH_WIN), 1)

    def sub_block(j, carry):
        rows = pl.ds(pl.multiple_of(j * SUB_ROWS, SUB_ROWS), SUB_ROWS)
        src = _lane_tile(src_ref[rows, :], DISPATCH_WIN)

        def window(w, acc):
            base = pl.multiple_of(w * DISPATCH_WIN, DISPATCH_WIN)
            sel = jnp.where(src == lane + base, 1.0, 0.0).astype(BF16)
            return acc + _dot(sel, hn_ref[pl.ds(base, DISPATCH_WIN), :])

        k = s * subs_per_slot + j
        acc = lax.fori_loop(win_lo[k], win_hi[k] + 1, window, jnp.zeros((SUB_ROWS, d), F32))
        x_ref[rows, :] = acc.astype(BF16)
        return carry

    def zero_block(j, carry):
        rows = pl.ds(pl.multiple_of(j * SUB_ROWS, SUB_ROWS), SUB_ROWS)
        x_ref[rows, :] = jnp.zeros((SUB_ROWS, d), BF16)
        return carry

    lax.fori_loop(0, n_sub, sub_block, 0)
    lax.fori_loop(n_sub, subs_per_slot, zero_block, 0)


def _dispatch(hn, src_rep, slot_nsub, win_lo, win_hi):
    t_pad, d = hn.shape
    n_slots = slot_nsub.shape[0]
    assert t_pad % DISPATCH_WIN == 0
    grid_spec = pltpu.PrefetchScalarGridSpec(
        num_scalar_prefetch=3,
        grid=(n_slots,),
        in_specs=[pl.BlockSpec((SLOT_ROWS, LANES), lambda s, *_: (s, 0)),
                  _resident((t_pad, d))],
        out_specs=pl.BlockSpec((SLOT_ROWS, d), lambda s, *_: (s, 0)),
    )
    return pl.pallas_call(
        _dispatch_kernel,
        grid_spec=grid_spec,
        out_shape=jax.ShapeDtypeStruct((n_slots * SLOT_ROWS, d), BF16),
        compiler_params=_params(1),
        name="moe_dispatch",
    )(slot_nsub, win_lo, win_hi, src_rep, hn)


def _expert_kernel(slot_expert, slot_nsub, x_ref, gate_ref, wg_ref, wu_ref, wd_ref, y_ref,
                   wg_bf, wu_bf, wd_bf, acc):
    s = pl.program_id(0)
    f = pl.program_id(1)
    n_sub = slot_nsub[s]
    last_f = f == pl.num_programs(1) - 1
    d = y_ref.shape[1]

    @pl.when(n_sub > 0)
    def _():
        wg_bf[...] = wg_ref[0].astype(BF16)
        wu_bf[...] = wu_ref[0].astype(BF16)
        wd_bf[...] = wd_ref[0].astype(BF16)

    def sub_block(j, carry):
        rows = pl.ds(pl.multiple_of(j * SUB_ROWS, SUB_ROWS), SUB_ROWS)
        xg = x_ref[rows, :]
        hid = jax.nn.silu(_dot(xg, wg_bf[...])) * _dot(xg, wu_bf[...])
        part = _dot(hid.astype(BF16), wd_bf[...])

        @pl.when(f == 0)
        def _():
            acc[rows, :] = part

        @pl.when(f > 0)
        def _():
            acc[rows, :] += part

        @pl.when(last_f)
        def _():
            y_ref[rows, :] = (acc[rows, :] * _lane_tile(gate_ref[rows, :], d)).astype(y_ref.dtype)

        return carry

    def zero_block(j, carry):
        rows = pl.ds(pl.multiple_of(j * SUB_ROWS, SUB_ROWS), SUB_ROWS)
        y_ref[rows, :] = jnp.zeros((SUB_ROWS, d), y_ref.dtype)
        return carry

    lax.fori_loop(0, n_sub, sub_block, 0)

    @pl.when(last_f)
    def _():
        lax.fori_loop(n_sub, SLOT_ROWS // SUB_ROWS, zero_block, 0)


def _expert_ffn(x_slots, gate_rep, slot_expert, slot_nsub, w_gate, w_up, w_down):
    n_slots = slot_expert.shape[0]
    d = x_slots.shape[1]
    f_dim = w_gate.shape[2]
    tf = EXPERT_F_TILE
    assert f_dim % tf == 0 and SLOT_ROWS % SUB_ROWS == 0
    n_f = f_dim // tf
    f_tile = lambda s, f, sn: jnp.where(sn[s] > 0, f, n_f - 1)
    grid_spec = pltpu.PrefetchScalarGridSpec(
        num_scalar_prefetch=2,
        grid=(n_slots, f_dim // tf),
        in_specs=[pl.BlockSpec((SLOT_ROWS, d), lambda s, f, se, sn: (s, 0)),
                  pl.BlockSpec((SLOT_ROWS, LANES), lambda s, f, se, sn: (s, 0)),
                  pl.BlockSpec((1, d, tf), lambda s, f, se, sn: (se[s], 0, f_tile(s, f, sn))),
                  pl.BlockSpec((1, d, tf), lambda s, f, se, sn: (se[s], 0, f_tile(s, f, sn))),
                  pl.BlockSpec((1, tf, d), lambda s, f, se, sn: (se[s], f_tile(s, f, sn), 0))],
        out_specs=pl.BlockSpec((SLOT_ROWS, d), lambda s, f, se, sn: (s, 0)),
        scratch_shapes=[pltpu.VMEM((d, tf), BF16), pltpu.VMEM((d, tf), BF16),
                        pltpu.VMEM((tf, d), BF16), pltpu.VMEM((SLOT_ROWS, d), F32)],
    )
    return pl.pallas_call(
        _expert_kernel,
        grid_spec=grid_spec,
        out_shape=jax.ShapeDtypeStruct((n_slots * SLOT_ROWS, d), BF16),
        compiler_params=_params(2),
        name="expert_swiglu",
    )(slot_expert, slot_nsub, x_slots, gate_rep, w_gate, w_up, w_down)


def _combine_kernel(n_blocks, block_ids, xp_ref, xs_ref, pos1_ref, pos2_ref, gf_ref, *rest,
                    n_prompt, max_blocks):
    y_refs = rest[:max_blocks]
    op_ref, os_ref = rest[max_blocks:]
    i = pl.program_id(0)

    def combine(x, o_ref):
        rows = x.shape[0]
        o_ref[...] = x
        p1 = _lane_tile(pos1_ref[0:rows, :], COMBINE_BLOCK)
        p2 = _lane_tile(pos2_ref[0:rows, :], COMBINE_BLOCK)
        lane = lax.broadcasted_iota(jnp.int32, (rows, COMBINE_BLOCK), 1)
        for b in range(max_blocks):
            @pl.when(b < n_blocks[i])
            def _():
                row_id = lane + block_ids[i * max_blocks + b] * COMBINE_BLOCK
                sel = jnp.where(p1 == row_id, 1.0, jnp.where(p2 == row_id, 1.0, 0.0)).astype(BF16)
                o_ref[...] += _dot(sel, y_refs[b][...])
        o_ref[...] = _rmsnorm(o_ref[...], gf_ref[...])

    @pl.when(i < n_prompt)
    def _():
        combine(xp_ref[...], op_ref)

    @pl.when(i == n_prompt)
    def _():
        combine(xs_ref[...], os_ref)


def _combine(xp, xs, pos_rep, y, n_blocks, block_ids, g_final, max_blocks):
    tp, d = xp.shape
    ns = xs.shape[0]
    n_prompt = tp // TOKEN_TILE
    tile = lambda i, *_: (jnp.minimum(i, n_prompt - 1), 0)
    pos_spec = pl.BlockSpec((TOKEN_TILE, LANES), lambda i, *_: (i, 0))

    def y_spec(b):
        return pl.BlockSpec((COMBINE_BLOCK, d), lambda i, nb, ids: (ids[i * max_blocks + b], 0))

    grid_spec = pltpu.PrefetchScalarGridSpec(
        num_scalar_prefetch=2,
        grid=(n_prompt + 1,),
        in_specs=[pl.BlockSpec((TOKEN_TILE, d), tile),
                  pl.BlockSpec((ns, d), lambda i, *_: (0, 0)),
                  pos_spec, pos_spec,
                  pl.BlockSpec((1, d), lambda i, *_: (0, 0))] + [y_spec(b) for b in range(max_blocks)],
        out_specs=(pl.BlockSpec((TOKEN_TILE, d), tile),
                   pl.BlockSpec((ns, d), lambda i, *_: (0, 0))),
    )
    return pl.pallas_call(
        functools.partial(_combine_kernel, n_prompt=n_prompt, max_blocks=max_blocks),
        grid_spec=grid_spec,
        out_shape=(jax.ShapeDtypeStruct((tp, d), F32), jax.ShapeDtypeStruct((ns, d), F32)),
        compiler_params=_params(1),
        name="moe_combine",
    )(n_blocks, block_ids, xp, xs, pos_rep[0], pos_rep[1], g_final.reshape(1, d), *([y] * max_blocks))


def _routing_tables(experts, gates, t_valid, t_pad, n_experts):
    i32 = jnp.int32
    n_entries = t_valid * TOP_K
    n_slots = n_entries // SLOT_ROWS + n_experts
    e_flat = experts[:t_valid, :TOP_K].reshape(-1)
    w_flat = gates[:t_valid, :TOP_K].reshape(-1)
    tok_flat = jnp.arange(n_entries, dtype=i32) // TOP_K
    onehot = (e_flat[:, None] == jnp.arange(n_experts, dtype=i32)[None, :]).astype(i32)
    cnt = jnp.sum(onehot, axis=0)
    rank = jnp.sum((jnp.cumsum(onehot, axis=0) - onehot) * onehot, axis=1)
    n_chunks = (cnt + SLOT_ROWS - 1) // SLOT_ROWS
    per_chunk = (cnt + jnp.maximum(n_chunks, 1) - 1) // jnp.maximum(n_chunks, 1)
    chunk_rows = jnp.maximum((per_chunk + SUB_ROWS - 1) // SUB_ROWS * SUB_ROWS, SUB_ROWS)
    chunk_end = jnp.cumsum(n_chunks)
    slot_base = chunk_end - n_chunks
    n_used = chunk_end[-1]

    cr = chunk_rows[e_flat]
    c = rank // cr
    pos = (slot_base[e_flat] + c) * SLOT_ROWS + (rank - c * cr)

    sid = jnp.arange(n_slots, dtype=i32)
    expert_of = lambda s: jnp.sum((s[..., None] >= chunk_end).astype(i32), axis=-1)
    slot_expert = jnp.where(sid < n_used, expert_of(sid), expert_of(n_used - 1)).astype(i32)
    c_in = sid - slot_base[slot_expert]
    rows = jnp.clip(cnt[slot_expert] - c_in * chunk_rows[slot_expert], 0, chunk_rows[slot_expert])
    rows = jnp.where(sid < n_used, rows, 0)
    slot_nsub = ((rows + SUB_ROWS - 1) // SUB_ROWS).astype(i32)

    n_rows = n_slots * SLOT_ROWS
    src = jnp.full((n_rows,), -1, i32).at[pos].set(tok_flat, unique_indices=True)
    gate_rows = jnp.zeros((n_rows,), F32).at[pos].set(w_flat, unique_indices=True)

    src2 = src.reshape(-1, SUB_ROWS)
    win_lo = jnp.min(jnp.where(src2 >= 0, src2, t_pad), axis=1) // DISPATCH_WIN
    win_hi = jnp.max(src2, axis=1) // DISPATCH_WIN

    pos_pad = jnp.full((t_pad, TOP_K), -1, i32).at[:t_valid].set(pos.reshape(t_valid, TOP_K))
    n_tiles = t_pad // TOKEN_TILE
    n_yblocks = n_rows // COMBINE_BLOCK
    max_blocks = min(n_yblocks, TOKEN_TILE * TOP_K // COMBINE_BLOCK + 2 * n_experts)
    blk = jnp.where(pos_pad >= 0, pos_pad // COMBINE_BLOCK, -1).reshape(n_tiles, -1)
    present = jnp.any(blk[:, :, None] == jnp.arange(n_yblocks, dtype=i32)[None, None, :], axis=1)
    n_blocks = jnp.sum(present.astype(i32), axis=1)
    ids = jnp.argsort(jnp.logical_not(present), axis=1, stable=True)[:, :max_blocks].astype(i32)
    valid = jnp.arange(max_blocks, dtype=i32)[None, :] < n_blocks[:, None]
    last = lax.cummax(jnp.where(valid, jnp.arange(n_tiles, dtype=i32)[:, None], 0), axis=0)
    block_ids = jnp.take_along_axis(ids, last, axis=0).reshape(-1)

    rep = lambda a: jnp.broadcast_to(a[:, None], (a.shape[0], LANES))
    pos_rep = (rep(pos_pad[:, 0]), rep(pos_pad[:, 1]))
    return dict(slot_expert=slot_expert, slot_nsub=slot_nsub, src_rep=rep(src), gate_rep=rep(gate_rows),
                win_lo=win_lo.astype(i32), win_hi=win_hi.astype(i32), pos_rep=pos_rep,
                n_blocks=n_blocks.astype(i32), block_ids=block_ids, max_blocks=max_blocks)


def _moe(xp, xs, g_norm, w_router, w_gate, w_up, w_down, g_final):
    tp, d = xp.shape
    ns = xs.shape[0]
    n_experts = w_router.shape[1]
    assert ns <= TOKEN_TILE
    hn, experts, gates = _router(xp, xs, g_norm, w_router)
    t = _routing_tables(experts, gates, tp + ns, hn.shape[0], n_experts)
    x_slots = _dispatch(hn, t["src_rep"], t["slot_nsub"], t["win_lo"], t["win_hi"])
    y = _expert_ffn(x_slots, t["gate_rep"], t["slot_expert"], t["slot_nsub"], w_gate, w_up, w_down)
    return _combine(xp, xs, t["pos_rep"], y, t["n_blocks"], t["block_ids"], g_final, t["max_blocks"])


def kernel(x_prompt, x_sample, mem_prompt, cache_mem_k, cache_mem_v, state_conv, norm_mix, norm_xattn, norm_ffn, norm_mem, norm_final, w_xq, w_xk, w_xv, w_xo, a_w_in, a_ln_g, a_ln_b, a_w_s, a_b_s, a_w_out, b_w_pw1, b_w_dw, b_b_dw, b_ln_g, b_ln_b, b_w_pw2, ffn_w_gate, ffn_w_up, ffn_w_down, moe_w_router, moe_w_gate, moe_w_up, moe_w_down):
    nb, seq, d = x_prompt.shape
    ns = x_sample.shape[0]
    depth = norm_mix.shape[0]
    n_mem = mem_prompt.shape[1]
    assert depth == 2 and x_sample.shape[1] == 1

    mem_k, mem_v, mem_k_heads, mem_v_heads = _memory_kv(mem_prompt, norm_mem, w_xk, w_xv)

    hp = x_prompt.reshape(nb * seq, d)
    hs = x_sample.reshape(ns, d)

    def cross_attention(hp, hs, layer):
        hp_new, qs = _xattn_prompt(hp, hs, norm_xattn, w_xq, w_xo, mem_k, mem_v, layer)
        o = _xattn_sample(qs, cache_mem_k, cache_mem_v, layer)
        hs_new = _attn_out_sample(o.reshape(ns, d), hs, w_xo, layer)
        return hp_new, hs_new

    hp, hs, v_sample = _gmlp(hp, hs, norm_mix[0], a_w_in[0], a_ln_g[0], a_ln_b[0],
                             a_w_s[0], a_b_s[0], a_w_out[0])
    hp, hs = cross_attention(hp, hs, 0)
    hp, hs = _ffn(hp, hs, norm_ffn[0], ffn_w_gate[0], ffn_w_up[0], ffn_w_down[0])

    conv_w = (norm_mix[1], b_w_pw1[0], b_w_dw[0], b_b_dw[0], b_ln_g[0], b_ln_b[0], b_w_pw2[0])
    hp, conv_state_prompt = _conv_prompt(hp, nb, *conv_w)
    hs, conv_state_sample = _conv_sample(hs, state_conv, *conv_w)
    hp, hs = cross_attention(hp, hs, 1)
    yp, ys = _moe(hp, hs, norm_ffn[1], moe_w_router[0], moe_w_gate[0], moe_w_up[0], moe_w_down[0],
                  norm_final)

    return (yp.reshape(nb, seq, d),
            ys.reshape(ns, 1, d),
            mem_k_heads,
            mem_v_heads,
            conv_state_prompt[None],
            conv_state_sample,
            v_sample.reshape(1, ns, 1, -1))
```

```python
import functools

import jax
import jax.numpy as jnp
from jax import lax
from jax.experimental import pallas as pl
from jax.experimental.pallas import tpu as pltpu

F32 = jnp.float32
BF16 = jnp.bfloat16

RMS_EPS = 1e-6
LN_EPS = 1e-5
CHUNK = 128
GROUPS = 8
HEADS = 4
TOP_K = 2
LANES = 128
V7X_VMEM_LIMIT = 56 * 1024 * 1024

TOKEN_TILE = 512
SAMPLE_ATTN_BLOCK = 8
SAMPLE_CONV_BLOCK = 16
SLOT_ROWS = 2304
SUB_ROWS = 256
EXPERT_F_TILE = 512
DISPATCH_WIN = 256
DISPATCH_SPAN = 5
COMBINE_BLOCK = 256
COMBINE_ALWAYS = 12
COMBINE_GROUP = 4
FFN_F_CHUNK = 256


def _params(n_axes, vmem=V7X_VMEM_LIMIT):
    return pltpu.CompilerParams(dimension_semantics=("arbitrary",) * n_axes,
                                vmem_limit_bytes=vmem)


def _resident(shape):
    nd = len(shape)
    return pl.BlockSpec(shape, lambda *_: (0,) * nd, pipeline_mode=pl.Buffered(1))


def _rmsnorm(x, g):
    return x * lax.rsqrt(jnp.mean(x * x, axis=-1, keepdims=True) + RMS_EPS) * g


def _layernorm(x, g, b):
    xc = x - jnp.mean(x, axis=-1, keepdims=True)
    var = jnp.mean(xc * xc, axis=-1, keepdims=True)
    return xc * lax.rsqrt(var + LN_EPS) * g + b


def _dot(a, b):
    return jnp.dot(a, b, preferred_element_type=F32)


def _cast_kernel(x_ref, o_ref):
    o_ref[...] = x_ref[...].astype(o_ref.dtype)


def _cast_bf16(w, rows_per_step):
    r, c = w.shape
    return pl.pallas_call(
        _cast_kernel,
        grid=(r // rows_per_step,),
        in_specs=[pl.BlockSpec((rows_per_step, c), lambda i: (i, 0))],
        out_specs=pl.BlockSpec((rows_per_step, c), lambda i: (i, 0)),
        out_shape=jax.ShapeDtypeStruct((r, c), BF16),
        compiler_params=_params(1),
        name="cast_bf16",
    )(w)


def _memkv_kernel(mem_ref, g_ref, wk_ref, wv_ref, k_ref, v_ref, kh_ref, vh_ref, wk_bf, wv_bf):
    @pl.when(pl.program_id(1) == 0)
    def _():
        wk_bf[...] = wk_ref[0].astype(BF16)
        wv_bf[...] = wv_ref[0].astype(BF16)

    hd = kh_ref.shape[-1]
    mn = _rmsnorm(mem_ref[0], g_ref[0]).astype(BF16)
    k = _dot(mn, wk_bf[...])
    v = _dot(mn, wv_bf[...])
    k_ref[0, 0] = k
    v_ref[0, 0] = v
    for h in range(HEADS):
        kh_ref[0, 0, :, h, :] = k[:, h * hd:(h + 1) * hd]
        vh_ref[0, 0, :, h, :] = v[:, h * hd:(h + 1) * hd]


def _memory_kv(mem, norm_mem, w_xk, w_xv):
    depth, d, _ = w_xk.shape
    nb, n_mem, _ = mem.shape
    hd = d // HEADS
    flat = jax.ShapeDtypeStruct((depth, nb, n_mem, d), F32)
    heads = jax.ShapeDtypeStruct((depth, nb, n_mem, HEADS, hd), F32)
    w_spec = pl.BlockSpec((1, d, d), lambda l, b: (l, 0, 0))
    flat_spec = pl.BlockSpec((1, 1, n_mem, d), lambda l, b: (l, b, 0, 0))
    heads_spec = pl.BlockSpec((1, 1, n_mem, HEADS, hd), lambda l, b: (l, b, 0, 0, 0))
    return pl.pallas_call(
        _memkv_kernel,
        grid=(depth, nb),
        in_specs=[pl.BlockSpec((1, n_mem, d), lambda l, b: (b, 0, 0)),
                  pl.BlockSpec((1, 1, d), lambda l, b: (l, 0, 0)),
                  w_spec, w_spec],
        out_specs=(flat_spec, flat_spec, heads_spec, heads_spec),
        out_shape=(flat, flat, heads, heads),
        scratch_shapes=[pltpu.VMEM((d, d), BF16), pltpu.VMEM((d, d), BF16)],
        compiler_params=_params(2),
        name="memory_kv",
    )(mem, norm_mem.reshape(depth, 1, d), w_xk, w_xv)


def _gmlp_kernel(xp_ref, xs_ref, gn_ref, win_ref, lng_ref, lnb_ref, wmix_ref, bias_ref, wout_ref,
                 op_ref, os_ref, vs_ref, win_bf, wout_bf, *, n_prompt):
    i = pl.program_id(0)
    width = wout_ref.shape[0]
    gdim = width // GROUPS

    @pl.when(i == 0)
    def _():
        win_bf[...] = win_ref[...].astype(BF16)
        wout_bf[...] = wout_ref[...].astype(BF16)

    def mixer(x, mode):
        rows = x.shape[0]
        xn = _rmsnorm(x, gn_ref[...]).astype(BF16)
        z = jax.nn.gelu(_dot(xn, win_bf[...]))
        u = z[:, :width]
        v = _layernorm(z[:, width:], lng_ref[...], lnb_ref[...])
        vb = v.astype(BF16)
        chunks = []
        for c in range(rows // CHUNK):
            cols = [_dot(wmix_ref[mode, g], vb[c * CHUNK:(c + 1) * CHUNK, g * gdim:(g + 1) * gdim])
                    for g in range(GROUPS)]
            chunks.append(jnp.concatenate(cols, axis=1) + bias_ref[mode])
        mixed = chunks[0] if len(chunks) == 1 else jnp.concatenate(chunks, axis=0)
        gated = (u * mixed).astype(BF16)
        return x + _dot(gated, wout_bf[...]), v

    @pl.when(i < n_prompt)
    def _():
        op_ref[...] = mixer(xp_ref[...], 0)[0]

    @pl.when(i == n_prompt)
    def _():
        out, v = mixer(xs_ref[...], 1)
        os_ref[...] = out
        vs_ref[...] = v


def _gmlp(xp, xs, g_norm, w_in, ln_g, ln_b, w_s, b_s, w_out):
    tp, d = xp.shape
    ns = xs.shape[0]
    width = w_out.shape[0]
    gdim = width // GROUPS
    assert tp % TOKEN_TILE == 0 and TOKEN_TILE % CHUNK == 0 and ns == CHUNK
    n_prompt = tp // TOKEN_TILE
    causal = jnp.tril(jnp.ones((CHUNK, CHUNK), dtype=bool))
    w_prompt = jnp.where(causal[None], w_s, 0.0)
    w_sample = w_s[:, 0, 0][:, None, None] * jnp.eye(CHUNK, dtype=F32)[None]
    wmix = jnp.stack([w_prompt, w_sample]).astype(BF16)
    b_prompt = jnp.repeat(jnp.transpose(b_s), gdim, axis=1)
    b_sample = jnp.broadcast_to(jnp.repeat(b_s[:, 0], gdim)[None], (CHUNK, width))
    bias = jnp.stack([b_prompt, b_sample])

    tile = lambda i: (jnp.minimum(i, n_prompt - 1), 0)
    row = lambda a: a.reshape(1, -1)
    return pl.pallas_call(
        functools.partial(_gmlp_kernel, n_prompt=n_prompt),
        grid=(n_prompt + 1,),
        in_specs=[pl.BlockSpec((TOKEN_TILE, d), tile),
                  _resident((ns, d)), _resident((1, d)), _resident(w_in.shape),
                  _resident((1, width)), _resident((1, width)),
                  _resident(wmix.shape), _resident(bias.shape), _resident(w_out.shape)],
        out_specs=(pl.BlockSpec((TOKEN_TILE, d), tile),
                   pl.BlockSpec((ns, d), lambda i: (0, 0)),
                   pl.BlockSpec((ns, width), lambda i: (0, 0))),
        out_shape=(jax.ShapeDtypeStruct((tp, d), F32),
                   jax.ShapeDtypeStruct((ns, d), F32),
                   jax.ShapeDtypeStruct((ns, width), F32)),
        scratch_shapes=[pltpu.VMEM(w_in.shape, BF16), pltpu.VMEM(w_out.shape, BF16)],
        compiler_params=_params(1),
        name="gmlp_mixer",
    )(xp, xs, row(g_norm), w_in, row(ln_g), row(ln_b), wmix, bias, w_out)


def _softmax_rows(s, axis):
    m = jnp.max(s, axis=axis, keepdims=True)
    e = jnp.exp(s - m)
    return e / jnp.sum(e, axis=axis, keepdims=True)


def _xattn_kernel(xp_ref, xs_ref, gn_ref, wq_ref, wo_ref, k_ref, v_ref,
                  op_ref, qs_ref, wq_bf, wo_bf, *, n_prompt):
    i = pl.program_id(0)
    d = wq_ref.shape[-1]
    hd = d // HEADS
    scale = hd ** -0.5

    @pl.when(i == 0)
    def _():
        wq_bf[...] = wq_ref[0].astype(BF16)
        wo_bf[...] = wo_ref[0].astype(BF16)

    @pl.when(i < n_prompt)
    def _():
        x = xp_ref[...]
        q = _dot(_rmsnorm(x, gn_ref[0]).astype(BF16), wq_bf[...])
        heads = []
        for h in range(HEADS):
            cols = slice(h * hd, (h + 1) * hd)
            qh = q[:, cols].astype(BF16)
            kh = k_ref[0, 0, :, cols].astype(BF16)
            s = lax.dot_general(qh, kh, (((1,), (1,)), ((), ())),
                                preferred_element_type=F32) * scale
            p = _softmax_rows(s, -1).astype(BF16)
            heads.append(_dot(p, v_ref[0, 0, :, cols].astype(BF16)))
        o = jnp.concatenate(heads, axis=1).astype(BF16)
        op_ref[...] = x + _dot(o, wo_bf[...])

    @pl.when(i == n_prompt)
    def _():
        qs_ref[...] = _dot(_rmsnorm(xs_ref[...], gn_ref[0]).astype(BF16), wq_bf[...])


def _xattn_prompt(xp, xs, norm_xattn, w_xq, w_xo, mem_k, mem_v, layer):
    tp, d = xp.shape
    ns = xs.shape[0]
    depth, nb, n_mem, _ = mem_k.shape
    n_prompt = tp // TOKEN_TILE
    tiles_per_seq = n_prompt // nb
    assert tiles_per_seq * nb == n_prompt
    tile = lambda i: (jnp.minimum(i, n_prompt - 1), 0)
    kv = lambda i: (layer, jnp.minimum(i, n_prompt - 1) // tiles_per_seq, 0, 0)
    of_layer = lambda shape: pl.BlockSpec((1,) + shape, lambda i: (layer,) + (0,) * len(shape),
                                          pipeline_mode=pl.Buffered(1))
    return pl.pallas_call(
        functools.partial(_xattn_kernel, n_prompt=n_prompt),
        grid=(n_prompt + 1,),
        in_specs=[pl.BlockSpec((TOKEN_TILE, d), tile),
                  _resident((ns, d)), of_layer((1, d)), of_layer((d, d)), of_layer((d, d)),
                  pl.BlockSpec((1, 1, n_mem, d), kv), pl.BlockSpec((1, 1, n_mem, d), kv)],
        out_specs=(pl.BlockSpec((TOKEN_TILE, d), tile),
                   pl.BlockSpec((ns, d), lambda i: (0, 0))),
        out_shape=(jax.ShapeDtypeStruct((tp, d), F32), jax.ShapeDtypeStruct((ns, d), F32)),
        scratch_shapes=[pltpu.VMEM((d, d), BF16), pltpu.VMEM((d, d), BF16)],
        compiler_params=_params(1),
        name="xattn_prompt",
    )(xp, xs, norm_xattn.reshape(depth, 1, d), w_xq, w_xo, mem_k, mem_v)


def _xattn_sample_kernel(q_ref, k_ref, v_ref, o_ref, *, block, scale):
    i = pl.program_id(0)
    for b in range(block):
        r = i * block + b
        s = jnp.sum(k_ref[0, b] * q_ref[r][None], axis=-1, keepdims=True) * scale
        p = _softmax_rows(s, 0)
        o_ref[r] = jnp.sum(p * v_ref[0, b], axis=0)


def _xattn_sample(qs, cache_k, cache_v, layer):
    _, ns, n_mem, heads, hd = cache_k.shape
    block = SAMPLE_ATTN_BLOCK
    assert ns % block == 0
    kv = pl.BlockSpec((1, block, n_mem, heads, hd), lambda i: (layer, i, 0, 0, 0))
    return pl.pallas_call(
        functools.partial(_xattn_sample_kernel, block=block, scale=hd ** -0.5),
        grid=(ns // block,),
        in_specs=[_resident((ns, heads, hd)), kv, kv],
        out_specs=pl.BlockSpec((ns, heads, hd), lambda i: (0, 0, 0)),
        out_shape=jax.ShapeDtypeStruct((ns, heads, hd), F32),
        compiler_params=_params(1),
        name="xattn_sample",
    )(qs.reshape(ns, heads, hd), cache_k, cache_v)


def _attn_out_kernel(o_ref, xs_ref, wo_ref, os_ref):
    os_ref[...] = xs_ref[...] + _dot(o_ref[...].astype(BF16), wo_ref[0].astype(BF16))


def _attn_out_sample(o, xs, w_xo, layer):
    ns, d = xs.shape
    return pl.pallas_call(
        _attn_out_kernel,
        grid=(1,),
        in_specs=[_resident((ns, d)), _resident((ns, d)),
                  pl.BlockSpec((1, d, d), lambda i: (layer, 0, 0))],
        out_specs=pl.BlockSpec((ns, d), lambda i: (0, 0)),
        out_shape=jax.ShapeDtypeStruct((ns, d), F32),
        compiler_params=_params(1),
        name="attn_out_sample",
    )(o, xs, w_xo)


def _ffn_kernel(xp_ref, xs_ref, gn_ref, wg_ref, wu_ref, wd_ref, op_ref, os_ref, *, n_prompt):
    i = pl.program_id(0)
    f_dim = wg_ref.shape[1]

    def ffn(x):
        xn = _rmsnorm(x, gn_ref[...]).astype(BF16)
        acc = x
        for f in range(0, f_dim, FFN_F_CHUNK):
            cols = slice(f, f + FFN_F_CHUNK)
            hid = jax.nn.silu(_dot(xn, wg_ref[:, cols])) * _dot(xn, wu_ref[:, cols])
            acc = acc + _dot(hid.astype(BF16), wd_ref[cols, :])
        return acc

    @pl.when(i < n_prompt)
    def _():
        op_ref[...] = ffn(xp_ref[...])

    @pl.when(i == n_prompt)
    def _():
        os_ref[...] = ffn(xs_ref[...])


def _ffn(xp, xs, g_norm, w_gate, w_up, w_down):
    tp, d = xp.shape
    ns = xs.shape[0]
    f_dim = w_gate.shape[1]
    assert f_dim % FFN_F_CHUNK == 0
    n_prompt = tp // TOKEN_TILE
    wg = _cast_bf16(w_gate, d // 4)
    wu = _cast_bf16(w_up, d // 4)
    wd = _cast_bf16(w_down, f_dim // 4)
    tile = lambda i: (jnp.minimum(i, n_prompt - 1), 0)
    return pl.pallas_call(
        functools.partial(_ffn_kernel, n_prompt=n_prompt),
        grid=(n_prompt + 1,),
        in_specs=[pl.BlockSpec((TOKEN_TILE, d), tile),
                  _resident((ns, d)), _resident((1, d)),
                  _resident(wg.shape), _resident(wu.shape), _resident(wd.shape)],
        out_specs=(pl.BlockSpec((TOKEN_TILE, d), tile),
                   pl.BlockSpec((ns, d), lambda i: (0, 0))),
        out_shape=(jax.ShapeDtypeStruct((tp, d), F32), jax.ShapeDtypeStruct((ns, d), F32)),
        compiler_params=_params(1),
        name="dense_swiglu",
    )(xp, xs, g_norm.reshape(1, d), wg, wu, wd)


CONV_HALO = 32


SUBLANES = 8
CONV_UNROLL = 8
CONV_TAIL = 16


def _depthwise_conv(cbuf, y_ref, wdw_ref, bdw_ref, tm, taps):
    first = CONV_HALO - (taps - 1)
    d = y_ref.shape[1]
    max_a = (first + taps - 1) // SUBLANES
    classes = [[(a, SUBLANES * a + r - first) for a in range(max_a + 1)
                if 0 <= SUBLANES * a + r - first < taps] for r in range(SUBLANES)]
    row_i = lax.broadcasted_iota(jnp.int32, (SUBLANES, LANES), 0)

    for l in range(d // LANES):
        lanes = slice(l * LANES, (l + 1) * LANES)
        w = [jnp.broadcast_to(wdw_ref[k:k + 1, lanes], (SUBLANES, LANES)) for k in range(taps)]
        bias = jnp.broadcast_to(bdw_ref[:, lanes], (SUBLANES, LANES))

        def rotated_q(groups, j, w=w):
            out = []
            for r in range(SUBLANES):
                q = None
                for a, k in classes[r]:
                    term = w[k] * groups[j + a]
                    q = term if q is None else q + term
                out.append(q if r == 0 else pltpu.roll(q, SUBLANES - r, axis=0))
            return out

        head = {a: cbuf[SUBLANES * a:SUBLANES * (a + 1), lanes] for a in range(max_a + 1)}

        def trip(blk, carry, lanes=lanes, bias=bias, rotated_q=rotated_q):
            base = pl.multiple_of(blk * (SUBLANES * CONV_UNROLL), SUBLANES * CONV_UNROLL)
            groups = {j: cbuf[pl.ds(base + SUBLANES * j, SUBLANES), lanes]
                      for j in range(1, CONV_UNROLL + max_a + 1)}
            prev = list(carry)
            for u in range(CONV_UNROLL):
                nxt = rotated_q(groups, u + 1)
                y = prev[0] + bias
                for r in range(1, SUBLANES):
                    y = y + jnp.where(row_i < SUBLANES - r, prev[r], nxt[r])
                y_ref[pl.ds(base + SUBLANES * u, SUBLANES), lanes] = y
                prev = nxt
            return tuple(prev)

        lax.fori_loop(0, tm // (SUBLANES * CONV_UNROLL), trip, tuple(rotated_q(head, 0)))


def _conv_prompt_kernel(xp_ref, gn_ref, w1_ref, wdw_ref, bdw_ref, lng_ref, lnb_ref, w2_ref,
                        op_ref, st_ref, w1_bf, w2_bf, cbuf, ybuf, *, tiles_per_seq):
    i = pl.program_id(0)
    d = w2_ref.shape[0]
    taps = wdw_ref.shape[0]
    tm = xp_ref.shape[0]

    @pl.when(i == 0)
    def _():
        w1_bf[...] = w1_ref[...].astype(BF16)
        w2_bf[...] = w2_ref[...].astype(BF16)
        cbuf[CONV_HALO + tm:CONV_HALO + tm + CONV_TAIL, :] = jnp.zeros((CONV_TAIL, d), F32)

    @pl.when(i % tiles_per_seq == 0)
    def _():
        cbuf[0:CONV_HALO, :] = jnp.zeros((CONV_HALO, d), F32)

    x = xp_ref[...]
    ag = _dot(_rmsnorm(x, gn_ref[...]).astype(BF16), w1_bf[...])
    c = ag[:, :d] * jax.nn.sigmoid(ag[:, d:])
    cbuf[CONV_HALO:CONV_HALO + tm, :] = c
    first = CONV_HALO - (taps - 1)
    _depthwise_conv(cbuf, ybuf, wdw_ref, bdw_ref, tm, taps)
    t = jax.nn.silu(_layernorm(ybuf[...], lng_ref[...], lnb_ref[...])).astype(BF16)
    op_ref[...] = x + _dot(t, w2_bf[...])
    cbuf[0:CONV_HALO, :] = cbuf[tm:tm + CONV_HALO, :]

    @pl.when(i % tiles_per_seq == tiles_per_seq - 1)
    def _():
        st_ref[0] = cbuf[first:CONV_HALO, :]


def _conv_prompt(xp, n_seq, g_norm, w_pw1, w_dw, b_dw, ln_g, ln_b, w_pw2):
    tp, d = xp.shape
    taps = w_dw.shape[0]
    n_prompt = tp // TOKEN_TILE
    tiles_per_seq = n_prompt // n_seq
    assert tiles_per_seq * n_seq == n_prompt and taps - 1 <= CONV_HALO <= TOKEN_TILE
    assert TOKEN_TILE % (SUBLANES * CONV_UNROLL) == 0 and CONV_TAIL >= SUBLANES * 2
    row = lambda a: a.reshape(1, -1)
    return pl.pallas_call(
        functools.partial(_conv_prompt_kernel, tiles_per_seq=tiles_per_seq),
        grid=(n_prompt,),
        in_specs=[pl.BlockSpec((TOKEN_TILE, d), lambda i: (i, 0)),
                  _resident((1, d)), _resident(w_pw1.shape), _resident(w_dw.shape),
                  _resident((1, d)), _resident((1, d)), _resident((1, d)), _resident(w_pw2.shape)],
        out_specs=(pl.BlockSpec((TOKEN_TILE, d), lambda i: (i, 0)),
                   pl.BlockSpec((1, taps - 1, d), lambda i: (i // tiles_per_seq, 0, 0))),
        out_shape=(jax.ShapeDtypeStruct((tp, d), F32),
                   jax.ShapeDtypeStruct((n_seq, taps - 1, d), F32)),
        scratch_shapes=[pltpu.VMEM(w_pw1.shape, BF16), pltpu.VMEM(w_pw2.shape, BF16),
                        pltpu.VMEM((CONV_HALO + TOKEN_TILE + CONV_TAIL, d), F32),
                        pltpu.VMEM((TOKEN_TILE, d), F32)],
        compiler_params=_params(1),
        name="conv_prompt",
    )(xp, row(g_norm), w_pw1, w_dw, row(b_dw), row(ln_g), row(ln_b), w_pw2)


def _conv_sample_kernel(xs_ref, gn_ref, w1_ref, wdw_ref, bdw_ref, lng_ref, lnb_ref, w2_ref, st_ref,
                        os_ref, sto_ref, c_all, t_all, *, block):
    i = pl.program_id(0)
    d = w2_ref.shape[0]
    taps = wdw_ref.shape[0]

    @pl.when(i == 0)
    def _():
        ag = _dot(_rmsnorm(xs_ref[...], gn_ref[...]).astype(BF16), w1_ref[...].astype(BF16))
        c_all[...] = ag[:, :d] * jax.nn.sigmoid(ag[:, d:])

    w_hist = wdw_ref[0:taps - 1, :]
    w_last = wdw_ref[taps - 1:taps, :]
    for b in range(block):
        r = i * block + b
        hist = st_ref[0, b]
        cb = c_all[pl.ds(r, 1), :]
        y = jnp.sum(hist * w_hist, axis=0, keepdims=True) + cb * w_last + bdw_ref[...]
        t_all[pl.ds(r, 1), :] = jax.nn.silu(_layernorm(y, lng_ref[...], lnb_ref[...]))
        sto_ref[0, b, 0:taps - 2, :] = hist[1:taps - 1, :]
        sto_ref[0, b, taps - 2:taps - 1, :] = cb

    @pl.when(i == pl.num_programs(0) - 1)
    def _():
        os_ref[...] = xs_ref[...] + _dot(t_all[...].astype(BF16), w2_ref[...].astype(BF16))


def _conv_sample(xs, state, g_norm, w_pw1, w_dw, b_dw, ln_g, ln_b, w_pw2):
    ns, d = xs.shape
    taps = w_dw.shape[0]
    block = SAMPLE_CONV_BLOCK
    assert ns % block == 0
    row = lambda a: a.reshape(1, -1)
    st = pl.BlockSpec((1, block, taps - 1, d), lambda i: (0, i, 0, 0))
    return pl.pallas_call(
        functools.partial(_conv_sample_kernel, block=block),
        grid=(ns // block,),
        in_specs=[_resident((ns, d)), _resident((1, d)), _resident(w_pw1.shape),
                  _resident(w_dw.shape), _resident((1, d)), _resident((1, d)), _resident((1, d)),
                  _resident(w_pw2.shape), st],
        out_specs=(pl.BlockSpec((ns, d), lambda i: (0, 0)), st),
        out_shape=(jax.ShapeDtypeStruct((ns, d), F32), jax.ShapeDtypeStruct(state.shape, F32)),
        scratch_shapes=[pltpu.VMEM((ns, d), F32), pltpu.VMEM((ns, d), F32)],
        compiler_params=_params(1),
        name="conv_sample",
    )(xs, row(g_norm), w_pw1, w_dw, row(b_dw), row(ln_g), row(ln_b), w_pw2, state)


def _router_kernel(xp_ref, xs_ref, gn_ref, wr_ref, hn_ref, info_ref, gates_ref, seen_ref, count_ref, seen,
                   *, n_prompt, n_experts):
    i = pl.program_id(0)

    def route(x):
        xn = _rmsnorm(x, gn_ref[...])
        logits = jnp.dot(xn, wr_ref[...], precision=lax.Precision.HIGHEST,
                         preferred_element_type=F32)
        lane = lax.broadcasted_iota(jnp.int32, logits.shape, 1).astype(F32)
        neg = jnp.float32(-jnp.inf)
        logits = jnp.where(lane < n_experts, logits, neg)
        v1 = jnp.max(logits, axis=-1, keepdims=True)
        i1 = jnp.min(jnp.where(logits == v1, lane, float(LANES)), axis=-1, keepdims=True)
        rest = jnp.where(lane == i1, neg, logits)
        v2 = jnp.max(rest, axis=-1, keepdims=True)
        i2 = jnp.min(jnp.where(rest == v2, lane, float(LANES)), axis=-1, keepdims=True)
        e2 = jnp.exp(v2 - v1)
        denom = 1.0 + e2

        rows = x.shape[0]
        pick1 = jnp.where(lane == i1, 1.0, 0.0)
        pick2 = jnp.where(lane == i2, 1.0, 0.0)
        picks = pick1 + pick2
        r_i = lax.broadcasted_iota(jnp.int32, (rows, rows), 0)
        c_i = lax.broadcasted_iota(jnp.int32, (rows, rows), 1)
        earlier = jnp.where(c_i < r_i, 1.0, 0.0).astype(BF16)
        before = _dot(earlier, picks.astype(BF16)) + seen[0:1, :]
        rank1 = jnp.sum(pick1 * before, axis=-1, keepdims=True)
        rank2 = jnp.sum(pick2 * before, axis=-1, keepdims=True)
        total = seen[0:1, :] + jnp.sum(picks, axis=0, keepdims=True)

        info = jnp.where(lane == 0.0, i1, jnp.where(lane == 1.0, i2,
                         jnp.where(lane == 2.0, rank1, jnp.where(lane == 3.0, rank2, 0.0))))
        gates = jnp.where(lane == 0.0, 1.0 / denom, jnp.where(lane == 1.0, e2 / denom, 0.0))
        starts = [seen[0:1, :]]
        for w in range(1, TOKEN_TILE // DISPATCH_WIN):
            starts.append(before[w * DISPATCH_WIN:w * DISPATCH_WIN + 1, :] if w * DISPATCH_WIN < rows else total)
        pad = jnp.zeros((SUBLANES - len(starts), LANES), F32)
        seen_ref[0] = jnp.concatenate(starts + [pad], axis=0).astype(jnp.int32)
        seen[0:1, :] = total
        return xn.astype(BF16), info.astype(jnp.int32), gates

    @pl.when(i == 0)
    def _():
        seen[...] = jnp.zeros(seen.shape, F32)

    @pl.when(i < n_prompt)
    def _():
        hn, info, gates = route(xp_ref[...])
        hn_ref[...] = hn
        info_ref[...] = info
        gates_ref[...] = gates

    @pl.when(i == n_prompt)
    def _():
        ns = xs_ref.shape[0]
        hn, info, gates = route(xs_ref[...])
        hn_ref[...] = jnp.zeros(hn_ref.shape, BF16)
        info_ref[...] = jnp.zeros(info_ref.shape, jnp.int32)
        gates_ref[...] = jnp.zeros(gates_ref.shape, F32)
        hn_ref[0:ns, :] = hn
        info_ref[0:ns, :] = info
        gates_ref[0:ns, :] = gates
        count_ref[...] = jnp.broadcast_to(seen[0:1, :], count_ref.shape).astype(jnp.int32)

    @pl.when(i > n_prompt)
    def _():
        hn_ref[...] = jnp.zeros(hn_ref.shape, BF16)
        info_ref[...] = jnp.zeros(info_ref.shape, jnp.int32)
        gates_ref[...] = jnp.zeros(gates_ref.shape, F32)
        seen_ref[0] = jnp.broadcast_to(seen[0:1, :], (SUBLANES, LANES)).astype(jnp.int32)


def _router(xp, xs, g_norm, w_router):
    tp, d = xp.shape
    ns = xs.shape[0]
    n_experts = w_router.shape[1]
    n_prompt = tp // TOKEN_TILE
    n_tiles = n_prompt + 1 + -(-(DISPATCH_SPAN - 1) * DISPATCH_WIN // TOKEN_TILE)
    t_pad = n_tiles * TOKEN_TILE
    assert TOKEN_TILE % DISPATCH_WIN == 0 and TOKEN_TILE // DISPATCH_WIN <= SUBLANES
    wr = jnp.pad(w_router, ((0, 0), (0, LANES - n_experts)))
    tile = lambda i: (jnp.minimum(i, n_prompt - 1), 0)
    lanes_spec = pl.BlockSpec((TOKEN_TILE, LANES), lambda i: (i, 0))
    return pl.pallas_call(
        functools.partial(_router_kernel, n_prompt=n_prompt, n_experts=n_experts),
        grid=(n_tiles,),
        in_specs=[pl.BlockSpec((TOKEN_TILE, d), tile),
                  _resident((ns, d)), _resident((1, d)), _resident((d, LANES))],
        out_specs=(pl.BlockSpec((TOKEN_TILE, d), lambda i: (i, 0)), lanes_spec, lanes_spec,
                   pl.BlockSpec((1, SUBLANES, LANES), lambda i: (i, 0, 0)),
                   pl.BlockSpec((SUBLANES, LANES), lambda i: (0, 0))),
        out_shape=(jax.ShapeDtypeStruct((t_pad, d), BF16),
                   jax.ShapeDtypeStruct((t_pad, LANES), jnp.int32),
                   jax.ShapeDtypeStruct((t_pad, LANES), F32),
                   jax.ShapeDtypeStruct((n_tiles, SUBLANES, LANES), jnp.int32),
                   jax.ShapeDtypeStruct((SUBLANES, LANES), jnp.int32)),
        scratch_shapes=[pltpu.VMEM((SUBLANES, LANES), F32)],
        compiler_params=_params(1),
        name="moe_router",
    )(xp, xs, g_norm.reshape(1, d), wr)


def _lane_tile(a, width):
    return jnp.concatenate([a] * (width // LANES), axis=1)


def _dispatch_kernel(slot_nsub, win_lo, win_hi, pos_ref, gate_ref, hn_ref, x_ref, g_ref):
    s = pl.program_id(0)
    n_sub = slot_nsub[s]
    subs_per_slot = SLOT_ROWS // SUB_ROWS
    d = hn_ref.shape[1]
    span = DISPATCH_SPAN * DISPATCH_WIN
    row_in_block = lax.broadcasted_iota(jnp.int32, (SUB_ROWS, span), 0)

    def sub_block(j, carry):
        rows = pl.ds(pl.multiple_of(j * SUB_ROWS, SUB_ROWS), SUB_ROWS)
        row_id = row_in_block + (s * SLOT_ROWS + j * SUB_ROWS)
        k = s * subs_per_slot + j
        x_ref[rows, :] = jnp.zeros((SUB_ROWS, d), BF16)
        g_ref[rows, :] = jnp.zeros((SUB_ROWS, LANES), F32)

        def windows(c, carry):
            w = win_lo[k] + c * DISPATCH_SPAN
            along = lambda ref, choice: jnp.concatenate(
                [ref[w + n, choice:choice + 1, :] for n in range(DISPATCH_SPAN)], axis=1)
            hit1 = along(pos_ref, 0) == row_id
            hit2 = along(pos_ref, 1) == row_id
            sel = jnp.where(hit1, 1.0, jnp.where(hit2, 1.0, 0.0)).astype(BF16)
            gates = jnp.where(hit1, along(gate_ref, 0), jnp.where(hit2, along(gate_ref, 1), 0.0))
            base = pl.multiple_of(w * DISPATCH_WIN, DISPATCH_WIN)
            x_ref[rows, :] += _dot(sel, hn_ref[pl.ds(base, span), :]).astype(BF16)
            g_ref[rows, :] += jnp.broadcast_to(jnp.sum(gates, axis=-1, keepdims=True), (SUB_ROWS, LANES))
            return carry

        n_win = win_hi[k] + 1 - win_lo[k]
        lax.fori_loop(0, (n_win + DISPATCH_SPAN - 1) // DISPATCH_SPAN, windows, 0)
        return carry

    def zero_block(j, carry):
        rows = pl.ds(pl.multiple_of(j * SUB_ROWS, SUB_ROWS), SUB_ROWS)
        x_ref[rows, :] = jnp.zeros((SUB_ROWS, d), BF16)
        g_ref[rows, :] = jnp.zeros((SUB_ROWS, LANES), F32)
        return carry

    lax.fori_loop(0, n_sub, sub_block, 0)
    lax.fori_loop(n_sub, subs_per_slot, zero_block, 0)


def _dispatch(hn, pos_win, gate_win, slot_nsub, win_lo, win_hi):
    t_pad, d = hn.shape
    n_slots = slot_nsub.shape[0]
    assert t_pad % DISPATCH_WIN == 0
    grid_spec = pltpu.PrefetchScalarGridSpec(
        num_scalar_prefetch=3,
        grid=(n_slots,),
        in_specs=[pl.BlockSpec(pos_win.shape, lambda s, *_: (0, 0, 0)),
                  pl.BlockSpec(gate_win.shape, lambda s, *_: (0, 0, 0)),
                  pl.BlockSpec((t_pad, d), lambda s, *_: (0, 0), pipeline_mode=pl.Buffered(1))],
        out_specs=(pl.BlockSpec((SLOT_ROWS, d), lambda s, *_: (s, 0)),
                   pl.BlockSpec((SLOT_ROWS, LANES), lambda s, *_: (s, 0))),
    )
    return pl.pallas_call(
        _dispatch_kernel,
        grid_spec=grid_spec,
        out_shape=(jax.ShapeDtypeStruct((n_slots * SLOT_ROWS, d), BF16),
                   jax.ShapeDtypeStruct((n_slots * SLOT_ROWS, LANES), F32)),
        compiler_params=_params(1),
        name="moe_dispatch",
    )(slot_nsub, win_lo, win_hi, pos_win, gate_win, hn)


def _expert_kernel(slot_expert, slot_nsub, x_ref, gate_ref, wg_ref, wu_ref, wd_ref, y_ref,
                   wg_bf, wu_bf, wd_bf, acc):
    s = pl.program_id(0)
    f = pl.program_id(1)
    n_sub = slot_nsub[s]
    last_f = f == pl.num_programs(1) - 1
    d = y_ref.shape[1]

    @pl.when(n_sub > 0)
    def _():
        wg_bf[...] = wg_ref[0].astype(BF16)
        wu_bf[...] = wu_ref[0].astype(BF16)
        wd_bf[...] = wd_ref[0].astype(BF16)

    def zero_acc(j, carry):
        rows = pl.ds(pl.multiple_of(j * SUB_ROWS, SUB_ROWS), SUB_ROWS)
        acc[rows, :] = jnp.zeros((SUB_ROWS, d), F32)
        return carry

    @pl.when(f == 0)
    def _():
        lax.fori_loop(0, n_sub, zero_acc, 0)

    def block(start, size):
        rows = pl.ds(pl.multiple_of(start, SUB_ROWS), size)
        xg = x_ref[rows, :]
        hid = jax.nn.silu(_dot(xg, wg_bf[...])) * _dot(xg, wu_bf[...])
        acc[rows, :] += _dot(hid.astype(BF16), wd_bf[...])

        @pl.when(last_f)
        def _():
            y_ref[rows, :] = (acc[rows, :] * _lane_tile(gate_ref[rows, :], d)).astype(y_ref.dtype)

    def pair(j, carry):
        block(j * (2 * SUB_ROWS), 2 * SUB_ROWS)
        return carry

    lax.fori_loop(0, n_sub // 2, pair, 0)

    @pl.when(n_sub % 2 == 1)
    def _():
        block((n_sub - 1) * SUB_ROWS, SUB_ROWS)

    def zero_block(j, carry):
        rows = pl.ds(pl.multiple_of(j * SUB_ROWS, SUB_ROWS), SUB_ROWS)
        y_ref[rows, :] = jnp.zeros((SUB_ROWS, d), y_ref.dtype)
        return carry

    @pl.when(last_f)
    def _():
        lax.fori_loop(n_sub, SLOT_ROWS // SUB_ROWS, zero_block, 0)


def _expert_ffn(x_slots, gate_rep, slot_expert, slot_nsub, w_gate, w_up, w_down):
    n_slots = slot_expert.shape[0]
    d = x_slots.shape[1]
    f_dim = w_gate.shape[2]
    tf = EXPERT_F_TILE
    assert f_dim % tf == 0 and SLOT_ROWS % SUB_ROWS == 0
    n_f = f_dim // tf
    f_tile = lambda s, f, sn: jnp.where(sn[s] > 0, f, n_f - 1)
    grid_spec = pltpu.PrefetchScalarGridSpec(
        num_scalar_prefetch=2,
        grid=(n_slots, f_dim // tf),
        in_specs=[pl.BlockSpec((SLOT_ROWS, d), lambda s, f, se, sn: (s, 0)),
                  pl.BlockSpec((SLOT_ROWS, LANES), lambda s, f, se, sn: (s, 0)),
                  pl.BlockSpec((1, d, tf), lambda s, f, se, sn: (se[s], 0, f_tile(s, f, sn))),
                  pl.BlockSpec((1, d, tf), lambda s, f, se, sn: (se[s], 0, f_tile(s, f, sn))),
                  pl.BlockSpec((1, tf, d), lambda s, f, se, sn: (se[s], f_tile(s, f, sn), 0))],
        out_specs=pl.BlockSpec((SLOT_ROWS, d), lambda s, f, se, sn: (s, 0)),
        scratch_shapes=[pltpu.VMEM((d, tf), BF16), pltpu.VMEM((d, tf), BF16),
                        pltpu.VMEM((tf, d), BF16), pltpu.VMEM((SLOT_ROWS, d), F32)],
    )
    return pl.pallas_call(
        _expert_kernel,
        grid_spec=grid_spec,
        out_shape=jax.ShapeDtypeStruct((n_slots * SLOT_ROWS, d), BF16),
        compiler_params=_params(2),
        name="expert_swiglu",
    )(slot_expert, slot_nsub, x_slots, gate_rep, w_gate, w_up, w_down)


def _combine_kernel(n_blocks, block_ids, xp_ref, xs_ref, pos1_ref, pos2_ref, gf_ref, *rest,
                    n_prompt, max_blocks):
    y_refs = rest[:max_blocks]
    op_ref, os_ref = rest[max_blocks:]
    i = pl.program_id(0)

    def combine(x, o_ref):
        rows = x.shape[0]
        p1 = _lane_tile(pos1_ref[0:rows, :], COMBINE_BLOCK)
        p2 = _lane_tile(pos2_ref[0:rows, :], COMBINE_BLOCK)
        lane = lax.broadcasted_iota(jnp.int32, (rows, COMBINE_BLOCK), 1)

        def picked(blocks):
            total = None
            for b in blocks:
                first_row = jnp.where(b < n_blocks[i], block_ids[i * max_blocks + b] * COMBINE_BLOCK,
                                      -2 * COMBINE_BLOCK)
                row_id = lane + first_row
                sel = jnp.where(p1 == row_id, 1.0, jnp.where(p2 == row_id, 1.0, 0.0)).astype(BF16)
                part = _dot(sel, y_refs[b][...])
                total = part if total is None else total + part
            return total

        o_ref[...] = x + picked(range(0, min(COMBINE_ALWAYS, max_blocks)))
        for g in range(COMBINE_ALWAYS, max_blocks, COMBINE_GROUP):
            @pl.when(g < n_blocks[i])
            def _():
                o_ref[...] += picked(range(g, min(g + COMBINE_GROUP, max_blocks)))
        o_ref[...] = _rmsnorm(o_ref[...], gf_ref[...])

    @pl.when(i < n_prompt)
    def _():
        combine(xp_ref[...], op_ref)

    @pl.when(i == n_prompt)
    def _():
        combine(xs_ref[...], os_ref)


def _combine(xp, xs, pos_rep, y, n_blocks, block_ids, g_final, max_blocks):
    tp, d = xp.shape
    ns = xs.shape[0]
    n_prompt = tp // TOKEN_TILE
    tile = lambda i, *_: (jnp.minimum(i, n_prompt - 1), 0)
    pos_spec = pl.BlockSpec((TOKEN_TILE, LANES), lambda i, *_: (i, 0))

    def y_spec(b):
        return pl.BlockSpec((COMBINE_BLOCK, d), lambda i, nb, ids: (ids[i * max_blocks + b], 0))

    grid_spec = pltpu.PrefetchScalarGridSpec(
        num_scalar_prefetch=2,
        grid=(n_prompt + 1,),
        in_specs=[pl.BlockSpec((TOKEN_TILE, d), tile),
                  pl.BlockSpec((ns, d), lambda i, *_: (0, 0)),
                  pos_spec, pos_spec,
                  pl.BlockSpec((1, d), lambda i, *_: (0, 0))] + [y_spec(b) for b in range(max_blocks)],
        out_specs=(pl.BlockSpec((TOKEN_TILE, d), tile),
                   pl.BlockSpec((ns, d), lambda i, *_: (0, 0))),
    )
    return pl.pallas_call(
        functools.partial(_combine_kernel, n_prompt=n_prompt, max_blocks=max_blocks),
        grid_spec=grid_spec,
        out_shape=(jax.ShapeDtypeStruct((tp, d), F32), jax.ShapeDtypeStruct((ns, d), F32)),
        compiler_params=_params(1),
        name="moe_combine",
    )(n_blocks, block_ids, xp, xs, pos_rep[0], pos_rep[1], g_final.reshape(1, d), *([y] * max_blocks))


def _routing_tables(info, gates, seen, counts, t_valid, n_experts):
    i32 = jnp.int32
    t_pad = info.shape[0]
    n_tiles = seen.shape[0]
    wins_per_tile = TOKEN_TILE // DISPATCH_WIN
    subs_per_slot = SLOT_ROWS // SUB_ROWS
    n_slots = t_valid * TOP_K // SLOT_ROWS + n_experts + 1
    experts = jnp.arange(n_experts, dtype=i32)

    cnt = counts[0, :n_experts]
    n_sub = (cnt + SUB_ROWS - 1) // SUB_ROWS
    n_chunks = (n_sub + subs_per_slot - 1) // subs_per_slot
    chunk_end = jnp.cumsum(n_chunks)
    slot_base = chunk_end - n_chunks
    n_used = chunk_end[-1]
    div_chunks = jnp.maximum(n_chunks, 1)
    div_sub = jnp.maximum(n_sub, 1)

    def slot_row(e, rank):
        pick = lambda tab: jnp.sum(jnp.where(e[..., None] == experts, tab, 0), axis=-1)
        subs, chunks, base = pick(div_sub), pick(div_chunks), pick(slot_base)
        sub = rank // SUB_ROWS
        chunk = ((sub + 1) * chunks + subs - 1) // subs - 1
        first_sub = chunk * pick(n_sub) // chunks
        return (base + chunk) * SLOT_ROWS + rank - first_sub * SUB_ROWS

    sid = jnp.arange(n_slots, dtype=i32)
    expert_of = lambda s: jnp.sum((s[..., None] >= chunk_end).astype(i32), axis=-1)
    slot_expert = jnp.where(sid < n_used, expert_of(sid), expert_of(n_used - 1)).astype(i32)
    run = sid - slot_base[slot_expert]
    run_lo = run * n_sub[slot_expert] // div_chunks[slot_expert]
    run_hi = (run + 1) * n_sub[slot_expert] // div_chunks[slot_expert]
    slot_nsub = jnp.where(sid < n_used, run_hi - run_lo, 0).astype(i32)

    token_ok = (jnp.arange(t_pad, dtype=i32) < t_valid)[:, None]
    pos = jnp.where(token_ok, slot_row(info[:, 0:TOP_K], info[:, TOP_K:2 * TOP_K]), -1)
    n_win = t_pad // DISPATCH_WIN
    by_window = lambda a: jnp.transpose(a.reshape(n_win, DISPATCH_WIN, TOP_K), (0, 2, 1))
    pos_win = by_window(pos)
    gate_win = by_window(gates[:, 0:TOP_K])

    win_seen = seen[:, :wins_per_tile, :n_experts].reshape(n_win, n_experts)
    seen_slot = jnp.transpose(win_seen)[slot_expert]
    j = jnp.arange(subs_per_slot, dtype=i32)[None, :]
    rank_lo = (run_lo[:, None] + j) * SUB_ROWS
    rank_hi = jnp.minimum(rank_lo + SUB_ROWS, cnt[slot_expert][:, None]) - 1
    window_of = lambda r: jnp.sum((seen_slot[:, None, :] <= r[:, :, None]).astype(i32), axis=-1) - 1
    active = j < slot_nsub[:, None]
    win_lo = jnp.where(active, window_of(rank_lo), 1).reshape(-1)
    win_hi = jnp.where(active, window_of(rank_hi), 0).reshape(-1)

    tile_lo = seen[:, 0, :n_experts]
    tile_hi = jnp.concatenate([tile_lo[1:], cnt[None, :]], axis=0)
    blk_lo = tile_lo // COMBINE_BLOCK
    per_expert = jnp.where(tile_hi > tile_lo, (tile_hi - 1) // COMBINE_BLOCK - blk_lo + 1, 0)
    ends = jnp.cumsum(per_expert, axis=1)
    n_blocks = ends[:, -1]
    max_blocks = TOKEN_TILE * TOP_K // COMBINE_BLOCK + 2 * n_experts
    b = jnp.arange(max_blocks, dtype=i32)[None, :]
    e_of_b = jnp.minimum(jnp.sum((ends[:, None, :] <= b[:, :, None]).astype(i32), axis=-1), n_experts - 1)
    take = lambda tab: jnp.take_along_axis(tab, e_of_b, axis=1)
    rank_block = take(blk_lo) + b - (take(ends) - take(per_expert))
    ids = slot_row(e_of_b, rank_block * COMBINE_BLOCK) // COMBINE_BLOCK
    valid = b < n_blocks[:, None]
    last = lax.cummax(jnp.where(valid, jnp.arange(n_tiles, dtype=i32)[:, None], 0), axis=0)
    block_ids = jnp.take_along_axis(jnp.where(valid, ids, 0), last, axis=0).reshape(-1)

    rep = lambda a: jnp.broadcast_to(a[:, None], (t_pad, LANES))
    return dict(slot_expert=slot_expert, slot_nsub=slot_nsub, pos_win=pos_win, gate_win=gate_win,
                win_lo=win_lo.astype(i32), win_hi=win_hi.astype(i32),
                pos_rep=(rep(pos[:, 0]), rep(pos[:, 1])),
                n_blocks=n_blocks.astype(i32), block_ids=block_ids.astype(i32), max_blocks=max_blocks)


def _moe(xp, xs, g_norm, w_router, w_gate, w_up, w_down, g_final):
    tp, d = xp.shape
    ns = xs.shape[0]
    n_experts = w_router.shape[1]
    assert ns <= TOKEN_TILE
    hn, info, gates, seen, counts = _router(xp, xs, g_norm, w_router)
    t = _routing_tables(info, gates, seen, counts, tp + ns, n_experts)
    x_slots, gate_rep = _dispatch(hn, t["pos_win"], t["gate_win"], t["slot_nsub"], t["win_lo"], t["win_hi"])
    y = _expert_ffn(x_slots, gate_rep, t["slot_expert"], t["slot_nsub"], w_gate, w_up, w_down)
    return _combine(xp, xs, t["pos_rep"], y, t["n_blocks"], t["block_ids"], g_final, t["max_blocks"])


def kernel(x_prompt, x_sample, mem_prompt, cache_mem_k, cache_mem_v, state_conv, norm_mix, norm_xattn, norm_ffn, norm_mem, norm_final, w_xq, w_xk, w_xv, w_xo, a_w_in, a_ln_g, a_ln_b, a_w_s, a_b_s, a_w_out, b_w_pw1, b_w_dw, b_b_dw, b_ln_g, b_ln_b, b_w_pw2, ffn_w_gate, ffn_w_up, ffn_w_down, moe_w_router, moe_w_gate, moe_w_up, moe_w_down):
    nb, seq, d = x_prompt.shape
    ns = x_sample.shape[0]
    depth = norm_mix.shape[0]
    n_mem = mem_prompt.shape[1]
    assert depth == 2 and x_sample.shape[1] == 1

    mem_k, mem_v, mem_k_heads, mem_v_heads = _memory_kv(mem_prompt, norm_mem, w_xk, w_xv)

    hp = x_prompt.reshape(nb * seq, d)
    hs = x_sample.reshape(ns, d)

    def cross_attention(hp, hs, layer):
        hp_new, qs = _xattn_prompt(hp, hs, norm_xattn, w_xq, w_xo, mem_k, mem_v, layer)
        o = _xattn_sample(qs, cache_mem_k, cache_mem_v, layer)
        hs_new = _attn_out_sample(o.reshape(ns, d), hs, w_xo, layer)
        return hp_new, hs_new

    hp, hs, v_sample = _gmlp(hp, hs, norm_mix[0], a_w_in[0], a_ln_g[0], a_ln_b[0],
                             a_w_s[0], a_b_s[0], a_w_out[0])
    hp, hs = cross_attention(hp, hs, 0)
    hp, hs = _ffn(hp, hs, norm_ffn[0], ffn_w_gate[0], ffn_w_up[0], ffn_w_down[0])

    conv_w = (norm_mix[1], b_w_pw1[0], b_w_dw[0], b_b_dw[0], b_ln_g[0], b_ln_b[0], b_w_pw2[0])
    hp, conv_state_prompt = _conv_prompt(hp, nb, *conv_w)
    hs, conv_state_sample = _conv_sample(hs, state_conv, *conv_w)
    hp, hs = cross_attention(hp, hs, 1)
    yp, ys = _moe(hp, hs, norm_ffn[1], moe_w_router[0], moe_w_gate[0], moe_w_up[0], moe_w_down[0],
                  norm_final)

    return (yp.reshape(nb, seq, d),
            ys.reshape(ns, 1, d),
            mem_k_heads,
            mem_v_heads,
            conv_state_prompt[None],
            conv_state_sample,
            v_sample.reshape(1, ns, 1, -1))
```

```python
import functools

import jax
import jax.numpy as jnp
from jax import lax
from jax.experimental import pallas as pl
from jax.experimental.pallas import tpu as pltpu

F32 = jnp.float32
BF16 = jnp.bfloat16

RMS_EPS = 1e-6
LN_EPS = 1e-5
CHUNK = 128
GROUPS = 8
HEADS = 4
TOP_K = 2
LANES = 128
V7X_VMEM_LIMIT = 56 * 1024 * 1024

TOKEN_TILE = 512
SAMPLE_ATTN_BLOCK = 8
MEMKV_SEQS = 2
SAMPLE_CONV_BLOCK = 16
SLOT_ROWS = 2304
SUB_ROWS = 256
EXPERT_F_TILE = 512
DISPATCH_WIN = 256
DISPATCH_SPAN = 5
COMBINE_BLOCK = 256
COMBINE_ALWAYS = 12
COMBINE_GROUP = 4
FFN_F_CHUNK = 256


def _params(n_axes, vmem=V7X_VMEM_LIMIT):
    return pltpu.CompilerParams(dimension_semantics=("arbitrary",) * n_axes,
                                vmem_limit_bytes=vmem)


def _resident(shape):
    nd = len(shape)
    return pl.BlockSpec(shape, lambda *_: (0,) * nd, pipeline_mode=pl.Buffered(1))


def _rmsnorm(x, g):
    return x * lax.rsqrt(jnp.mean(x * x, axis=-1, keepdims=True) + RMS_EPS) * g


def _layernorm(x, g, b):
    xc = x - jnp.mean(x, axis=-1, keepdims=True)
    var = jnp.mean(xc * xc, axis=-1, keepdims=True)
    return xc * lax.rsqrt(var + LN_EPS) * g + b


def _dot(a, b):
    return jnp.dot(a, b, preferred_element_type=F32)


def _cast_kernel(x_ref, o_ref):
    o_ref[...] = x_ref[...].astype(o_ref.dtype)


def _cast_bf16(w, rows_per_step):
    r, c = w.shape
    return pl.pallas_call(
        _cast_kernel,
        grid=(r // rows_per_step,),
        in_specs=[pl.BlockSpec((rows_per_step, c), lambda i: (i, 0))],
        out_specs=pl.BlockSpec((rows_per_step, c), lambda i: (i, 0)),
        out_shape=jax.ShapeDtypeStruct((r, c), BF16),
        compiler_params=_params(1),
        name="cast_bf16",
    )(w)


def _memkv_kernel(mem_ref, g_ref, wk_ref, wv_ref, k_ref, v_ref, kh_ref, vh_ref, wk_bf, wv_bf):
    @pl.when(pl.program_id(1) == 0)
    def _():
        wk_bf[...] = wk_ref[0].astype(BF16)
        wv_bf[...] = wv_ref[0].astype(BF16)

    hd = kh_ref.shape[-1]
    nb_step, n_mem, d = mem_ref.shape
    mn = _rmsnorm(mem_ref[...].reshape(nb_step * n_mem, d), g_ref[0]).astype(BF16)
    k = _dot(mn, wk_bf[...])
    v = _dot(mn, wv_bf[...])
    for b in range(nb_step):
        rows = slice(b * n_mem, (b + 1) * n_mem)
        k_ref[0, b] = k[rows]
        v_ref[0, b] = v[rows]
        for h in range(HEADS):
            kh_ref[0, b, :, h, :] = k[rows, h * hd:(h + 1) * hd]
            vh_ref[0, b, :, h, :] = v[rows, h * hd:(h + 1) * hd]


def _memory_kv(mem, norm_mem, w_xk, w_xv):
    depth, d, _ = w_xk.shape
    nb, n_mem, _ = mem.shape
    hd = d // HEADS
    flat = jax.ShapeDtypeStruct((depth, nb, n_mem, d), F32)
    heads = jax.ShapeDtypeStruct((depth, nb, n_mem, HEADS, hd), F32)
    step = MEMKV_SEQS if nb % MEMKV_SEQS == 0 else 1
    w_spec = pl.BlockSpec((1, d, d), lambda l, b: (l, 0, 0))
    flat_spec = pl.BlockSpec((1, step, n_mem, d), lambda l, b: (l, b, 0, 0))
    heads_spec = pl.BlockSpec((1, step, n_mem, HEADS, hd), lambda l, b: (l, b, 0, 0, 0))
    return pl.pallas_call(
        _memkv_kernel,
        grid=(depth, nb // step),
        in_specs=[pl.BlockSpec((step, n_mem, d), lambda l, b: (b, 0, 0)),
                  pl.BlockSpec((1, 1, d), lambda l, b: (l, 0, 0)),
                  w_spec, w_spec],
        out_specs=(flat_spec, flat_spec, heads_spec, heads_spec),
        out_shape=(flat, flat, heads, heads),
        scratch_shapes=[pltpu.VMEM((d, d), BF16), pltpu.VMEM((d, d), BF16)],
        compiler_params=_params(2),
        name="memory_kv",
    )(mem, norm_mem.reshape(depth, 1, d), w_xk, w_xv)


def _gmlp_kernel(xp_ref, xs_ref, gn_ref, win_ref, lng_ref, lnb_ref, wmix_ref, bias_ref, wout_ref,
                 op_ref, os_ref, vs_ref, win_bf, wout_bf, *, n_prompt):
    i = pl.program_id(0)
    width = wout_ref.shape[0]
    gdim = width // GROUPS

    @pl.when(i == 0)
    def _():
        win_bf[...] = win_ref[...].astype(BF16)
        wout_bf[...] = wout_ref[...].astype(BF16)

    def mixer(x, mode):
        rows = x.shape[0]
        xn = _rmsnorm(x, gn_ref[...]).astype(BF16)
        z = jax.nn.gelu(_dot(xn, win_bf[...]))
        u = z[:, :width]
        v = _layernorm(z[:, width:], lng_ref[...], lnb_ref[...])
        vb = v.astype(BF16)
        chunks = []
        for c in range(rows // CHUNK):
            cols = [_dot(wmix_ref[mode, g], vb[c * CHUNK:(c + 1) * CHUNK, g * gdim:(g + 1) * gdim])
                    for g in range(GROUPS)]
            chunks.append(jnp.concatenate(cols, axis=1) + bias_ref[mode])
        mixed = chunks[0] if len(chunks) == 1 else jnp.concatenate(chunks, axis=0)
        gated = (u * mixed).astype(BF16)
        return x + _dot(gated, wout_bf[...]), v

    @pl.when(i < n_prompt)
    def _():
        op_ref[...] = mixer(xp_ref[...], 0)[0]

    @pl.when(i == n_prompt)
    def _():
        out, v = mixer(xs_ref[...], 1)
        os_ref[...] = out
        vs_ref[...] = v


def _gmlp(xp, xs, g_norm, w_in, ln_g, ln_b, w_s, b_s, w_out):
    tp, d = xp.shape
    ns = xs.shape[0]
    width = w_out.shape[0]
    gdim = width // GROUPS
    assert tp % TOKEN_TILE == 0 and TOKEN_TILE % CHUNK == 0 and ns == CHUNK
    n_prompt = tp // TOKEN_TILE
    causal = jnp.tril(jnp.ones((CHUNK, CHUNK), dtype=bool))
    w_prompt = jnp.where(causal[None], w_s, 0.0)
    w_sample = w_s[:, 0, 0][:, None, None] * jnp.eye(CHUNK, dtype=F32)[None]
    wmix = jnp.stack([w_prompt, w_sample]).astype(BF16)
    b_prompt = jnp.repeat(jnp.transpose(b_s), gdim, axis=1)
    b_sample = jnp.broadcast_to(jnp.repeat(b_s[:, 0], gdim)[None], (CHUNK, width))
    bias = jnp.stack([b_prompt, b_sample])

    tile = lambda i: (jnp.minimum(i, n_prompt - 1), 0)
    row = lambda a: a.reshape(1, -1)
    return pl.pallas_call(
        functools.partial(_gmlp_kernel, n_prompt=n_prompt),
        grid=(n_prompt + 1,),
        in_specs=[pl.BlockSpec((TOKEN_TILE, d), tile),
                  _resident((ns, d)), _resident((1, d)), _resident(w_in.shape),
                  _resident((1, width)), _resident((1, width)),
                  _resident(wmix.shape), _resident(bias.shape), _resident(w_out.shape)],
        out_specs=(pl.BlockSpec((TOKEN_TILE, d), tile),
                   pl.BlockSpec((ns, d), lambda i: (0, 0)),
                   pl.BlockSpec((ns, width), lambda i: (0, 0))),
        out_shape=(jax.ShapeDtypeStruct((tp, d), F32),
                   jax.ShapeDtypeStruct((ns, d), F32),
                   jax.ShapeDtypeStruct((ns, width), F32)),
        scratch_shapes=[pltpu.VMEM(w_in.shape, BF16), pltpu.VMEM(w_out.shape, BF16)],
        compiler_params=_params(1),
        name="gmlp_mixer",
    )(xp, xs, row(g_norm), w_in, row(ln_g), row(ln_b), wmix, bias, w_out)


def _softmax_rows(s, axis):
    m = jnp.max(s, axis=axis, keepdims=True)
    e = jnp.exp(s - m)
    return e / jnp.sum(e, axis=axis, keepdims=True)


def _xattn_kernel(xp_ref, xs_ref, gn_ref, wq_ref, wo_ref, k_ref, v_ref,
                  op_ref, qs_ref, wq_bf, wo_bf, *, n_prompt):
    i = pl.program_id(0)
    d = wq_ref.shape[-1]
    hd = d // HEADS
    scale = hd ** -0.5

    @pl.when(i == 0)
    def _():
        wq_bf[...] = wq_ref[0].astype(BF16)
        wo_bf[...] = wo_ref[0].astype(BF16)

    def attend(x):
        q = _dot(_rmsnorm(x, gn_ref[0]).astype(BF16), wq_bf[...])
        heads = []
        for h in range(HEADS):
            cols = slice(h * hd, (h + 1) * hd)
            qh = q[:, cols].astype(BF16)
            kh = k_ref[0, 0, :, cols].astype(BF16)
            s = lax.dot_general(qh, kh, (((1,), (1,)), ((), ())),
                                preferred_element_type=F32) * scale
            p = _softmax_rows(s, -1).astype(BF16)
            heads.append(_dot(p, v_ref[0, 0, :, cols].astype(BF16)))
        o = jnp.concatenate(heads, axis=1).astype(BF16)
        return x + _dot(o, wo_bf[...])

    @pl.when(i < n_prompt)
    def _():
        op_ref[...] = attend(xp_ref[...])

    @pl.when(i == n_prompt)
    def _():
        qs_ref[...] = _dot(_rmsnorm(xs_ref[...], gn_ref[0]).astype(BF16), wq_bf[...])


def _xattn_prompt(xp, xs, norm_xattn, w_xq, w_xo, mem_k, mem_v, layer):
    tp, d = xp.shape
    ns = xs.shape[0]
    depth, nb, n_mem, _ = mem_k.shape
    n_prompt = tp // TOKEN_TILE
    tiles_per_seq = n_prompt // nb
    assert tiles_per_seq * nb == n_prompt
    tile = lambda i: (jnp.minimum(i, n_prompt - 1), 0)
    kv = lambda i: (layer, jnp.minimum(i, n_prompt - 1) // tiles_per_seq, 0, 0)
    of_layer = lambda shape: pl.BlockSpec((1,) + shape, lambda i: (layer,) + (0,) * len(shape),
                                          pipeline_mode=pl.Buffered(1))
    return pl.pallas_call(
        functools.partial(_xattn_kernel, n_prompt=n_prompt),
        grid=(n_prompt + 1,),
        in_specs=[pl.BlockSpec((TOKEN_TILE, d), tile),
                  _resident((ns, d)), of_layer((1, d)), of_layer((d, d)), of_layer((d, d)),
                  pl.BlockSpec((1, 1, n_mem, d), kv), pl.BlockSpec((1, 1, n_mem, d), kv)],
        out_specs=(pl.BlockSpec((TOKEN_TILE, d), tile),
                   pl.BlockSpec((ns, d), lambda i: (0, 0))),
        out_shape=(jax.ShapeDtypeStruct((tp, d), F32), jax.ShapeDtypeStruct((ns, d), F32)),
        scratch_shapes=[pltpu.VMEM((d, d), BF16), pltpu.VMEM((d, d), BF16)],
        compiler_params=_params(1),
        name="xattn_prompt",
    )(xp, xs, norm_xattn.reshape(depth, 1, d), w_xq, w_xo, mem_k, mem_v)


def _xattn_sample_kernel(q_ref, k_ref, v_ref, o_ref, *, block, scale):
    i = pl.program_id(0)
    for b in range(block):
        r = i * block + b
        s = jnp.sum(k_ref[0, b] * q_ref[r][None], axis=-1, keepdims=True) * scale
        p = _softmax_rows(s, 0)
        o_ref[r] = jnp.sum(p * v_ref[0, b], axis=0)


def _xattn_sample(qs, cache_k, cache_v, layer):
    _, ns, n_mem, heads, hd = cache_k.shape
    block = SAMPLE_ATTN_BLOCK
    assert ns % block == 0
    kv = pl.BlockSpec((1, block, n_mem, heads, hd), lambda i: (layer, i, 0, 0, 0))
    return pl.pallas_call(
        functools.partial(_xattn_sample_kernel, block=block, scale=hd ** -0.5),
        grid=(ns // block,),
        in_specs=[_resident((ns, heads, hd)), kv, kv],
        out_specs=pl.BlockSpec((ns, heads, hd), lambda i: (0, 0, 0)),
        out_shape=jax.ShapeDtypeStruct((ns, heads, hd), F32),
        compiler_params=_params(1),
        name="xattn_sample",
    )(qs.reshape(ns, heads, hd), cache_k, cache_v).reshape(qs.shape)


def _attn_out_kernel(o_ref, xs_ref, wo_ref, os_ref):
    os_ref[...] = xs_ref[...] + _dot(o_ref[...].astype(BF16), wo_ref[0].astype(BF16))


def _attn_out_sample(o, xs, w_xo, layer):
    ns, d = xs.shape
    return pl.pallas_call(
        _attn_out_kernel,
        grid=(1,),
        in_specs=[_resident((ns, d)), _resident((ns, d)),
                  pl.BlockSpec((1, d, d), lambda i: (layer, 0, 0))],
        out_specs=pl.BlockSpec((ns, d), lambda i: (0, 0)),
        out_shape=jax.ShapeDtypeStruct((ns, d), F32),
        compiler_params=_params(1),
        name="attn_out_sample",
    )(o, xs, w_xo)


def _ffn_kernel(xp_ref, xs_ref, gn_ref, wg_ref, wu_ref, wd_ref, op_ref, os_ref, *, n_prompt):
    i = pl.program_id(0)
    f_dim = wg_ref.shape[1]

    def ffn(x):
        xn = _rmsnorm(x, gn_ref[...]).astype(BF16)
        acc = x
        for f in range(0, f_dim, FFN_F_CHUNK):
            cols = slice(f, f + FFN_F_CHUNK)
            hid = jax.nn.silu(_dot(xn, wg_ref[:, cols])) * _dot(xn, wu_ref[:, cols])
            acc = acc + _dot(hid.astype(BF16), wd_ref[cols, :])
        return acc

    @pl.when(i < n_prompt)
    def _():
        op_ref[...] = ffn(xp_ref[...])

    @pl.when(i == n_prompt)
    def _():
        os_ref[...] = ffn(xs_ref[...])


def _ffn(xp, xs, g_norm, w_gate, w_up, w_down):
    tp, d = xp.shape
    ns = xs.shape[0]
    f_dim = w_gate.shape[1]
    assert f_dim % FFN_F_CHUNK == 0
    n_prompt = tp // TOKEN_TILE
    wg = _cast_bf16(w_gate, d // 4)
    wu = _cast_bf16(w_up, d // 4)
    wd = _cast_bf16(w_down, f_dim // 4)
    tile = lambda i: (jnp.minimum(i, n_prompt - 1), 0)
    return pl.pallas_call(
        functools.partial(_ffn_kernel, n_prompt=n_prompt),
        grid=(n_prompt + 1,),
        in_specs=[pl.BlockSpec((TOKEN_TILE, d), tile),
                  _resident((ns, d)), _resident((1, d)),
                  _resident(wg.shape), _resident(wu.shape), _resident(wd.shape)],
        out_specs=(pl.BlockSpec((TOKEN_TILE, d), tile),
                   pl.BlockSpec((ns, d), lambda i: (0, 0))),
        out_shape=(jax.ShapeDtypeStruct((tp, d), F32), jax.ShapeDtypeStruct((ns, d), F32)),
        compiler_params=_params(1),
        name="dense_swiglu",
    )(xp, xs, g_norm.reshape(1, d), wg, wu, wd)


CONV_HALO = 32


SUBLANES = 8
CONV_UNROLL = 8
CONV_TAIL = 16


def _depthwise_conv(cbuf, y_ref, wdw_ref, bdw_ref, tm, taps):
    first = CONV_HALO - (taps - 1)
    d = y_ref.shape[1]
    max_a = (first + taps - 1) // SUBLANES
    classes = [[(a, SUBLANES * a + r - first) for a in range(max_a + 1)
                if 0 <= SUBLANES * a + r - first < taps] for r in range(SUBLANES)]
    row_i = lax.broadcasted_iota(jnp.int32, (SUBLANES, LANES), 0)

    for l in range(d // LANES):
        lanes = slice(l * LANES, (l + 1) * LANES)
        w = [jnp.broadcast_to(wdw_ref[k:k + 1, lanes], (SUBLANES, LANES)) for k in range(taps)]
        bias = jnp.broadcast_to(bdw_ref[:, lanes], (SUBLANES, LANES))

        def rotated_q(groups, j, w=w):
            out = []
            for r in range(SUBLANES):
                q = None
                for a, k in classes[r]:
                    term = w[k] * groups[j + a]
                    q = term if q is None else q + term
                out.append(q if r == 0 else pltpu.roll(q, SUBLANES - r, axis=0))
            return out

        head = {a: cbuf[SUBLANES * a:SUBLANES * (a + 1), lanes] for a in range(max_a + 1)}

        prev = rotated_q(head, 0)
        for blk in range(tm // (SUBLANES * CONV_UNROLL)):
            base = blk * SUBLANES * CONV_UNROLL
            groups = {j: cbuf[base + SUBLANES * j:base + SUBLANES * (j + 1), lanes]
                      for j in range(1, CONV_UNROLL + max_a + 1)}
            for u in range(CONV_UNROLL):
                nxt = rotated_q(groups, u + 1)
                y = prev[0] + bias
                for r in range(1, SUBLANES):
                    y = y + jnp.where(row_i < SUBLANES - r, prev[r], nxt[r])
                y_ref[base + SUBLANES * u:base + SUBLANES * (u + 1), lanes] = y
                prev = nxt


def _conv_prompt_kernel(xp_ref, gn_ref, w1_ref, wdw_ref, bdw_ref, lng_ref, lnb_ref, w2_ref,
                        op_ref, st_ref, w1_bf, w2_bf, cbuf, ybuf, *, tiles_per_seq):
    i = pl.program_id(0)
    d = w2_ref.shape[0]
    taps = wdw_ref.shape[0]
    tm = xp_ref.shape[0]

    @pl.when(i == 0)
    def _():
        w1_bf[...] = w1_ref[...].astype(BF16)
        w2_bf[...] = w2_ref[...].astype(BF16)
        cbuf[CONV_HALO + tm:CONV_HALO + tm + CONV_TAIL, :] = jnp.zeros((CONV_TAIL, d), F32)

    @pl.when(i % tiles_per_seq == 0)
    def _():
        cbuf[0:CONV_HALO, :] = jnp.zeros((CONV_HALO, d), F32)

    x = xp_ref[...]
    ag = _dot(_rmsnorm(x, gn_ref[...]).astype(BF16), w1_bf[...])
    c = ag[:, :d] * jax.nn.sigmoid(ag[:, d:])
    cbuf[CONV_HALO:CONV_HALO + tm, :] = c
    first = CONV_HALO - (taps - 1)
    _depthwise_conv(cbuf, ybuf, wdw_ref, bdw_ref, tm, taps)
    t = jax.nn.silu(_layernorm(ybuf[...], lng_ref[...], lnb_ref[...])).astype(BF16)
    op_ref[...] = x + _dot(t, w2_bf[...])
    cbuf[0:CONV_HALO, :] = cbuf[tm:tm + CONV_HALO, :]

    @pl.when(i % tiles_per_seq == tiles_per_seq - 1)
    def _():
        st_ref[0] = cbuf[first:CONV_HALO, :]


def _conv_prompt(xp, n_seq, g_norm, w_pw1, w_dw, b_dw, ln_g, ln_b, w_pw2):
    tp, d = xp.shape
    taps = w_dw.shape[0]
    n_prompt = tp // TOKEN_TILE
    tiles_per_seq = n_prompt // n_seq
    assert tiles_per_seq * n_seq == n_prompt and taps - 1 <= CONV_HALO <= TOKEN_TILE
    assert TOKEN_TILE % (SUBLANES * CONV_UNROLL) == 0 and CONV_TAIL >= SUBLANES * 2
    row = lambda a: a.reshape(1, -1)
    return pl.pallas_call(
        functools.partial(_conv_prompt_kernel, tiles_per_seq=tiles_per_seq),
        grid=(n_prompt,),
        in_specs=[pl.BlockSpec((TOKEN_TILE, d), lambda i: (i, 0)),
                  _resident((1, d)), _resident(w_pw1.shape), _resident(w_dw.shape),
                  _resident((1, d)), _resident((1, d)), _resident((1, d)), _resident(w_pw2.shape)],
        out_specs=(pl.BlockSpec((TOKEN_TILE, d), lambda i: (i, 0)),
                   pl.BlockSpec((1, taps - 1, d), lambda i: (i // tiles_per_seq, 0, 0))),
        out_shape=(jax.ShapeDtypeStruct((tp, d), F32),
                   jax.ShapeDtypeStruct((n_seq, taps - 1, d), F32)),
        scratch_shapes=[pltpu.VMEM(w_pw1.shape, BF16), pltpu.VMEM(w_pw2.shape, BF16),
                        pltpu.VMEM((CONV_HALO + TOKEN_TILE + CONV_TAIL, d), F32),
                        pltpu.VMEM((TOKEN_TILE, d), F32)],
        compiler_params=_params(1),
        name="conv_prompt",
    )(xp, row(g_norm), w_pw1, w_dw, row(b_dw), row(ln_g), row(ln_b), w_pw2)


def _conv_sample_kernel(xs_ref, gn_ref, w1_ref, wdw_ref, bdw_ref, lng_ref, lnb_ref, w2_ref, st_ref,
                        os_ref, sto_ref, c_all, t_all, *, block):
    i = pl.program_id(0)
    d = w2_ref.shape[0]
    taps = wdw_ref.shape[0]

    @pl.when(i == 0)
    def _():
        ag = _dot(_rmsnorm(xs_ref[...], gn_ref[...]).astype(BF16), w1_ref[...].astype(BF16))
        c_all[...] = ag[:, :d] * jax.nn.sigmoid(ag[:, d:])

    w_hist = wdw_ref[0:taps - 1, :]
    w_last = wdw_ref[taps - 1:taps, :]
    for b in range(block):
        r = i * block + b
        hist = st_ref[0, b]
        cb = c_all[pl.ds(r, 1), :]
        y = jnp.sum(hist * w_hist, axis=0, keepdims=True) + cb * w_last + bdw_ref[...]
        t_all[pl.ds(r, 1), :] = jax.nn.silu(_layernorm(y, lng_ref[...], lnb_ref[...]))
        sto_ref[0, b, 0:taps - 2, :] = hist[1:taps - 1, :]
        sto_ref[0, b, taps - 2:taps - 1, :] = cb

    @pl.when(i == pl.num_programs(0) - 1)
    def _():
        os_ref[...] = xs_ref[...] + _dot(t_all[...].astype(BF16), w2_ref[...].astype(BF16))


def _conv_sample(xs, state, g_norm, w_pw1, w_dw, b_dw, ln_g, ln_b, w_pw2):
    ns, d = xs.shape
    taps = w_dw.shape[0]
    block = SAMPLE_CONV_BLOCK
    assert ns % block == 0
    row = lambda a: a.reshape(1, -1)
    st = pl.BlockSpec((1, block, taps - 1, d), lambda i: (0, i, 0, 0))
    return pl.pallas_call(
        functools.partial(_conv_sample_kernel, block=block),
        grid=(ns // block,),
        in_specs=[_resident((ns, d)), _resident((1, d)), _resident(w_pw1.shape),
                  _resident(w_dw.shape), _resident((1, d)), _resident((1, d)), _resident((1, d)),
                  _resident(w_pw2.shape), st],
        out_specs=(pl.BlockSpec((ns, d), lambda i: (0, 0)), st),
        out_shape=(jax.ShapeDtypeStruct((ns, d), F32), jax.ShapeDtypeStruct(state.shape, F32)),
        scratch_shapes=[pltpu.VMEM((ns, d), F32), pltpu.VMEM((ns, d), F32)],
        compiler_params=_params(1),
        name="conv_sample",
    )(xs, row(g_norm), w_pw1, w_dw, row(b_dw), row(ln_g), row(ln_b), w_pw2, state)


def _router_kernel(xp_ref, xs_ref, gn_ref, wr_ref, hn_ref, info_ref, gates_ref, seen_ref, count_ref, seen,
                   *, n_prompt, n_experts):
    i = pl.program_id(0)

    def route(x):
        xn = _rmsnorm(x, gn_ref[...])
        xh = xn.astype(BF16)
        xl = (xn - xh.astype(F32)).astype(BF16)
        logits = _dot(jnp.concatenate([xh, xl, xh], axis=1), wr_ref[...])
        lane = lax.broadcasted_iota(jnp.int32, logits.shape, 1).astype(F32)
        neg = jnp.float32(-jnp.inf)
        logits = jnp.where(lane < n_experts, logits, neg)
        v1 = jnp.max(logits, axis=-1, keepdims=True)
        i1 = jnp.min(jnp.where(logits == v1, lane, float(LANES)), axis=-1, keepdims=True)
        rest = jnp.where(lane == i1, neg, logits)
        v2 = jnp.max(rest, axis=-1, keepdims=True)
        i2 = jnp.min(jnp.where(rest == v2, lane, float(LANES)), axis=-1, keepdims=True)
        e2 = jnp.exp(v2 - v1)
        denom = 1.0 + e2

        rows = x.shape[0]
        pick1 = jnp.where(lane == i1, 1.0, 0.0)
        pick2 = jnp.where(lane == i2, 1.0, 0.0)
        picks = pick1 + pick2
        r_i = lax.broadcasted_iota(jnp.int32, (rows, rows), 0)
        c_i = lax.broadcasted_iota(jnp.int32, (rows, rows), 1)
        earlier = jnp.where(c_i < r_i, 1.0, 0.0).astype(BF16)
        before = _dot(earlier, picks.astype(BF16)) + seen[0:1, :]
        rank1 = jnp.sum(pick1 * before, axis=-1, keepdims=True)
        rank2 = jnp.sum(pick2 * before, axis=-1, keepdims=True)
        total = seen[0:1, :] + jnp.sum(picks, axis=0, keepdims=True)

        info = jnp.where(lane == 0.0, i1, jnp.where(lane == 1.0, i2,
                         jnp.where(lane == 2.0, rank1, jnp.where(lane == 3.0, rank2, 0.0))))
        gates = jnp.where(lane == 0.0, 1.0 / denom, jnp.where(lane == 1.0, e2 / denom, 0.0))
        starts = [seen[0:1, :]]
        for w in range(1, TOKEN_TILE // DISPATCH_WIN):
            starts.append(before[w * DISPATCH_WIN:w * DISPATCH_WIN + 1, :] if w * DISPATCH_WIN < rows else total)
        pad = jnp.zeros((SUBLANES - len(starts), LANES), F32)
        seen_ref[0] = jnp.concatenate(starts + [pad], axis=0).astype(jnp.int32)
        seen[0:1, :] = total
        return xn.astype(BF16), info.astype(jnp.int32), gates

    @pl.when(i == 0)
    def _():
        seen[...] = jnp.zeros(seen.shape, F32)

    @pl.when(i < n_prompt)
    def _():
        hn, info, gates = route(xp_ref[...])
        hn_ref[...] = hn
        info_ref[...] = info
        gates_ref[...] = gates

    @pl.when(i == n_prompt)
    def _():
        ns = xs_ref.shape[0]
        hn, info, gates = route(xs_ref[...])
        hn_ref[...] = jnp.zeros(hn_ref.shape, BF16)
        info_ref[...] = jnp.zeros(info_ref.shape, jnp.int32)
        gates_ref[...] = jnp.zeros(gates_ref.shape, F32)
        hn_ref[0:ns, :] = hn
        info_ref[0:ns, :] = info
        gates_ref[0:ns, :] = gates
        count_ref[...] = jnp.broadcast_to(seen[0:1, :], count_ref.shape).astype(jnp.int32)

    @pl.when(i > n_prompt)
    def _():
        hn_ref[...] = jnp.zeros(hn_ref.shape, BF16)
        info_ref[...] = jnp.zeros(info_ref.shape, jnp.int32)
        gates_ref[...] = jnp.zeros(gates_ref.shape, F32)
        seen_ref[0] = jnp.broadcast_to(seen[0:1, :], (SUBLANES, LANES)).astype(jnp.int32)


def _router(xp, xs, g_norm, w_router):
    tp, d = xp.shape
    ns = xs.shape[0]
    n_experts = w_router.shape[1]
    n_prompt = tp // TOKEN_TILE
    n_tiles = n_prompt + 1 + -(-(DISPATCH_SPAN - 1) * DISPATCH_WIN // TOKEN_TILE)
    t_pad = n_tiles * TOKEN_TILE
    assert TOKEN_TILE % DISPATCH_WIN == 0 and TOKEN_TILE // DISPATCH_WIN <= SUBLANES
    wr = jnp.pad(w_router, ((0, 0), (0, LANES - n_experts)))
    wr_hi = wr.astype(BF16)
    wr_lo = (wr - wr_hi.astype(F32)).astype(BF16)
    wr = jnp.concatenate([wr_hi, wr_hi, wr_lo], axis=0)
    tile = lambda i: (jnp.minimum(i, n_prompt - 1), 0)
    lanes_spec = pl.BlockSpec((TOKEN_TILE, LANES), lambda i: (i, 0))
    return pl.pallas_call(
        functools.partial(_router_kernel, n_prompt=n_prompt, n_experts=n_experts),
        grid=(n_tiles,),
        in_specs=[pl.BlockSpec((TOKEN_TILE, d), tile),
                  _resident((ns, d)), _resident((1, d)), _resident(wr.shape)],
        out_specs=(pl.BlockSpec((TOKEN_TILE, d), lambda i: (i, 0)), lanes_spec, lanes_spec,
                   pl.BlockSpec((1, SUBLANES, LANES), lambda i: (i, 0, 0)),
                   pl.BlockSpec((SUBLANES, LANES), lambda i: (0, 0))),
        out_shape=(jax.ShapeDtypeStruct((t_pad, d), BF16),
                   jax.ShapeDtypeStruct((t_pad, LANES), jnp.int32),
                   jax.ShapeDtypeStruct((t_pad, LANES), F32),
                   jax.ShapeDtypeStruct((n_tiles, SUBLANES, LANES), jnp.int32),
                   jax.ShapeDtypeStruct((SUBLANES, LANES), jnp.int32)),
        scratch_shapes=[pltpu.VMEM((SUBLANES, LANES), F32)],
        compiler_params=_params(1),
        name="moe_router",
    )(xp, xs, g_norm.reshape(1, d), wr)


def _lane_tile(a, width):
    return jnp.concatenate([a] * (width // LANES), axis=1)


def _dispatch_kernel(slot_nsub, win_lo, win_hi, pos_ref, gate_ref, hn_ref, x_ref, g_ref):
    s = pl.program_id(0)
    n_sub = slot_nsub[s]
    subs_per_slot = SLOT_ROWS // SUB_ROWS
    d = hn_ref.shape[1]
    span = DISPATCH_SPAN * DISPATCH_WIN
    row_in_block = lax.broadcasted_iota(jnp.int32, (SUB_ROWS, span), 0)

    def sub_block(j, carry):
        rows = pl.ds(pl.multiple_of(j * SUB_ROWS, SUB_ROWS), SUB_ROWS)
        row_id = row_in_block + (s * SLOT_ROWS + j * SUB_ROWS)
        k = s * subs_per_slot + j
        x_ref[rows, :] = jnp.zeros((SUB_ROWS, d), BF16)
        g_ref[rows, :] = jnp.zeros((SUB_ROWS, LANES), F32)

        def windows(c, carry):
            w = win_lo[k] + c * DISPATCH_SPAN
            along = lambda ref, choice: jnp.concatenate(
                [ref[w + n, choice:choice + 1, :] for n in range(DISPATCH_SPAN)], axis=1)
            hit1 = along(pos_ref, 0) == row_id
            hit2 = along(pos_ref, 1) == row_id
            sel = jnp.where(hit1, 1.0, jnp.where(hit2, 1.0, 0.0)).astype(BF16)
            gates = jnp.where(hit1, along(gate_ref, 0), jnp.where(hit2, along(gate_ref, 1), 0.0))
            base = pl.multiple_of(w * DISPATCH_WIN, DISPATCH_WIN)
            x_ref[rows, :] += _dot(sel, hn_ref[pl.ds(base, span), :]).astype(BF16)
            g_ref[rows, :] += jnp.broadcast_to(jnp.sum(gates, axis=-1, keepdims=True), (SUB_ROWS, LANES))
            return carry

        n_win = win_hi[k] + 1 - win_lo[k]
        lax.fori_loop(0, (n_win + DISPATCH_SPAN - 1) // DISPATCH_SPAN, windows, 0)
        return carry

    def zero_block(j, carry):
        rows = pl.ds(pl.multiple_of(j * SUB_ROWS, SUB_ROWS), SUB_ROWS)
        x_ref[rows, :] = jnp.zeros((SUB_ROWS, d), BF16)
        g_ref[rows, :] = jnp.zeros((SUB_ROWS, LANES), F32)
        return carry

    lax.fori_loop(0, n_sub, sub_block, 0)
    lax.fori_loop(n_sub, subs_per_slot, zero_block, 0)


def _dispatch(hn, pos_win, gate_win, slot_nsub, win_lo, win_hi):
    t_pad, d = hn.shape
    n_slots = slot_nsub.shape[0]
    assert t_pad % DISPATCH_WIN == 0
    grid_spec = pltpu.PrefetchScalarGridSpec(
        num_scalar_prefetch=3,
        grid=(n_slots,),
        in_specs=[pl.BlockSpec(pos_win.shape, lambda s, *_: (0, 0, 0)),
                  pl.BlockSpec(gate_win.shape, lambda s, *_: (0, 0, 0)),
                  pl.BlockSpec((t_pad, d), lambda s, *_: (0, 0), pipeline_mode=pl.Buffered(1))],
        out_specs=(pl.BlockSpec((SLOT_ROWS, d), lambda s, *_: (s, 0)),
                   pl.BlockSpec((SLOT_ROWS, LANES), lambda s, *_: (s, 0))),
    )
    return pl.pallas_call(
        _dispatch_kernel,
        grid_spec=grid_spec,
        out_shape=(jax.ShapeDtypeStruct((n_slots * SLOT_ROWS, d), BF16),
                   jax.ShapeDtypeStruct((n_slots * SLOT_ROWS, LANES), F32)),
        compiler_params=_params(1),
        name="moe_dispatch",
    )(slot_nsub, win_lo, win_hi, pos_win, gate_win, hn)


def _expert_kernel(slot_expert, slot_nsub, x_ref, gate_ref, wg_ref, wu_ref, wd_ref, y_ref,
                   wg_bf, wu_bf, wd_bf, acc):
    s = pl.program_id(0)
    f = pl.program_id(1)
    n_sub = slot_nsub[s]
    last_f = f == pl.num_programs(1) - 1
    d = y_ref.shape[1]

    @pl.when(n_sub > 0)
    def _():
        wg_bf[...] = wg_ref[0].astype(BF16)
        wu_bf[...] = wu_ref[0].astype(BF16)
        wd_bf[...] = wd_ref[0].astype(BF16)

    def zero_acc(j, carry):
        rows = pl.ds(pl.multiple_of(j * SUB_ROWS, SUB_ROWS), SUB_ROWS)
        acc[rows, :] = jnp.zeros((SUB_ROWS, d), F32)
        return carry

    @pl.when(f == 0)
    def _():
        lax.fori_loop(0, n_sub, zero_acc, 0)

    def block(start, size):
        rows = pl.ds(pl.multiple_of(start, SUB_ROWS), size)
        xg = x_ref[rows, :]
        hid = jax.nn.silu(_dot(xg, wg_bf[...])) * _dot(xg, wu_bf[...])
        acc[rows, :] += _dot(hid.astype(BF16), wd_bf[...])

        @pl.when(last_f)
        def _():
            y_ref[rows, :] = (acc[rows, :] * _lane_tile(gate_ref[rows, :], d)).astype(y_ref.dtype)

    def quad(j, carry):
        block(j * (4 * SUB_ROWS), 4 * SUB_ROWS)
        return carry

    lax.fori_loop(0, n_sub // 4, quad, 0)

    @pl.when(n_sub % 4 >= 2)
    def _():
        block((n_sub // 4) * (4 * SUB_ROWS), 2 * SUB_ROWS)

    @pl.when(n_sub % 2 == 1)
    def _():
        block((n_sub - 1) * SUB_ROWS, SUB_ROWS)

    def zero_block(j, carry):
        rows = pl.ds(pl.multiple_of(j * SUB_ROWS, SUB_ROWS), SUB_ROWS)
        y_ref[rows, :] = jnp.zeros((SUB_ROWS, d), y_ref.dtype)
        return carry

    @pl.when(last_f)
    def _():
        lax.fori_loop(n_sub, SLOT_ROWS // SUB_ROWS, zero_block, 0)


def _expert_ffn(x_slots, gate_rep, slot_expert, slot_nsub, w_gate, w_up, w_down):
    n_slots = slot_expert.shape[0]
    d = x_slots.shape[1]
    f_dim = w_gate.shape[2]
    tf = EXPERT_F_TILE
    assert f_dim % tf == 0 and SLOT_ROWS % SUB_ROWS == 0
    n_f = f_dim // tf
    f_tile = lambda s, f, sn: jnp.where(sn[s] > 0, f, n_f - 1)
    grid_spec = pltpu.PrefetchScalarGridSpec(
        num_scalar_prefetch=2,
        grid=(n_slots, f_dim // tf),
        in_specs=[pl.BlockSpec((SLOT_ROWS, d), lambda s, f, se, sn: (s, 0)),
                  pl.BlockSpec((SLOT_ROWS, LANES), lambda s, f, se, sn: (s, 0)),
                  pl.BlockSpec((1, d, tf), lambda s, f, se, sn: (se[s], 0, f_tile(s, f, sn))),
                  pl.BlockSpec((1, d, tf), lambda s, f, se, sn: (se[s], 0, f_tile(s, f, sn))),
                  pl.BlockSpec((1, tf, d), lambda s, f, se, sn: (se[s], f_tile(s, f, sn), 0))],
        out_specs=pl.BlockSpec((SLOT_ROWS, d), lambda s, f, se, sn: (s, 0)),
        scratch_shapes=[pltpu.VMEM((d, tf), BF16), pltpu.VMEM((d, tf), BF16),
                        pltpu.VMEM((tf, d), BF16), pltpu.VMEM((SLOT_ROWS, d), F32)],
    )
    return pl.pallas_call(
        _expert_kernel,
        grid_spec=grid_spec,
        out_shape=jax.ShapeDtypeStruct((n_slots * SLOT_ROWS, d), BF16),
        compiler_params=_params(2),
        name="expert_swiglu",
    )(slot_expert, slot_nsub, x_slots, gate_rep, w_gate, w_up, w_down)


def _combine_kernel(n_blocks, block_ids, xp_ref, xs_ref, pos1_ref, pos2_ref, gf_ref, *rest,
                    n_prompt, max_blocks):
    y_refs = rest[:max_blocks]
    op_ref, os_ref = rest[max_blocks:]
    i = pl.program_id(0)

    def combine(x, o_ref):
        rows = x.shape[0]
        p1 = _lane_tile(pos1_ref[0:rows, :], COMBINE_BLOCK)
        p2 = _lane_tile(pos2_ref[0:rows, :], COMBINE_BLOCK)
        lane = lax.broadcasted_iota(jnp.int32, (rows, COMBINE_BLOCK), 1)

        def picked(blocks):
            total = None
            for b in blocks:
                first_row = jnp.where(b < n_blocks[i], block_ids[i * max_blocks + b] * COMBINE_BLOCK,
                                      -2 * COMBINE_BLOCK)
                row_id = lane + first_row
                sel = jnp.where(p1 == row_id, 1.0, jnp.where(p2 == row_id, 1.0, 0.0)).astype(BF16)
                part = _dot(sel, y_refs[b][...])
                total = part if total is None else total + part
            return total

        o_ref[...] = x + picked(range(0, min(COMBINE_ALWAYS, max_blocks)))
        for g in range(COMBINE_ALWAYS, max_blocks, COMBINE_GROUP):
            @pl.when(g < n_blocks[i])
            def _():
                o_ref[...] += picked(range(g, min(g + COMBINE_GROUP, max_blocks)))
        o_ref[...] = _rmsnorm(o_ref[...], gf_ref[...])

    @pl.when(i < n_prompt)
    def _():
        combine(xp_ref[...], op_ref)

    @pl.when(i == n_prompt)
    def _():
        combine(xs_ref[...], os_ref)


def _combine(xp, xs, pos_rep, y, n_blocks, block_ids, g_final, max_blocks):
    tp, d = xp.shape
    ns = xs.shape[0]
    n_prompt = tp // TOKEN_TILE
    tile = lambda i, *_: (jnp.minimum(i, n_prompt - 1), 0)
    pos_spec = pl.BlockSpec((TOKEN_TILE, LANES), lambda i, *_: (i, 0))

    def y_spec(b):
        return pl.BlockSpec((COMBINE_BLOCK, d), lambda i, nb, ids: (ids[i * max_blocks + b], 0))

    grid_spec = pltpu.PrefetchScalarGridSpec(
        num_scalar_prefetch=2,
        grid=(n_prompt + 1,),
        in_specs=[pl.BlockSpec((TOKEN_TILE, d), tile),
                  pl.BlockSpec((ns, d), lambda i, *_: (0, 0)),
                  pos_spec, pos_spec,
                  pl.BlockSpec((1, d), lambda i, *_: (0, 0))] + [y_spec(b) for b in range(max_blocks)],
        out_specs=(pl.BlockSpec((TOKEN_TILE, d), tile),
                   pl.BlockSpec((ns, d), lambda i, *_: (0, 0))),
    )
    return pl.pallas_call(
        functools.partial(_combine_kernel, n_prompt=n_prompt, max_blocks=max_blocks),
        grid_spec=grid_spec,
        out_shape=(jax.ShapeDtypeStruct((tp, d), F32), jax.ShapeDtypeStruct((ns, d), F32)),
        compiler_params=_params(1),
        name="moe_combine",
    )(n_blocks, block_ids, xp, xs, pos_rep[0], pos_rep[1], g_final.reshape(1, d), *([y] * max_blocks))


def _routing_tables(info, gates, seen, counts, t_valid, n_experts):
    i32 = jnp.int32
    t_pad = info.shape[0]
    n_tiles = seen.shape[0]
    wins_per_tile = TOKEN_TILE // DISPATCH_WIN
    subs_per_slot = SLOT_ROWS // SUB_ROWS
    n_slots = t_valid * TOP_K // SLOT_ROWS + n_experts + 1
    experts = jnp.arange(n_experts, dtype=i32)

    cnt = counts[0, :n_experts]
    n_sub = (cnt + SUB_ROWS - 1) // SUB_ROWS
    n_chunks = (n_sub + subs_per_slot - 1) // subs_per_slot
    chunk_end = jnp.cumsum(n_chunks)
    slot_base = chunk_end - n_chunks
    n_used = chunk_end[-1]
    div_chunks = jnp.maximum(n_chunks, 1)
    div_sub = jnp.maximum(n_sub, 1)

    def slot_row(e, rank):
        pick = lambda tab: jnp.sum(jnp.where(e[..., None] == experts, tab, 0), axis=-1)
        subs, chunks, base = pick(div_sub), pick(div_chunks), pick(slot_base)
        sub = rank // SUB_ROWS
        chunk = ((sub + 1) * chunks + subs - 1) // subs - 1
        first_sub = chunk * pick(n_sub) // chunks
        return (base + chunk) * SLOT_ROWS + rank - first_sub * SUB_ROWS

    sid = jnp.arange(n_slots, dtype=i32)
    expert_of = lambda s: jnp.sum((s[..., None] >= chunk_end).astype(i32), axis=-1)
    slot_expert = jnp.where(sid < n_used, expert_of(sid), expert_of(n_used - 1)).astype(i32)
    of_slot = lambda tab: jnp.sum(jnp.where(slot_expert[:, None] == experts, tab, 0), axis=-1)
    run = sid - of_slot(slot_base)
    run_lo = run * of_slot(n_sub) // of_slot(div_chunks)
    run_hi = (run + 1) * of_slot(n_sub) // of_slot(div_chunks)
    slot_nsub = jnp.where(sid < n_used, run_hi - run_lo, 0).astype(i32)

    token_ok = jnp.arange(t_pad, dtype=i32) < t_valid

    def entry_row(e, rank):
        sub = rank // SUB_ROWS
        row = jnp.full((t_pad,), -1, i32)
        for s in range(n_slots):
            mine = (e == slot_expert[s]) & (sub >= run_lo[s]) & (sub < run_hi[s]) & token_ok
            row = jnp.where(mine, s * SLOT_ROWS + rank - run_lo[s] * SUB_ROWS, row)
        return row

    pos = [entry_row(info[:, c], info[:, TOP_K + c]) for c in range(TOP_K)]
    n_win = t_pad // DISPATCH_WIN
    by_window = lambda cols: jnp.stack([a.reshape(n_win, DISPATCH_WIN) for a in cols], axis=1)
    pos_win = by_window(pos)
    gate_win = by_window([gates[:, c] for c in range(TOP_K)])

    win_seen = seen[:, :wins_per_tile, :n_experts].reshape(n_win, n_experts)
    seen_slot = jnp.sum(jnp.where(slot_expert[:, None, None] == experts, win_seen[None], 0), axis=-1)
    j = jnp.arange(subs_per_slot, dtype=i32)[None, :]
    rank_lo = (run_lo[:, None] + j) * SUB_ROWS
    rank_hi = jnp.minimum(rank_lo + SUB_ROWS, of_slot(cnt)[:, None]) - 1
    window_of = lambda r: jnp.sum((seen_slot[:, None, :] <= r[:, :, None]).astype(i32), axis=-1) - 1
    active = j < slot_nsub[:, None]
    win_lo = jnp.where(active, window_of(rank_lo), 1).reshape(-1)
    win_hi = jnp.where(active, window_of(rank_hi), 0).reshape(-1)

    tile_lo = seen[:, 0, :n_experts]
    tile_hi = jnp.concatenate([tile_lo[1:], cnt[None, :]], axis=0)
    blk_lo = tile_lo // COMBINE_BLOCK
    per_expert = jnp.where(tile_hi > tile_lo, (tile_hi - 1) // COMBINE_BLOCK - blk_lo + 1, 0)
    ends = jnp.cumsum(per_expert, axis=1)
    n_blocks = ends[:, -1]
    max_blocks = TOKEN_TILE * TOP_K // COMBINE_BLOCK + 2 * n_experts
    b = jnp.arange(max_blocks, dtype=i32)[None, :]
    e_of_b = jnp.minimum(jnp.sum((ends[:, None, :] <= b[:, :, None]).astype(i32), axis=-1), n_experts - 1)
    take = lambda tab: jnp.sum(jnp.where(e_of_b[:, :, None] == experts, tab[:, None, :], 0), axis=-1)
    rank_block = take(blk_lo) + b - (take(ends) - take(per_expert))
    ids = slot_row(e_of_b, rank_block * COMBINE_BLOCK) // COMBINE_BLOCK
    valid = b < n_blocks[:, None]
    id_bits = 1 << 16
    assert n_slots * subs_per_slot < id_bits
    keyed = jnp.where(valid, jnp.arange(n_tiles, dtype=i32)[:, None] * id_bits + ids, 0)
    block_ids = (lax.cummax(keyed, axis=0) % id_bits).reshape(-1)

    rep = lambda a: jnp.broadcast_to(a[:, None], (t_pad, LANES))
    return dict(slot_expert=slot_expert, slot_nsub=slot_nsub, pos_win=pos_win, gate_win=gate_win,
                win_lo=win_lo.astype(i32), win_hi=win_hi.astype(i32),
                pos_rep=(rep(pos[0]), rep(pos[1])),
                n_blocks=n_blocks.astype(i32), block_ids=block_ids.astype(i32), max_blocks=max_blocks)


def _moe(xp, xs, g_norm, w_router, w_gate, w_up, w_down, g_final):
    tp, d = xp.shape
    ns = xs.shape[0]
    n_experts = w_router.shape[1]
    assert ns <= TOKEN_TILE
    hn, info, gates, seen, counts = _router(xp, xs, g_norm, w_router)
    t = _routing_tables(info, gates, seen, counts, tp + ns, n_experts)
    x_slots, gate_rep = _dispatch(hn, t["pos_win"], t["gate_win"], t["slot_nsub"], t["win_lo"], t["win_hi"])
    y = _expert_ffn(x_slots, gate_rep, t["slot_expert"], t["slot_nsub"], w_gate, w_up, w_down)
    return _combine(xp, xs, t["pos_rep"], y, t["n_blocks"], t["block_ids"], g_final, t["max_blocks"])


def kernel(x_prompt, x_sample, mem_prompt, cache_mem_k, cache_mem_v, state_conv, norm_mix, norm_xattn, norm_ffn, norm_mem, norm_final, w_xq, w_xk, w_xv, w_xo, a_w_in, a_ln_g, a_ln_b, a_w_s, a_b_s, a_w_out, b_w_pw1, b_w_dw, b_b_dw, b_ln_g, b_ln_b, b_w_pw2, ffn_w_gate, ffn_w_up, ffn_w_down, moe_w_router, moe_w_gate, moe_w_up, moe_w_down):
    nb, seq, d = x_prompt.shape
    ns = x_sample.shape[0]
    depth = norm_mix.shape[0]
    n_mem = mem_prompt.shape[1]
    assert depth == 2 and x_sample.shape[1] == 1

    mem_k, mem_v, mem_k_heads, mem_v_heads = _memory_kv(mem_prompt, norm_mem, w_xk, w_xv)

    hp = x_prompt.reshape(nb * seq, d)
    hs = x_sample.reshape(ns, d)

    def cross_attention(hp, hs, layer):
        hp_new, qs = _xattn_prompt(hp, hs, norm_xattn, w_xq, w_xo, mem_k, mem_v, layer)
        o = _xattn_sample(qs, cache_mem_k, cache_mem_v, layer)
        hs_new = _attn_out_sample(o, hs, w_xo, layer)
        return hp_new, hs_new

    hp, hs, v_sample = _gmlp(hp, hs, norm_mix[0], a_w_in[0], a_ln_g[0], a_ln_b[0],
                             a_w_s[0], a_b_s[0], a_w_out[0])
    hp, hs = cross_attention(hp, hs, 0)
    hp, hs = _ffn(hp, hs, norm_ffn[0], ffn_w_gate[0], ffn_w_up[0], ffn_w_down[0])

    conv_w = (norm_mix[1], b_w_pw1[0], b_w_dw[0], b_b_dw[0], b_ln_g[0], b_ln_b[0], b_w_pw2[0])
    hp, conv_state_prompt = _conv_prompt(hp, nb, *conv_w)
    hs, conv_state_sample = _conv_sample(hs, state_conv, *conv_w)
    hp, hs = cross_attention(hp, hs, 1)
    yp, ys = _moe(hp, hs, norm_ffn[1], moe_w_router[0], moe_w_gate[0], moe_w_up[0], moe_w_down[0],
                  norm_final)

    return (yp.reshape(nb, seq, d),
            ys.reshape(ns, 1, d),
            mem_k_heads,
            mem_v_heads,
            conv_state_prompt[None],
            conv_state_sample,
            v_sample.reshape(1, ns, 1, -1))
```

```python
import functools

import jax
import jax.numpy as jnp
from jax import lax
from jax.experimental import pallas as pl
from jax.experimental.pallas import tpu as pltpu

F32 = jnp.float32
BF16 = jnp.bfloat16

RMS_EPS = 1e-6
LN_EPS = 1e-5
CHUNK = 128
GROUPS = 8
HEADS = 4
TOP_K = 2
LANES = 128
V7X_VMEM_LIMIT = 56 * 1024 * 1024

TOKEN_TILE = 512
SAMPLE_ATTN_BLOCK = 8
MEMKV_SEQS = 2
SAMPLE_CONV_BLOCK = 16
SLOT_ROWS = 2304
SUB_ROWS = 256
EXPERT_F_TILE = 512
DISPATCH_WIN = 256
DISPATCH_SPAN = 5
COMBINE_BLOCK = 256
COMBINE_ALWAYS = 12
COMBINE_GROUP = 4
FFN_F_CHUNK = 256


def _params(n_axes, vmem=V7X_VMEM_LIMIT):
    return pltpu.CompilerParams(dimension_semantics=("arbitrary",) * n_axes,
                                vmem_limit_bytes=vmem)


def _resident(shape):
    nd = len(shape)
    return pl.BlockSpec(shape, lambda *_: (0,) * nd, pipeline_mode=pl.Buffered(1))


def _rmsnorm(x, g):
    return x * lax.rsqrt(jnp.mean(x * x, axis=-1, keepdims=True) + RMS_EPS) * g


def _layernorm(x, g, b):
    xc = x - jnp.mean(x, axis=-1, keepdims=True)
    var = jnp.mean(xc * xc, axis=-1, keepdims=True)
    return xc * lax.rsqrt(var + LN_EPS) * g + b


def _dot(a, b):
    return jnp.dot(a, b, preferred_element_type=F32)


def _cast_kernel(x_ref, o_ref):
    o_ref[...] = x_ref[...].astype(o_ref.dtype)


def _cast_bf16(w, rows_per_step):
    r, c = w.shape
    return pl.pallas_call(
        _cast_kernel,
        grid=(r // rows_per_step,),
        in_specs=[pl.BlockSpec((rows_per_step, c), lambda i: (i, 0))],
        out_specs=pl.BlockSpec((rows_per_step, c), lambda i: (i, 0)),
        out_shape=jax.ShapeDtypeStruct((r, c), BF16),
        compiler_params=_params(1),
        name="cast_bf16",
    )(w)


def _memkv_kernel(mem_ref, g_ref, wk_ref, wv_ref, k_ref, v_ref, kh_ref, vh_ref, wk_bf, wv_bf):
    @pl.when(pl.program_id(1) == 0)
    def _():
        wk_bf[...] = wk_ref[0].astype(BF16)
        wv_bf[...] = wv_ref[0].astype(BF16)

    hd = kh_ref.shape[-1]
    nb_step, n_mem, d = mem_ref.shape
    mn = _rmsnorm(mem_ref[...].reshape(nb_step * n_mem, d), g_ref[0]).astype(BF16)
    k = _dot(mn, wk_bf[...])
    v = _dot(mn, wv_bf[...])
    for b in range(nb_step):
        rows = slice(b * n_mem, (b + 1) * n_mem)
        k_ref[0, b] = k[rows]
        v_ref[0, b] = v[rows]
        for h in range(HEADS):
            kh_ref[0, b, :, h, :] = k[rows, h * hd:(h + 1) * hd]
            vh_ref[0, b, :, h, :] = v[rows, h * hd:(h + 1) * hd]


def _memory_kv(mem, norm_mem, w_xk, w_xv):
    depth, d, _ = w_xk.shape
    nb, n_mem, _ = mem.shape
    hd = d // HEADS
    flat = jax.ShapeDtypeStruct((depth, nb, n_mem, d), F32)
    heads = jax.ShapeDtypeStruct((depth, nb, n_mem, HEADS, hd), F32)
    step = MEMKV_SEQS if nb % MEMKV_SEQS == 0 else 1
    w_spec = pl.BlockSpec((1, d, d), lambda l, b: (l, 0, 0))
    flat_spec = pl.BlockSpec((1, step, n_mem, d), lambda l, b: (l, b, 0, 0))
    heads_spec = pl.BlockSpec((1, step, n_mem, HEADS, hd), lambda l, b: (l, b, 0, 0, 0))
    return pl.pallas_call(
        _memkv_kernel,
        grid=(depth, nb // step),
        in_specs=[pl.BlockSpec((step, n_mem, d), lambda l, b: (b, 0, 0)),
                  pl.BlockSpec((1, 1, d), lambda l, b: (l, 0, 0)),
                  w_spec, w_spec],
        out_specs=(flat_spec, flat_spec, heads_spec, heads_spec),
        out_shape=(flat, flat, heads, heads),
        scratch_shapes=[pltpu.VMEM((d, d), BF16), pltpu.VMEM((d, d), BF16)],
        compiler_params=_params(2),
        name="memory_kv",
    )(mem, norm_mem.reshape(depth, 1, d), w_xk, w_xv)


def _gmlp_kernel(xp_ref, xs_ref, gn_ref, win_ref, lng_ref, lnb_ref, wmix_ref, bias_ref, wout_ref,
                 op_ref, os_ref, vs_ref, win_bf, wout_bf, *, n_prompt):
    i = pl.program_id(0)
    width = wout_ref.shape[0]
    gdim = width // GROUPS

    @pl.when(i == 0)
    def _():
        win_bf[...] = win_ref[...].astype(BF16)
        wout_bf[...] = wout_ref[...].astype(BF16)

    def mixer(x, mode):
        rows = x.shape[0]
        xn = _rmsnorm(x, gn_ref[...]).astype(BF16)
        z = jax.nn.gelu(_dot(xn, win_bf[...]))
        u = z[:, :width]
        v = _layernorm(z[:, width:], lng_ref[...], lnb_ref[...])
        vb = v.astype(BF16)
        chunks = []
        for c in range(rows // CHUNK):
            cols = [_dot(wmix_ref[mode, g], vb[c * CHUNK:(c + 1) * CHUNK, g * gdim:(g + 1) * gdim])
                    for g in range(GROUPS)]
            chunks.append(jnp.concatenate(cols, axis=1) + bias_ref[mode])
        mixed = chunks[0] if len(chunks) == 1 else jnp.concatenate(chunks, axis=0)
        gated = (u * mixed).astype(BF16)
        return x + _dot(gated, wout_bf[...]), v

    @pl.when(i < n_prompt)
    def _():
        op_ref[...] = mixer(xp_ref[...], 0)[0]

    @pl.when(i == n_prompt)
    def _():
        out, v = mixer(xs_ref[...], 1)
        os_ref[...] = out
        vs_ref[...] = v


def _gmlp(xp, xs, g_norm, w_in, ln_g, ln_b, w_s, b_s, w_out):
    tp, d = xp.shape
    ns = xs.shape[0]
    width = w_out.shape[0]
    gdim = width // GROUPS
    assert tp % TOKEN_TILE == 0 and TOKEN_TILE % CHUNK == 0 and ns == CHUNK
    n_prompt = tp // TOKEN_TILE
    causal = jnp.tril(jnp.ones((CHUNK, CHUNK), dtype=bool))
    w_prompt = jnp.where(causal[None], w_s, 0.0)
    w_sample = w_s[:, 0, 0][:, None, None] * jnp.eye(CHUNK, dtype=F32)[None]
    wmix = jnp.stack([w_prompt, w_sample]).astype(BF16)
    b_prompt = jnp.repeat(jnp.transpose(b_s), gdim, axis=1)
    b_sample = jnp.broadcast_to(jnp.repeat(b_s[:, 0], gdim)[None], (CHUNK, width))
    bias = jnp.stack([b_prompt, b_sample])

    tile = lambda i: (jnp.minimum(i, n_prompt - 1), 0)
    row = lambda a: a.reshape(1, -1)
    return pl.pallas_call(
        functools.partial(_gmlp_kernel, n_prompt=n_prompt),
        grid=(n_prompt + 1,),
        in_specs=[pl.BlockSpec((TOKEN_TILE, d), tile),
                  _resident((ns, d)), _resident((1, d)), _resident(w_in.shape),
                  _resident((1, width)), _resident((1, width)),
                  _resident(wmix.shape), _resident(bias.shape), _resident(w_out.shape)],
        out_specs=(pl.BlockSpec((TOKEN_TILE, d), tile),
                   pl.BlockSpec((ns, d), lambda i: (0, 0)),
                   pl.BlockSpec((ns, width), lambda i: (0, 0))),
        out_shape=(jax.ShapeDtypeStruct((tp, d), F32),
                   jax.ShapeDtypeStruct((ns, d), F32),
                   jax.ShapeDtypeStruct((ns, width), F32)),
        scratch_shapes=[pltpu.VMEM(w_in.shape, BF16), pltpu.VMEM(w_out.shape, BF16)],
        compiler_params=_params(1),
        name="gmlp_mixer",
    )(xp, xs, row(g_norm), w_in, row(ln_g), row(ln_b), wmix, bias, w_out)


def _softmax_rows(s, axis):
    m = jnp.max(s, axis=axis, keepdims=True)
    e = jnp.exp(s - m)
    return e / jnp.sum(e, axis=axis, keepdims=True)


def _xattn_kernel(xp_ref, xs_ref, gn_ref, wq_ref, wo_ref, k_ref, v_ref,
                  op_ref, qs_ref, wq_bf, wo_bf, *, n_prompt):
    i = pl.program_id(0)
    d = wq_ref.shape[-1]
    hd = d // HEADS
    scale = hd ** -0.5

    @pl.when(i == 0)
    def _():
        wq_bf[...] = wq_ref[0].astype(BF16)
        wo_bf[...] = wo_ref[0].astype(BF16)

    def attend(x):
        q = _dot(_rmsnorm(x, gn_ref[0]).astype(BF16), wq_bf[...])
        heads = []
        for h in range(HEADS):
            cols = slice(h * hd, (h + 1) * hd)
            qh = q[:, cols].astype(BF16)
            kh = k_ref[0, 0, :, cols].astype(BF16)
            s = lax.dot_general(qh, kh, (((1,), (1,)), ((), ())),
                                preferred_element_type=F32) * scale
            p = _softmax_rows(s, -1).astype(BF16)
            heads.append(_dot(p, v_ref[0, 0, :, cols].astype(BF16)))
        o = jnp.concatenate(heads, axis=1).astype(BF16)
        return x + _dot(o, wo_bf[...])

    @pl.when(i < n_prompt)
    def _():
        op_ref[...] = attend(xp_ref[...])

    @pl.when(i == n_prompt)
    def _():
        qs_ref[...] = _dot(_rmsnorm(xs_ref[...], gn_ref[0]).astype(BF16), wq_bf[...])


def _xattn_prompt(xp, xs, norm_xattn, w_xq, w_xo, mem_k, mem_v, layer):
    tp, d = xp.shape
    ns = xs.shape[0]
    depth, nb, n_mem, _ = mem_k.shape
    n_prompt = tp // TOKEN_TILE
    tiles_per_seq = n_prompt // nb
    assert tiles_per_seq * nb == n_prompt
    tile = lambda i: (jnp.minimum(i, n_prompt - 1), 0)
    kv = lambda i: (layer, jnp.minimum(i, n_prompt - 1) // tiles_per_seq, 0, 0)
    of_layer = lambda shape: pl.BlockSpec((1,) + shape, lambda i: (layer,) + (0,) * len(shape),
                                          pipeline_mode=pl.Buffered(1))
    return pl.pallas_call(
        functools.partial(_xattn_kernel, n_prompt=n_prompt),
        grid=(n_prompt + 1,),
        in_specs=[pl.BlockSpec((TOKEN_TILE, d), tile),
                  _resident((ns, d)), of_layer((1, d)), of_layer((d, d)), of_layer((d, d)),
                  pl.BlockSpec((1, 1, n_mem, d), kv), pl.BlockSpec((1, 1, n_mem, d), kv)],
        out_specs=(pl.BlockSpec((TOKEN_TILE, d), tile),
                   pl.BlockSpec((ns, d), lambda i: (0, 0))),
        out_shape=(jax.ShapeDtypeStruct((tp, d), F32), jax.ShapeDtypeStruct((ns, d), F32)),
        scratch_shapes=[pltpu.VMEM((d, d), BF16), pltpu.VMEM((d, d), BF16)],
        compiler_params=_params(1),
        name="xattn_prompt",
    )(xp, xs, norm_xattn.reshape(depth, 1, d), w_xq, w_xo, mem_k, mem_v)


def _xattn_sample_kernel(q_ref, k_ref, v_ref, o_ref, *, block, scale):
    i = pl.program_id(0)
    for b in range(block):
        r = i * block + b
        s = jnp.sum(k_ref[0, b] * q_ref[r][None], axis=-1, keepdims=True) * scale
        p = _softmax_rows(s, 0)
        o_ref[r] = jnp.sum(p * v_ref[0, b], axis=0)


def _xattn_sample(qs, cache_k, cache_v, layer):
    _, ns, n_mem, heads, hd = cache_k.shape
    block = SAMPLE_ATTN_BLOCK
    assert ns % block == 0
    kv = pl.BlockSpec((1, block, n_mem, heads, hd), lambda i: (layer, i, 0, 0, 0))
    return pl.pallas_call(
        functools.partial(_xattn_sample_kernel, block=block, scale=hd ** -0.5),
        grid=(ns // block,),
        in_specs=[_resident((ns, heads, hd)), kv, kv],
        out_specs=pl.BlockSpec((ns, heads, hd), lambda i: (0, 0, 0)),
        out_shape=jax.ShapeDtypeStruct((ns, heads, hd), F32),
        compiler_params=_params(1),
        name="xattn_sample",
    )(qs.reshape(ns, heads, hd), cache_k, cache_v).reshape(qs.shape)


def _attn_out_kernel(o_ref, xs_ref, wo_ref, os_ref):
    os_ref[...] = xs_ref[...] + _dot(o_ref[...].astype(BF16), wo_ref[0].astype(BF16))


def _attn_out_sample(o, xs, w_xo, layer):
    ns, d = xs.shape
    return pl.pallas_call(
        _attn_out_kernel,
        grid=(1,),
        in_specs=[_resident((ns, d)), _resident((ns, d)),
                  pl.BlockSpec((1, d, d), lambda i: (layer, 0, 0))],
        out_specs=pl.BlockSpec((ns, d), lambda i: (0, 0)),
        out_shape=jax.ShapeDtypeStruct((ns, d), F32),
        compiler_params=_params(1),
        name="attn_out_sample",
    )(o, xs, w_xo)


def _ffn_kernel(xp_ref, xs_ref, gn_ref, wg_ref, wu_ref, wd_ref, op_ref, os_ref, *, n_prompt):
    i = pl.program_id(0)
    f_dim = wg_ref.shape[1]

    def ffn(x):
        xn = _rmsnorm(x, gn_ref[...]).astype(BF16)
        acc = x
        for f in range(0, f_dim, FFN_F_CHUNK):
            cols = slice(f, f + FFN_F_CHUNK)
            hid = jax.nn.silu(_dot(xn, wg_ref[:, cols])) * _dot(xn, wu_ref[:, cols])
            acc = acc + _dot(hid.astype(BF16), wd_ref[cols, :])
        return acc

    @pl.when(i < n_prompt)
    def _():
        op_ref[...] = ffn(xp_ref[...])

    @pl.when(i == n_prompt)
    def _():
        os_ref[...] = ffn(xs_ref[...])


def _ffn(xp, xs, g_norm, w_gate, w_up, w_down):
    tp, d = xp.shape
    ns = xs.shape[0]
    f_dim = w_gate.shape[1]
    assert f_dim % FFN_F_CHUNK == 0
    n_prompt = tp // TOKEN_TILE
    wg = _cast_bf16(w_gate, d // 4)
    wu = _cast_bf16(w_up, d // 4)
    wd = _cast_bf16(w_down, f_dim // 4)
    tile = lambda i: (jnp.minimum(i, n_prompt - 1), 0)
    return pl.pallas_call(
        functools.partial(_ffn_kernel, n_prompt=n_prompt),
        grid=(n_prompt + 1,),
        in_specs=[pl.BlockSpec((TOKEN_TILE, d), tile),
                  _resident((ns, d)), _resident((1, d)),
                  _resident(wg.shape), _resident(wu.shape), _resident(wd.shape)],
        out_specs=(pl.BlockSpec((TOKEN_TILE, d), tile),
                   pl.BlockSpec((ns, d), lambda i: (0, 0))),
        out_shape=(jax.ShapeDtypeStruct((tp, d), F32), jax.ShapeDtypeStruct((ns, d), F32)),
        compiler_params=_params(1),
        name="dense_swiglu",
    )(xp, xs, g_norm.reshape(1, d), wg, wu, wd)


CONV_HALO = 32


SUBLANES = 8
CONV_UNROLL = 8
CONV_TAIL = 16


def _depthwise_conv(cbuf, y_ref, wdw_ref, bdw_ref, tm, taps):
    first = CONV_HALO - (taps - 1)
    d = y_ref.shape[1]
    max_a = (first + taps - 1) // SUBLANES
    classes = [[(a, SUBLANES * a + r - first) for a in range(max_a + 1)
                if 0 <= SUBLANES * a + r - first < taps] for r in range(SUBLANES)]
    row_i = lax.broadcasted_iota(jnp.int32, (SUBLANES, LANES), 0)

    for l in range(d // LANES):
        lanes = slice(l * LANES, (l + 1) * LANES)
        w = [jnp.broadcast_to(wdw_ref[k:k + 1, lanes], (SUBLANES, LANES)) for k in range(taps)]
        bias = jnp.broadcast_to(bdw_ref[:, lanes], (SUBLANES, LANES))

        def rotated_q(groups, j, w=w):
            out = []
            for r in range(SUBLANES):
                q = None
                for a, k in classes[r]:
                    term = w[k] * groups[j + a]
                    q = term if q is None else q + term
                out.append(q if r == 0 else pltpu.roll(q, SUBLANES - r, axis=0))
            return out

        head = {a: cbuf[SUBLANES * a:SUBLANES * (a + 1), lanes] for a in range(max_a + 1)}

        prev = rotated_q(head, 0)
        for blk in range(tm // (SUBLANES * CONV_UNROLL)):
            base = blk * SUBLANES * CONV_UNROLL
            groups = {j: cbuf[base + SUBLANES * j:base + SUBLANES * (j + 1), lanes]
                      for j in range(1, CONV_UNROLL + max_a + 1)}
            for u in range(CONV_UNROLL):
                nxt = rotated_q(groups, u + 1)
                y = prev[0] + bias
                for r in range(1, SUBLANES):
                    y = y + jnp.where(row_i < SUBLANES - r, prev[r], nxt[r])
                y_ref[base + SUBLANES * u:base + SUBLANES * (u + 1), lanes] = y
                prev = nxt


def _conv_prompt_kernel(xp_ref, gn_ref, w1_ref, wdw_ref, bdw_ref, lng_ref, lnb_ref, w2_ref,
                        op_ref, st_ref, w1_bf, w2_bf, cbuf, ybuf, *, tiles_per_seq):
    i = pl.program_id(0)
    d = w2_ref.shape[0]
    taps = wdw_ref.shape[0]
    tm = xp_ref.shape[0]

    @pl.when(i == 0)
    def _():
        w1_bf[...] = w1_ref[...].astype(BF16)
        w2_bf[...] = w2_ref[...].astype(BF16)
        cbuf[CONV_HALO + tm:CONV_HALO + tm + CONV_TAIL, :] = jnp.zeros((CONV_TAIL, d), F32)

    @pl.when(i % tiles_per_seq == 0)
    def _():
        cbuf[0:CONV_HALO, :] = jnp.zeros((CONV_HALO, d), F32)

    x = xp_ref[...]
    ag = _dot(_rmsnorm(x, gn_ref[...]).astype(BF16), w1_bf[...])
    c = ag[:, :d] * jax.nn.sigmoid(ag[:, d:])
    cbuf[CONV_HALO:CONV_HALO + tm, :] = c
    first = CONV_HALO - (taps - 1)
    _depthwise_conv(cbuf, ybuf, wdw_ref, bdw_ref, tm, taps)
    t = jax.nn.silu(_layernorm(ybuf[...], lng_ref[...], lnb_ref[...])).astype(BF16)
    op_ref[...] = x + _dot(t, w2_bf[...])
    cbuf[0:CONV_HALO, :] = cbuf[tm:tm + CONV_HALO, :]

    @pl.when(i % tiles_per_seq == tiles_per_seq - 1)
    def _():
        st_ref[0] = cbuf[first:CONV_HALO, :]


def _conv_prompt(xp, n_seq, g_norm, w_pw1, w_dw, b_dw, ln_g, ln_b, w_pw2):
    tp, d = xp.shape
    taps = w_dw.shape[0]
    n_prompt = tp // TOKEN_TILE
    tiles_per_seq = n_prompt // n_seq
    assert tiles_per_seq * n_seq == n_prompt and taps - 1 <= CONV_HALO <= TOKEN_TILE
    assert TOKEN_TILE % (SUBLANES * CONV_UNROLL) == 0 and CONV_TAIL >= SUBLANES * 2
    row = lambda a: a.reshape(1, -1)
    return pl.pallas_call(
        functools.partial(_conv_prompt_kernel, tiles_per_seq=tiles_per_seq),
        grid=(n_prompt,),
        in_specs=[pl.BlockSpec((TOKEN_TILE, d), lambda i: (i, 0)),
                  _resident((1, d)), _resident(w_pw1.shape), _resident(w_dw.shape),
                  _resident((1, d)), _resident((1, d)), _resident((1, d)), _resident(w_pw2.shape)],
        out_specs=(pl.BlockSpec((TOKEN_TILE, d), lambda i: (i, 0)),
                   pl.BlockSpec((1, taps - 1, d), lambda i: (i // tiles_per_seq, 0, 0))),
        out_shape=(jax.ShapeDtypeStruct((tp, d), F32),
                   jax.ShapeDtypeStruct((n_seq, taps - 1, d), F32)),
        scratch_shapes=[pltpu.VMEM(w_pw1.shape, BF16), pltpu.VMEM(w_pw2.shape, BF16),
                        pltpu.VMEM((CONV_HALO + TOKEN_TILE + CONV_TAIL, d), F32),
                        pltpu.VMEM((TOKEN_TILE, d), F32)],
        compiler_params=_params(1),
        name="conv_prompt",
    )(xp, row(g_norm), w_pw1, w_dw, row(b_dw), row(ln_g), row(ln_b), w_pw2)


def _conv_sample_kernel(xs_ref, gn_ref, w1_ref, wdw_ref, bdw_ref, lng_ref, lnb_ref, w2_ref, st_ref,
                        os_ref, sto_ref, c_all, t_all, *, block):
    i = pl.program_id(0)
    d = w2_ref.shape[0]
    taps = wdw_ref.shape[0]

    @pl.when(i == 0)
    def _():
        ag = _dot(_rmsnorm(xs_ref[...], gn_ref[...]).astype(BF16), w1_ref[...].astype(BF16))
        c_all[...] = ag[:, :d] * jax.nn.sigmoid(ag[:, d:])

    w_hist = wdw_ref[0:taps - 1, :]
    w_last = wdw_ref[taps - 1:taps, :]
    for b in range(block):
        r = i * block + b
        hist = st_ref[0, b]
        cb = c_all[pl.ds(r, 1), :]
        y = jnp.sum(hist * w_hist, axis=0, keepdims=True) + cb * w_last + bdw_ref[...]
        t_all[pl.ds(r, 1), :] = jax.nn.silu(_layernorm(y, lng_ref[...], lnb_ref[...]))
        sto_ref[0, b, 0:taps - 2, :] = hist[1:taps - 1, :]
        sto_ref[0, b, taps - 2:taps - 1, :] = cb

    @pl.when(i == pl.num_programs(0) - 1)
    def _():
        os_ref[...] = xs_ref[...] + _dot(t_all[...].astype(BF16), w2_ref[...].astype(BF16))


def _conv_sample(xs, state, g_norm, w_pw1, w_dw, b_dw, ln_g, ln_b, w_pw2):
    ns, d = xs.shape
    taps = w_dw.shape[0]
    block = SAMPLE_CONV_BLOCK
    assert ns % block == 0
    row = lambda a: a.reshape(1, -1)
    st = pl.BlockSpec((1, block, taps - 1, d), lambda i: (0, i, 0, 0))
    return pl.pallas_call(
        functools.partial(_conv_sample_kernel, block=block),
        grid=(ns // block,),
        in_specs=[_resident((ns, d)), _resident((1, d)), _resident(w_pw1.shape),
                  _resident(w_dw.shape), _resident((1, d)), _resident((1, d)), _resident((1, d)),
                  _resident(w_pw2.shape), st],
        out_specs=(pl.BlockSpec((ns, d), lambda i: (0, 0)), st),
        out_shape=(jax.ShapeDtypeStruct((ns, d), F32), jax.ShapeDtypeStruct(state.shape, F32)),
        scratch_shapes=[pltpu.VMEM((ns, d), F32), pltpu.VMEM((ns, d), F32)],
        compiler_params=_params(1),
        name="conv_sample",
    )(xs, row(g_norm), w_pw1, w_dw, row(b_dw), row(ln_g), row(ln_b), w_pw2, state)


RANK_BITS = 24
RANK_SPAN = 1 << RANK_BITS
NO_RANK = -float(1 << 20)


def _router_kernel(xp_ref, xs_ref, gn_ref, wr_ref, hn_ref, c1_ref, c2_ref, meta_ref, seen_ref, count_ref, seen,
                   *, n_prompt, n_experts):
    i = pl.program_id(0)

    def route(x):
        xn = _rmsnorm(x, gn_ref[...])
        xh = xn.astype(BF16)
        xl = (xn - xh.astype(F32)).astype(BF16)
        logits = _dot(jnp.concatenate([xh, xl, xh], axis=1), wr_ref[...])
        lane = lax.broadcasted_iota(jnp.int32, logits.shape, 1).astype(F32)
        neg = jnp.float32(-jnp.inf)
        logits = jnp.where(lane < n_experts, logits, neg)
        v1 = jnp.max(logits, axis=-1, keepdims=True)
        i1 = jnp.min(jnp.where(logits == v1, lane, float(LANES)), axis=-1, keepdims=True)
        rest = jnp.where(lane == i1, neg, logits)
        v2 = jnp.max(rest, axis=-1, keepdims=True)
        i2 = jnp.min(jnp.where(rest == v2, lane, float(LANES)), axis=-1, keepdims=True)
        e2 = jnp.exp(v2 - v1)
        denom = 1.0 + e2

        rows = x.shape[0]
        pick1 = jnp.where(lane == i1, 1.0, 0.0)
        pick2 = jnp.where(lane == i2, 1.0, 0.0)
        picks = pick1 + pick2
        r_i = lax.broadcasted_iota(jnp.int32, (rows, rows), 0)
        c_i = lax.broadcasted_iota(jnp.int32, (rows, rows), 1)
        earlier = jnp.where(c_i < r_i, 1.0, 0.0).astype(BF16)
        before = _dot(earlier, picks.astype(BF16)) + seen[0:1, :]
        rank1 = jnp.sum(pick1 * before, axis=-1, keepdims=True)
        rank2 = jnp.sum(pick2 * before, axis=-1, keepdims=True)
        total = seen[0:1, :] + jnp.sum(picks, axis=0, keepdims=True)

        meta = jnp.where(lane == 0.0, i1, jnp.where(lane == 1.0, i2, jnp.where(lane == 2.0, rank1,
               jnp.where(lane == 3.0, rank2, jnp.where(lane == 4.0, 1.0 / denom,
               jnp.where(lane == 5.0, e2 / denom, 0.0))))))
        wide = lambda a: jnp.broadcast_to(a, (rows, LANES)).astype(jnp.int32)
        code = lambda e, r: wide(e) * RANK_SPAN + wide(r)
        starts = [seen[0:1, :]]
        for w in range(1, TOKEN_TILE // DISPATCH_WIN):
            starts.append(before[w * DISPATCH_WIN:w * DISPATCH_WIN + 1, :] if w * DISPATCH_WIN < rows else total)
        pad = jnp.zeros((SUBLANES - len(starts), LANES), F32)
        seen_ref[0] = jnp.concatenate(starts + [pad], axis=0).astype(jnp.int32)
        seen[0:1, :] = total
        return xn.astype(BF16), code(i1, rank1), code(i2, rank2), meta

    def no_token(rows):
        lane_t = lax.broadcasted_iota(jnp.int32, (rows, LANES), 1)
        return jnp.where(lane_t == 2, NO_RANK, jnp.where(lane_t == 3, NO_RANK, 0.0))

    def put(hn, code1, code2, meta):
        hn_ref[...] = hn
        c1_ref[...] = code1
        c2_ref[...] = code2
        by_lane = jnp.transpose(meta)[0:SUBLANES, :]
        for w in range(TOKEN_TILE // DISPATCH_WIN):
            meta_ref[w] = by_lane[:, w * DISPATCH_WIN:(w + 1) * DISPATCH_WIN]

    @pl.when(i == 0)
    def _():
        seen[...] = jnp.zeros(seen.shape, F32)

    @pl.when(i < n_prompt)
    def _():
        put(*route(xp_ref[...]))

    @pl.when(i == n_prompt)
    def _():
        ns = xs_ref.shape[0]
        hn, code1, code2, meta = route(xs_ref[...])
        fill = lambda a, v: jnp.concatenate([a, jnp.full((TOKEN_TILE - ns, a.shape[1]), v, a.dtype)], axis=0)
        put(fill(hn, 0.0), fill(code1, -1), fill(code2, -1),
            jnp.concatenate([meta, no_token(TOKEN_TILE - ns)], axis=0))
        count_ref[...] = jnp.broadcast_to(seen[0:1, :], count_ref.shape).astype(jnp.int32)

    @pl.when(i > n_prompt)
    def _():
        none = jnp.full((TOKEN_TILE, LANES), -1, jnp.int32)
        put(jnp.zeros(hn_ref.shape, BF16), none, none, no_token(TOKEN_TILE))
        seen_ref[0] = jnp.broadcast_to(seen[0:1, :], (SUBLANES, LANES)).astype(jnp.int32)


def _router(xp, xs, g_norm, w_router):
    tp, d = xp.shape
    ns = xs.shape[0]
    n_experts = w_router.shape[1]
    n_prompt = tp // TOKEN_TILE
    n_tiles = n_prompt + 1 + -(-(DISPATCH_SPAN - 1) * DISPATCH_WIN // TOKEN_TILE)
    t_pad = n_tiles * TOKEN_TILE
    assert TOKEN_TILE % DISPATCH_WIN == 0 and TOKEN_TILE // DISPATCH_WIN <= SUBLANES
    assert t_pad * TOP_K < RANK_SPAN and n_experts * RANK_SPAN < 2 ** 31
    wins = TOKEN_TILE // DISPATCH_WIN
    wr = jnp.pad(w_router, ((0, 0), (0, LANES - n_experts)))
    wr_hi = wr.astype(BF16)
    wr_lo = (wr - wr_hi.astype(F32)).astype(BF16)
    wr = jnp.concatenate([wr_hi, wr_hi, wr_lo], axis=0)
    tile = lambda i: (jnp.minimum(i, n_prompt - 1), 0)
    lanes_spec = pl.BlockSpec((TOKEN_TILE, LANES), lambda i: (i, 0))
    return pl.pallas_call(
        functools.partial(_router_kernel, n_prompt=n_prompt, n_experts=n_experts),
        grid=(n_tiles,),
        in_specs=[pl.BlockSpec((TOKEN_TILE, d), tile),
                  _resident((ns, d)), _resident((1, d)), _resident(wr.shape)],
        out_specs=(pl.BlockSpec((TOKEN_TILE, d), lambda i: (i, 0)), lanes_spec, lanes_spec,
                   pl.BlockSpec((wins, SUBLANES, DISPATCH_WIN), lambda i: (i, 0, 0)),
                   pl.BlockSpec((1, SUBLANES, LANES), lambda i: (i, 0, 0)),
                   pl.BlockSpec((SUBLANES, LANES), lambda i: (0, 0))),
        out_shape=(jax.ShapeDtypeStruct((t_pad, d), BF16),
                   jax.ShapeDtypeStruct((t_pad, LANES), jnp.int32),
                   jax.ShapeDtypeStruct((t_pad, LANES), jnp.int32),
                   jax.ShapeDtypeStruct((n_tiles * wins, SUBLANES, DISPATCH_WIN), F32),
                   jax.ShapeDtypeStruct((n_tiles, SUBLANES, LANES), jnp.int32),
                   jax.ShapeDtypeStruct((SUBLANES, LANES), jnp.int32)),
        scratch_shapes=[pltpu.VMEM((SUBLANES, LANES), F32)],
        compiler_params=_params(1),
        name="moe_router",
    )(xp, xs, g_norm.reshape(1, d), wr)


def _lane_tile(a, width):
    return jnp.concatenate([a] * (width // LANES), axis=1)


def _dispatch_kernel(slot_nsub, win_lo, win_hi, first_row, meta_ref, hn_ref, x_ref, g_ref, *, n_experts):
    s = pl.program_id(0)
    n_sub = slot_nsub[s]
    subs_per_slot = SLOT_ROWS // SUB_ROWS
    d = hn_ref.shape[1]
    span = DISPATCH_SPAN * DISPATCH_WIN
    row_in_block = lax.broadcasted_iota(jnp.int32, (SUB_ROWS, span), 0)

    def sub_block(j, carry):
        rows = pl.ds(pl.multiple_of(j * SUB_ROWS, SUB_ROWS), SUB_ROWS)
        row_id = (row_in_block + (s * SLOT_ROWS + j * SUB_ROWS)).astype(F32)
        k = s * subs_per_slot + j
        x_ref[rows, :] = jnp.zeros((SUB_ROWS, d), BF16)
        g_ref[rows, :] = jnp.zeros((SUB_ROWS, LANES), F32)

        def windows(c, carry):
            w = win_lo[k] + c * DISPATCH_SPAN
            along = lambda row: jnp.concatenate(
                [meta_ref[w + n, row:row + 1, :] for n in range(DISPATCH_SPAN)], axis=1)

            def owned_row(expert, rank):
                first = jnp.zeros_like(expert)
                for e in range(n_experts):
                    first = jnp.where(expert == float(e), first_row[e].astype(F32), first)
                return first + rank

            hit1 = owned_row(along(0), along(2)) == row_id
            hit2 = owned_row(along(1), along(3)) == row_id
            sel = jnp.where(hit1, 1.0, jnp.where(hit2, 1.0, 0.0)).astype(BF16)
            gates = jnp.where(hit1, along(4), jnp.where(hit2, along(5), 0.0))
            base = pl.multiple_of(w * DISPATCH_WIN, DISPATCH_WIN)
            x_ref[rows, :] += _dot(sel, hn_ref[pl.ds(base, span), :]).astype(BF16)
            g_ref[rows, :] += jnp.broadcast_to(jnp.sum(gates, axis=-1, keepdims=True), (SUB_ROWS, LANES))
            return carry

        n_win = win_hi[k] + 1 - win_lo[k]
        lax.fori_loop(0, (n_win + DISPATCH_SPAN - 1) // DISPATCH_SPAN, windows, 0)
        return carry

    def zero_block(j, carry):
        rows = pl.ds(pl.multiple_of(j * SUB_ROWS, SUB_ROWS), SUB_ROWS)
        x_ref[rows, :] = jnp.zeros((SUB_ROWS, d), BF16)
        g_ref[rows, :] = jnp.zeros((SUB_ROWS, LANES), F32)
        return carry

    lax.fori_loop(0, n_sub, sub_block, 0)
    lax.fori_loop(n_sub, subs_per_slot, zero_block, 0)


def _dispatch(hn, meta, slot_nsub, win_lo, win_hi, first_row):
    t_pad, d = hn.shape
    n_slots = slot_nsub.shape[0]
    assert t_pad % DISPATCH_WIN == 0
    grid_spec = pltpu.PrefetchScalarGridSpec(
        num_scalar_prefetch=4,
        grid=(n_slots,),
        in_specs=[pl.BlockSpec(meta.shape, lambda s, *_: (0, 0, 0)),
                  pl.BlockSpec((t_pad, d), lambda s, *_: (0, 0), pipeline_mode=pl.Buffered(1))],
        out_specs=(pl.BlockSpec((SLOT_ROWS, d), lambda s, *_: (s, 0)),
                   pl.BlockSpec((SLOT_ROWS, LANES), lambda s, *_: (s, 0))),
    )
    return pl.pallas_call(
        functools.partial(_dispatch_kernel, n_experts=first_row.shape[0]),
        grid_spec=grid_spec,
        out_shape=(jax.ShapeDtypeStruct((n_slots * SLOT_ROWS, d), BF16),
                   jax.ShapeDtypeStruct((n_slots * SLOT_ROWS, LANES), F32)),
        compiler_params=_params(1),
        name="moe_dispatch",
    )(slot_nsub, win_lo, win_hi, first_row, meta, hn)


def _expert_kernel(slot_expert, slot_nsub, x_ref, gate_ref, wg_ref, wu_ref, wd_ref, y_ref,
                   wg_bf, wu_bf, wd_bf, acc):
    s = pl.program_id(0)
    f = pl.program_id(1)
    n_sub = slot_nsub[s]
    last_f = f == pl.num_programs(1) - 1
    d = y_ref.shape[1]

    @pl.when(n_sub > 0)
    def _():
        wg_bf[...] = wg_ref[0].astype(BF16)
        wu_bf[...] = wu_ref[0].astype(BF16)
        wd_bf[...] = wd_ref[0].astype(BF16)

    def zero_acc(j, carry):
        rows = pl.ds(pl.multiple_of(j * SUB_ROWS, SUB_ROWS), SUB_ROWS)
        acc[rows, :] = jnp.zeros((SUB_ROWS, d), F32)
        return carry

    @pl.when(f == 0)
    def _():
        lax.fori_loop(0, n_sub, zero_acc, 0)

    def block(start, size):
        rows = pl.ds(pl.multiple_of(start, SUB_ROWS), size)
        xg = x_ref[rows, :]
        hid = jax.nn.silu(_dot(xg, wg_bf[...])) * _dot(xg, wu_bf[...])
        acc[rows, :] += _dot(hid.astype(BF16), wd_bf[...])

        @pl.when(last_f)
        def _():
            y_ref[rows, :] = (acc[rows, :] * _lane_tile(gate_ref[rows, :], d)).astype(y_ref.dtype)

    def quad(j, carry):
        block(j * (4 * SUB_ROWS), 4 * SUB_ROWS)
        return carry

    lax.fori_loop(0, n_sub // 4, quad, 0)

    @pl.when(n_sub % 4 >= 2)
    def _():
        block((n_sub // 4) * (4 * SUB_ROWS), 2 * SUB_ROWS)

    @pl.when(n_sub % 2 == 1)
    def _():
        block((n_sub - 1) * SUB_ROWS, SUB_ROWS)

    def zero_block(j, carry):
        rows = pl.ds(pl.multiple_of(j * SUB_ROWS, SUB_ROWS), SUB_ROWS)
        y_ref[rows, :] = jnp.zeros((SUB_ROWS, d), y_ref.dtype)
        return carry

    @pl.when(last_f)
    def _():
        lax.fori_loop(n_sub, SLOT_ROWS // SUB_ROWS, zero_block, 0)


def _expert_ffn(x_slots, gate_rep, slot_expert, slot_nsub, w_gate, w_up, w_down):
    n_slots = slot_expert.shape[0]
    d = x_slots.shape[1]
    f_dim = w_gate.shape[2]
    tf = EXPERT_F_TILE
    assert f_dim % tf == 0 and SLOT_ROWS % SUB_ROWS == 0
    n_f = f_dim // tf
    f_tile = lambda s, f, sn: jnp.where(sn[s] > 0, f, n_f - 1)
    grid_spec = pltpu.PrefetchScalarGridSpec(
        num_scalar_prefetch=2,
        grid=(n_slots, f_dim // tf),
        in_specs=[pl.BlockSpec((SLOT_ROWS, d), lambda s, f, se, sn: (s, 0)),
                  pl.BlockSpec((SLOT_ROWS, LANES), lambda s, f, se, sn: (s, 0)),
                  pl.BlockSpec((1, d, tf), lambda s, f, se, sn: (se[s], 0, f_tile(s, f, sn))),
                  pl.BlockSpec((1, d, tf), lambda s, f, se, sn: (se[s], 0, f_tile(s, f, sn))),
                  pl.BlockSpec((1, tf, d), lambda s, f, se, sn: (se[s], f_tile(s, f, sn), 0))],
        out_specs=pl.BlockSpec((SLOT_ROWS, d), lambda s, f, se, sn: (s, 0)),
        scratch_shapes=[pltpu.VMEM((d, tf), BF16), pltpu.VMEM((d, tf), BF16),
                        pltpu.VMEM((tf, d), BF16), pltpu.VMEM((SLOT_ROWS, d), F32)],
    )
    return pl.pallas_call(
        _expert_kernel,
        grid_spec=grid_spec,
        out_shape=jax.ShapeDtypeStruct((n_slots * SLOT_ROWS, d), BF16),
        compiler_params=_params(2),
        name="expert_swiglu",
    )(slot_expert, slot_nsub, x_slots, gate_rep, w_gate, w_up, w_down)


def _combine_kernel(n_blocks, block_ids, first_row, xp_ref, xs_ref, c1_ref, c2_ref, gf_ref, *rest,
                    n_prompt, max_blocks, n_experts):
    y_refs = rest[:max_blocks]
    op_ref, os_ref = rest[max_blocks:]
    i = pl.program_id(0)

    def owned_row(code):
        expert = code >> RANK_BITS
        first = jnp.zeros_like(code)
        for e in range(n_experts):
            first = jnp.where(expert == e, first_row[e], first)
        return jnp.where(code < 0, -1, first + (code & (RANK_SPAN - 1)))

    def combine(x, o_ref):
        rows = x.shape[0]
        p1 = _lane_tile(owned_row(c1_ref[0:rows, :]), COMBINE_BLOCK)
        p2 = _lane_tile(owned_row(c2_ref[0:rows, :]), COMBINE_BLOCK)
        lane = lax.broadcasted_iota(jnp.int32, (rows, COMBINE_BLOCK), 1)

        def picked(blocks):
            total = None
            for b in blocks:
                first_row = jnp.where(b < n_blocks[i], block_ids[i * max_blocks + b] * COMBINE_BLOCK,
                                      -2 * COMBINE_BLOCK)
                row_id = lane + first_row
                sel = jnp.where(p1 == row_id, 1.0, jnp.where(p2 == row_id, 1.0, 0.0)).astype(BF16)
                part = _dot(sel, y_refs[b][...])
                total = part if total is None else total + part
            return total

        o_ref[...] = x + picked(range(0, min(COMBINE_ALWAYS, max_blocks)))
        for g in range(COMBINE_ALWAYS, max_blocks, COMBINE_GROUP):
            @pl.when(g < n_blocks[i])
            def _():
                o_ref[...] += picked(range(g, min(g + COMBINE_GROUP, max_blocks)))
        o_ref[...] = _rmsnorm(o_ref[...], gf_ref[...])

    @pl.when(i < n_prompt)
    def _():
        combine(xp_ref[...], op_ref)

    @pl.when(i == n_prompt)
    def _():
        combine(xs_ref[...], os_ref)


def _combine(xp, xs, codes, y, n_blocks, block_ids, first_row, g_final, max_blocks):
    tp, d = xp.shape
    ns = xs.shape[0]
    n_prompt = tp // TOKEN_TILE
    tile = lambda i, *_: (jnp.minimum(i, n_prompt - 1), 0)
    pos_spec = pl.BlockSpec((TOKEN_TILE, LANES), lambda i, *_: (i, 0))

    def y_spec(b):
        return pl.BlockSpec((COMBINE_BLOCK, d), lambda i, nb, ids, fr: (ids[i * max_blocks + b], 0))

    grid_spec = pltpu.PrefetchScalarGridSpec(
        num_scalar_prefetch=3,
        grid=(n_prompt + 1,),
        in_specs=[pl.BlockSpec((TOKEN_TILE, d), tile),
                  pl.BlockSpec((ns, d), lambda i, *_: (0, 0)),
                  pos_spec, pos_spec,
                  pl.BlockSpec((1, d), lambda i, *_: (0, 0))] + [y_spec(b) for b in range(max_blocks)],
        out_specs=(pl.BlockSpec((TOKEN_TILE, d), tile),
                   pl.BlockSpec((ns, d), lambda i, *_: (0, 0))),
    )
    return pl.pallas_call(
        functools.partial(_combine_kernel, n_prompt=n_prompt, max_blocks=max_blocks,
                          n_experts=first_row.shape[0]),
        grid_spec=grid_spec,
        out_shape=(jax.ShapeDtypeStruct((tp, d), F32), jax.ShapeDtypeStruct((ns, d), F32)),
        compiler_params=_params(1),
        name="moe_combine",
    )(n_blocks, block_ids, first_row, xp, xs, codes[0], codes[1], g_final.reshape(1, d),
      *([y] * max_blocks))


def _routing_tables(seen, counts, t_valid, n_experts):
    i32 = jnp.int32
    n_tiles = seen.shape[0]
    wins_per_tile = TOKEN_TILE // DISPATCH_WIN
    n_win = n_tiles * wins_per_tile
    subs_per_slot = SLOT_ROWS // SUB_ROWS
    n_slots = t_valid * TOP_K // SLOT_ROWS + n_experts
    experts = jnp.arange(n_experts, dtype=i32)

    cnt = counts[0, :n_experts]
    n_chunks = (cnt + SLOT_ROWS - 1) // SLOT_ROWS
    chunk_end = jnp.cumsum(n_chunks)
    slot_base = chunk_end - n_chunks
    n_used = chunk_end[-1]
    first_row = slot_base * SLOT_ROWS

    sid = jnp.arange(n_slots, dtype=i32)
    expert_of = lambda s: jnp.sum((s[..., None] >= chunk_end).astype(i32), axis=-1)
    slot_expert = jnp.where(sid < n_used, expert_of(sid), expert_of(n_used - 1)).astype(i32)
    of_slot = lambda tab: jnp.sum(jnp.where(slot_expert[:, None] == experts, tab, 0), axis=-1)
    slot_rank = (sid - of_slot(slot_base)) * SLOT_ROWS
    slot_rows = jnp.where(sid < n_used, jnp.clip(of_slot(cnt) - slot_rank, 0, SLOT_ROWS), 0)
    slot_nsub = ((slot_rows + SUB_ROWS - 1) // SUB_ROWS).astype(i32)

    win_seen = seen[:, :wins_per_tile, :n_experts].reshape(n_win, n_experts)
    seen_slot = jnp.sum(jnp.where(slot_expert[:, None, None] == experts, win_seen[None], 0), axis=-1)
    j = jnp.arange(subs_per_slot, dtype=i32)[None, :]
    rank_lo = slot_rank[:, None] + j * SUB_ROWS
    rank_hi = jnp.minimum(rank_lo + SUB_ROWS, of_slot(cnt)[:, None]) - 1
    window_of = lambda r: jnp.sum((seen_slot[:, None, :] <= r[:, :, None]).astype(i32), axis=-1) - 1
    active = j < slot_nsub[:, None]
    win_lo = jnp.where(active, window_of(rank_lo), 1).reshape(-1)
    win_hi = jnp.where(active, window_of(rank_hi), 0).reshape(-1)

    tile_lo = seen[:, 0, :n_experts]
    tile_hi = jnp.concatenate([tile_lo[1:], cnt[None, :]], axis=0)
    blk_lo = tile_lo // COMBINE_BLOCK
    per_expert = jnp.where(tile_hi > tile_lo, (tile_hi - 1) // COMBINE_BLOCK - blk_lo + 1, 0)
    ends = jnp.cumsum(per_expert, axis=1)
    n_blocks = ends[:, -1]
    max_blocks = TOKEN_TILE * TOP_K // COMBINE_BLOCK + 2 * n_experts
    b = jnp.arange(max_blocks, dtype=i32)[None, :]
    e_of_b = jnp.minimum(jnp.sum((ends[:, None, :] <= b[:, :, None]).astype(i32), axis=-1), n_experts - 1)
    take = lambda tab: jnp.sum(jnp.where(e_of_b[:, :, None] == experts, tab[:, None, :], 0), axis=-1)
    rank_block = take(blk_lo) + b - (take(ends) - take(per_expert))
    ids = jnp.sum(jnp.where(e_of_b[:, :, None] == experts, first_row // COMBINE_BLOCK, 0), axis=-1) + rank_block
    valid = b < n_blocks[:, None]
    id_bits = 1 << 16
    assert n_slots * subs_per_slot < id_bits
    keyed = jnp.where(valid, jnp.arange(n_tiles, dtype=i32)[:, None] * id_bits + ids, 0)
    block_ids = (lax.cummax(keyed, axis=0) % id_bits).reshape(-1)

    return dict(slot_expert=slot_expert, slot_nsub=slot_nsub, first_row=first_row.astype(i32),
                win_lo=win_lo.astype(i32), win_hi=win_hi.astype(i32),
                n_blocks=n_blocks.astype(i32), block_ids=block_ids.astype(i32), max_blocks=max_blocks)


def _moe(xp, xs, g_norm, w_router, w_gate, w_up, w_down, g_final):
    tp, d = xp.shape
    ns = xs.shape[0]
    n_experts = w_router.shape[1]
    assert ns <= TOKEN_TILE and SLOT_ROWS % COMBINE_BLOCK == 0
    hn, code1, code2, meta, seen, counts = _router(xp, xs, g_norm, w_router)
    t = _routing_tables(seen, counts, tp + ns, n_experts)
    x_slots, gate_rep = _dispatch(hn, meta, t["slot_nsub"], t["win_lo"], t["win_hi"], t["first_row"])
    y = _expert_ffn(x_slots, gate_rep, t["slot_expert"], t["slot_nsub"], w_gate, w_up, w_down)
    return _combine(xp, xs, (code1, code2), y, t["n_blocks"], t["block_ids"], t["first_row"], g_final,
                    t["max_blocks"])


def kernel(x_prompt, x_sample, mem_prompt, cache_mem_k, cache_mem_v, state_conv, norm_mix, norm_xattn, norm_ffn, norm_mem, norm_final, w_xq, w_xk, w_xv, w_xo, a_w_in, a_ln_g, a_ln_b, a_w_s, a_b_s, a_w_out, b_w_pw1, b_w_dw, b_b_dw, b_ln_g, b_ln_b, b_w_pw2, ffn_w_gate, ffn_w_up, ffn_w_down, moe_w_router, moe_w_gate, moe_w_up, moe_w_down):
    nb, seq, d = x_prompt.shape
    ns = x_sample.shape[0]
    depth = norm_mix.shape[0]
    n_mem = mem_prompt.shape[1]
    assert depth == 2 and x_sample.shape[1] == 1

    mem_k, mem_v, mem_k_heads, mem_v_heads = _memory_kv(mem_prompt, norm_mem, w_xk, w_xv)

    hp = x_prompt.reshape(nb * seq, d)
    hs = x_sample.reshape(ns, d)

    def cross_attention(hp, hs, layer):
        hp_new, qs = _xattn_prompt(hp, hs, norm_xattn, w_xq, w_xo, mem_k, mem_v, layer)
        o = _xattn_sample(qs, cache_mem_k, cache_mem_v, layer)
        hs_new = _attn_out_sample(o, hs, w_xo, layer)
        return hp_new, hs_new

    hp, hs, v_sample = _gmlp(hp, hs, norm_mix[0], a_w_in[0], a_ln_g[0], a_ln_b[0],
                             a_w_s[0], a_b_s[0], a_w_out[0])
    hp, hs = cross_attention(hp, hs, 0)
    hp, hs = _ffn(hp, hs, norm_ffn[0], ffn_w_gate[0], ffn_w_up[0], ffn_w_down[0])

    conv_w = (norm_mix[1], b_w_pw1[0], b_w_dw[0], b_b_dw[0], b_ln_g[0], b_ln_b[0], b_w_pw2[0])
    hp, conv_state_prompt = _conv_prompt(hp, nb, *conv_w)
    hs, conv_state_sample = _conv_sample(hs, state_conv, *conv_w)
    hp, hs = cross_attention(hp, hs, 1)
    yp, ys = _moe(hp, hs, norm_ffn[1], moe_w_router[0], moe_w_gate[0], moe_w_up[0], moe_w_down[0],
                  norm_final)

    return (yp.reshape(nb, seq, d),
            ys.reshape(ns, 1, d),
            mem_k_heads,
            mem_v_heads,
            conv_state_prompt[None],
            conv_state_sample,
            v_sample.reshape(1, ns, 1, -1))
```

```python
import functools

import jax
import jax.numpy as jnp
from jax import lax
from jax.experimental import pallas as pl
from jax.experimental.pallas import tpu as pltpu

F32 = jnp.float32
BF16 = jnp.bfloat16

RMS_EPS = 1e-6
LN_EPS = 1e-5
CHUNK = 128
GROUPS = 8
HEADS = 4
TOP_K = 2
LANES = 128
V7X_VMEM_LIMIT = 56 * 1024 * 1024

TOKEN_TILE = 512
TILE_PARTS = 2
SAMPLE_ATTN_BLOCK = 8
MEMKV_SEQS = 2
SAMPLE_CONV_BLOCK = 16
SLOT_ROWS = 2304
SUB_ROWS = 256
EXPERT_F_TILE = 512
DISPATCH_WIN = 256
DISPATCH_SPAN = 5
COMBINE_BLOCK = 256
COMBINE_ALWAYS = 10
COMBINE_GROUP = 2
FFN_F_CHUNK = 256


def _params(n_axes, vmem=V7X_VMEM_LIMIT):
    return pltpu.CompilerParams(dimension_semantics=("arbitrary",) * n_axes,
                                vmem_limit_bytes=vmem)


def _resident(shape):
    nd = len(shape)
    return pl.BlockSpec(shape, lambda *_: (0,) * nd, pipeline_mode=pl.Buffered(1))


def _rmsnorm(x, g):
    return x * lax.rsqrt(jnp.mean(x * x, axis=-1, keepdims=True) + RMS_EPS) * g


def _layernorm(x, g, b):
    xc = x - jnp.mean(x, axis=-1, keepdims=True)
    var = jnp.mean(xc * xc, axis=-1, keepdims=True)
    return xc * lax.rsqrt(var + LN_EPS) * g + b


def _dot(a, b):
    return jnp.dot(a, b, preferred_element_type=F32)


def _row_ranges(n_rows, multiple):
    part = n_rows // TILE_PARTS
    assert part % multiple == 0
    return [slice(p * part, (p + 1) * part) for p in range(TILE_PARTS)]


def _cast_kernel(x_ref, o_ref):
    o_ref[...] = x_ref[...].astype(o_ref.dtype)


def _cast_bf16(w, rows_per_step):
    r, c = w.shape
    return pl.pallas_call(
        _cast_kernel,
        grid=(r // rows_per_step,),
        in_specs=[pl.BlockSpec((rows_per_step, c), lambda i: (i, 0))],
        out_specs=pl.BlockSpec((rows_per_step, c), lambda i: (i, 0)),
        out_shape=jax.ShapeDtypeStruct((r, c), BF16),
        compiler_params=_params(1),
        name="cast_bf16",
    )(w)


def _memkv_kernel(mem_ref, g_ref, wk_ref, wv_ref, k_ref, v_ref, kh_ref, vh_ref, wk_bf, wv_bf):
    @pl.when(pl.program_id(1) == 0)
    def _():
        wk_bf[...] = wk_ref[0].astype(BF16)
        wv_bf[...] = wv_ref[0].astype(BF16)

    hd = kh_ref.shape[-1]
    nb_step, n_mem, d = mem_ref.shape
    mn = _rmsnorm(mem_ref[...].reshape(nb_step * n_mem, d), g_ref[0]).astype(BF16)
    k = _dot(mn, wk_bf[...])
    v = _dot(mn, wv_bf[...])
    for b in range(nb_step):
        rows = slice(b * n_mem, (b + 1) * n_mem)
        k_ref[0, b] = k[rows]
        v_ref[0, b] = v[rows]
        for h in range(HEADS):
            kh_ref[0, b, :, h, :] = k[rows, h * hd:(h + 1) * hd]
            vh_ref[0, b, :, h, :] = v[rows, h * hd:(h + 1) * hd]


def _memory_kv(mem, norm_mem, w_xk, w_xv):
    depth, d, _ = w_xk.shape
    nb, n_mem, _ = mem.shape
    hd = d // HEADS
    flat = jax.ShapeDtypeStruct((depth, nb, n_mem, d), F32)
    heads = jax.ShapeDtypeStruct((depth, nb, n_mem, HEADS, hd), F32)
    step = MEMKV_SEQS if nb % MEMKV_SEQS == 0 else 1
    w_spec = pl.BlockSpec((1, d, d), lambda l, b: (l, 0, 0))
    flat_spec = pl.BlockSpec((1, step, n_mem, d), lambda l, b: (l, b, 0, 0))
    heads_spec = pl.BlockSpec((1, step, n_mem, HEADS, hd), lambda l, b: (l, b, 0, 0, 0))
    return pl.pallas_call(
        _memkv_kernel,
        grid=(depth, nb // step),
        in_specs=[pl.BlockSpec((step, n_mem, d), lambda l, b: (b, 0, 0)),
                  pl.BlockSpec((1, 1, d), lambda l, b: (l, 0, 0)),
                  w_spec, w_spec],
        out_specs=(flat_spec, flat_spec, heads_spec, heads_spec),
        out_shape=(flat, flat, heads, heads),
        scratch_shapes=[pltpu.VMEM((d, d), BF16), pltpu.VMEM((d, d), BF16)],
        compiler_params=_params(2),
        name="memory_kv",
    )(mem, norm_mem.reshape(depth, 1, d), w_xk, w_xv)


def _gmlp_kernel(xp_ref, xs_ref, gn_ref, win_ref, lng_ref, lnb_ref, wmix_ref, bias_ref, wout_ref,
                 op_ref, os_ref, vs_ref, win_bf, wout_bf, *, n_prompt):
    i = pl.program_id(0)
    width = wout_ref.shape[0]
    gdim = width // GROUPS

    @pl.when(i == 0)
    def _():
        win_bf[...] = win_ref[...].astype(BF16)
        wout_bf[...] = wout_ref[...].astype(BF16)

    def spatial(vb, mode):
        chunks = []
        for c in range(vb.shape[0] // CHUNK):
            cols = [_dot(wmix_ref[mode, g], vb[c * CHUNK:(c + 1) * CHUNK, g * gdim:(g + 1) * gdim])
                    for g in range(GROUPS)]
            chunks.append(jnp.concatenate(cols, axis=1) + bias_ref[mode])
        return chunks[0] if len(chunks) == 1 else jnp.concatenate(chunks, axis=0)

    def mixer(parts, mode):
        zs = [_dot(_rmsnorm(x, gn_ref[...]).astype(BF16), win_bf[...]) for x in parts]
        zs = [jax.nn.gelu(z) for z in zs]
        vs = [_layernorm(z[:, width:], lng_ref[...], lnb_ref[...]) for z in zs]
        mixed = [spatial(v.astype(BF16), mode) for v in vs]
        gated = [(z[:, :width] * m).astype(BF16) for z, m in zip(zs, mixed)]
        outs = [x + _dot(g, wout_bf[...]) for x, g in zip(parts, gated)]
        return outs, vs

    @pl.when(i < n_prompt)
    def _():
        ranges = _row_ranges(xp_ref.shape[0], CHUNK)
        outs, _ = mixer([xp_ref[r, :] for r in ranges], 0)
        for r, out in zip(ranges, outs):
            op_ref[r, :] = out

    @pl.when(i == n_prompt)
    def _():
        outs, vs = mixer([xs_ref[...]], 1)
        os_ref[...] = outs[0]
        vs_ref[...] = vs[0]


def _gmlp(xp, xs, g_norm, w_in, ln_g, ln_b, w_s, b_s, w_out):
    tp, d = xp.shape
    ns = xs.shape[0]
    width = w_out.shape[0]
    gdim = width // GROUPS
    assert tp % TOKEN_TILE == 0 and TOKEN_TILE % CHUNK == 0 and ns == CHUNK
    n_prompt = tp // TOKEN_TILE
    causal = jnp.tril(jnp.ones((CHUNK, CHUNK), dtype=bool))
    w_prompt = jnp.where(causal[None], w_s, 0.0)
    w_sample = w_s[:, 0, 0][:, None, None] * jnp.eye(CHUNK, dtype=F32)[None]
    wmix = jnp.stack([w_prompt, w_sample]).astype(BF16)
    b_prompt = jnp.repeat(jnp.transpose(b_s), gdim, axis=1)
    b_sample = jnp.broadcast_to(jnp.repeat(b_s[:, 0], gdim)[None], (CHUNK, width))
    bias = jnp.stack([b_prompt, b_sample])

    tile = lambda i: (jnp.minimum(i, n_prompt - 1), 0)
    row = lambda a: a.reshape(1, -1)
    return pl.pallas_call(
        functools.partial(_gmlp_kernel, n_prompt=n_prompt),
        grid=(n_prompt + 1,),
        in_specs=[pl.BlockSpec((TOKEN_TILE, d), tile),
                  _resident((ns, d)), _resident((1, d)), _resident(w_in.shape),
                  _resident((1, width)), _resident((1, width)),
                  _resident(wmix.shape), _resident(bias.shape), _resident(w_out.shape)],
        out_specs=(pl.BlockSpec((TOKEN_TILE, d), tile),
                   pl.BlockSpec((ns, d), lambda i: (0, 0)),
                   pl.BlockSpec((ns, width), lambda i: (0, 0))),
        out_shape=(jax.ShapeDtypeStruct((tp, d), F32),
                   jax.ShapeDtypeStruct((ns, d), F32),
                   jax.ShapeDtypeStruct((ns, width), F32)),
        scratch_shapes=[pltpu.VMEM(w_in.shape, BF16), pltpu.VMEM(w_out.shape, BF16)],
        compiler_params=_params(1),
        name="gmlp_mixer",
    )(xp, xs, row(g_norm), w_in, row(ln_g), row(ln_b), wmix, bias, w_out)


def _softmax_rows(s, axis):
    m = jnp.max(s, axis=axis, keepdims=True)
    e = jnp.exp(s - m)
    return e / jnp.sum(e, axis=axis, keepdims=True)


def _xattn_kernel(xp_ref, xs_ref, gn_ref, wq_ref, wo_ref, k_ref, v_ref,
                  op_ref, qs_ref, wq_bf, wo_bf, *, n_prompt):
    i = pl.program_id(0)
    d = wq_ref.shape[-1]
    hd = d // HEADS
    scale = hd ** -0.5

    @pl.when(i == 0)
    def _():
        wq_bf[...] = wq_ref[0].astype(BF16)
        wo_bf[...] = wo_ref[0].astype(BF16)

    def attend(parts):
        kb = k_ref[0, 0].astype(BF16)
        vb = v_ref[0, 0].astype(BF16)
        qs = [_dot(_rmsnorm(x, gn_ref[0]).astype(BF16), wq_bf[...]).astype(BF16) for x in parts]
        outs = []
        for x, q in zip(parts, qs):
            heads = []
            for h in range(HEADS):
                cols = slice(h * hd, (h + 1) * hd)
                s = lax.dot_general(q[:, cols], kb[:, cols], (((1,), (1,)), ((), ())),
                                    preferred_element_type=F32) * scale
                heads.append(_dot(_softmax_rows(s, -1).astype(BF16), vb[:, cols]))
            outs.append(jnp.concatenate(heads, axis=1).astype(BF16))
        return [x + _dot(o, wo_bf[...]) for x, o in zip(parts, outs)]

    @pl.when(i < n_prompt)
    def _():
        ranges = _row_ranges(xp_ref.shape[0], SUBLANES)
        for r, out in zip(ranges, attend([xp_ref[r, :] for r in ranges])):
            op_ref[r, :] = out

    @pl.when(i == n_prompt)
    def _():
        qs_ref[...] = _dot(_rmsnorm(xs_ref[...], gn_ref[0]).astype(BF16), wq_bf[...])


def _xattn_prompt(xp, xs, norm_xattn, w_xq, w_xo, mem_k, mem_v, layer):
    tp, d = xp.shape
    ns = xs.shape[0]
    depth, nb, n_mem, _ = mem_k.shape
    n_prompt = tp // TOKEN_TILE
    tiles_per_seq = n_prompt // nb
    assert tiles_per_seq * nb == n_prompt
    tile = lambda i: (jnp.minimum(i, n_prompt - 1), 0)
    kv = lambda i: (layer, jnp.minimum(i, n_prompt - 1) // tiles_per_seq, 0, 0)
    of_layer = lambda shape: pl.BlockSpec((1,) + shape, lambda i: (layer,) + (0,) * len(shape),
                                          pipeline_mode=pl.Buffered(1))
    return pl.pallas_call(
        functools.partial(_xattn_kernel, n_prompt=n_prompt),
        grid=(n_prompt + 1,),
        in_specs=[pl.BlockSpec((TOKEN_TILE, d), tile),
                  _resident((ns, d)), of_layer((1, d)), of_layer((d, d)), of_layer((d, d)),
                  pl.BlockSpec((1, 1, n_mem, d), kv), pl.BlockSpec((1, 1, n_mem, d), kv)],
        out_specs=(pl.BlockSpec((TOKEN_TILE, d), tile),
                   pl.BlockSpec((ns, d), lambda i: (0, 0))),
        out_shape=(jax.ShapeDtypeStruct((tp, d), F32), jax.ShapeDtypeStruct((ns, d), F32)),
        scratch_shapes=[pltpu.VMEM((d, d), BF16), pltpu.VMEM((d, d), BF16)],
        compiler_params=_params(1),
        name="xattn_prompt",
    )(xp, xs, norm_xattn.reshape(depth, 1, d), w_xq, w_xo, mem_k, mem_v)


def _xattn_sample_kernel(q_ref, k_ref, v_ref, o_ref, *, block, scale):
    i = pl.program_id(0)
    for b in range(block):
        r = i * block + b
        s = jnp.sum(k_ref[0, b] * q_ref[r][None], axis=-1, keepdims=True) * scale
        p = _softmax_rows(s, 0)
        o_ref[r] = jnp.sum(p * v_ref[0, b], axis=0)


def _xattn_sample(qs, cache_k, cache_v, layer):
    _, ns, n_mem, heads, hd = cache_k.shape
    block = SAMPLE_ATTN_BLOCK
    assert ns % block == 0
    kv = pl.BlockSpec((1, block, n_mem, heads, hd), lambda i: (layer, i, 0, 0, 0))
    return pl.pallas_call(
        functools.partial(_xattn_sample_kernel, block=block, scale=hd ** -0.5),
        grid=(ns // block,),
        in_specs=[_resident((ns, heads, hd)), kv, kv],
        out_specs=pl.BlockSpec((ns, heads, hd), lambda i: (0, 0, 0)),
        out_shape=jax.ShapeDtypeStruct((ns, heads, hd), F32),
        compiler_params=_params(1),
        name="xattn_sample",
    )(qs.reshape(ns, heads, hd), cache_k, cache_v).reshape(qs.shape)


def _attn_out_kernel(o_ref, xs_ref, wo_ref, os_ref):
    os_ref[...] = xs_ref[...] + _dot(o_ref[...].astype(BF16), wo_ref[0].astype(BF16))


def _attn_out_sample(o, xs, w_xo, layer):
    ns, d = xs.shape
    return pl.pallas_call(
        _attn_out_kernel,
        grid=(1,),
        in_specs=[_resident((ns, d)), _resident((ns, d)),
                  pl.BlockSpec((1, d, d), lambda i: (layer, 0, 0))],
        out_specs=pl.BlockSpec((ns, d), lambda i: (0, 0)),
        out_shape=jax.ShapeDtypeStruct((ns, d), F32),
        compiler_params=_params(1),
        name="attn_out_sample",
    )(o, xs, w_xo)


def _ffn_kernel(xp_ref, xs_ref, gn_ref, wg_ref, wu_ref, wd_ref, op_ref, os_ref, *, n_prompt):
    i = pl.program_id(0)
    f_dim = wg_ref.shape[1]

    def ffn(x):
        xn = _rmsnorm(x, gn_ref[...]).astype(BF16)
        acc = x
        for f in range(0, f_dim, FFN_F_CHUNK):
            cols = slice(f, f + FFN_F_CHUNK)
            hid = jax.nn.silu(_dot(xn, wg_ref[:, cols])) * _dot(xn, wu_ref[:, cols])
            acc = acc + _dot(hid.astype(BF16), wd_ref[cols, :])
        return acc

    @pl.when(i < n_prompt)
    def _():
        op_ref[...] = ffn(xp_ref[...])

    @pl.when(i == n_prompt)
    def _():
        os_ref[...] = ffn(xs_ref[...])


def _ffn(xp, xs, g_norm, w_gate, w_up, w_down):
    tp, d = xp.shape
    ns = xs.shape[0]
    f_dim = w_gate.shape[1]
    assert f_dim % FFN_F_CHUNK == 0
    n_prompt = tp // TOKEN_TILE
    wg = _cast_bf16(w_gate, d // 4)
    wu = _cast_bf16(w_up, d // 4)
    wd = _cast_bf16(w_down, f_dim // 4)
    tile = lambda i: (jnp.minimum(i, n_prompt - 1), 0)
    return pl.pallas_call(
        functools.partial(_ffn_kernel, n_prompt=n_prompt),
        grid=(n_prompt + 1,),
        in_specs=[pl.BlockSpec((TOKEN_TILE, d), tile),
                  _resident((ns, d)), _resident((1, d)),
                  _resident(wg.shape), _resident(wu.shape), _resident(wd.shape)],
        out_specs=(pl.BlockSpec((TOKEN_TILE, d), tile),
                   pl.BlockSpec((ns, d), lambda i: (0, 0))),
        out_shape=(jax.ShapeDtypeStruct((tp, d), F32), jax.ShapeDtypeStruct((ns, d), F32)),
        compiler_params=_params(1),
        name="dense_swiglu",
    )(xp, xs, g_norm.reshape(1, d), wg, wu, wd)


CONV_HALO = 32


SUBLANES = 8
CONV_UNROLL = 8
CONV_TAIL = 16


def _depthwise_conv(cbuf, y_ref, wdw_ref, bdw_ref, tm, taps):
    first = CONV_HALO - (taps - 1)
    d = y_ref.shape[1]
    max_a = (first + taps - 1) // SUBLANES
    classes = [[(a, SUBLANES * a + r - first) for a in range(max_a + 1)
                if 0 <= SUBLANES * a + r - first < taps] for r in range(SUBLANES)]
    row_i = lax.broadcasted_iota(jnp.int32, (SUBLANES, LANES), 0)

    for l in range(d // LANES):
        lanes = slice(l * LANES, (l + 1) * LANES)
        w = [jnp.broadcast_to(wdw_ref[k:k + 1, lanes], (SUBLANES, LANES)) for k in range(taps)]
        bias = jnp.broadcast_to(bdw_ref[:, lanes], (SUBLANES, LANES))

        def rotated_q(groups, j, w=w):
            out = []
            for r in range(SUBLANES):
                q = None
                for a, k in classes[r]:
                    term = w[k] * groups[j + a]
                    q = term if q is None else q + term
                out.append(q if r == 0 else pltpu.roll(q, SUBLANES - r, axis=0))
            return out

        head = {a: cbuf[SUBLANES * a:SUBLANES * (a + 1), lanes] for a in range(max_a + 1)}

        prev = rotated_q(head, 0)
        for blk in range(tm // (SUBLANES * CONV_UNROLL)):
            base = blk * SUBLANES * CONV_UNROLL
            groups = {j: cbuf[base + SUBLANES * j:base + SUBLANES * (j + 1), lanes]
                      for j in range(1, CONV_UNROLL + max_a + 1)}
            for u in range(CONV_UNROLL):
                nxt = rotated_q(groups, u + 1)
                y = prev[0] + bias
                for r in range(1, SUBLANES):
                    y = y + jnp.where(row_i < SUBLANES - r, prev[r], nxt[r])
                y_ref[base + SUBLANES * u:base + SUBLANES * (u + 1), lanes] = y
                prev = nxt


def _conv_prompt_kernel(xp_ref, gn_ref, w1_ref, wdw_ref, bdw_ref, lng_ref, lnb_ref, w2_ref,
                        op_ref, st_ref, w1_bf, w2_bf, cbuf, ybuf, *, tiles_per_seq):
    i = pl.program_id(0)
    d = w2_ref.shape[0]
    taps = wdw_ref.shape[0]
    tm = xp_ref.shape[0]

    @pl.when(i == 0)
    def _():
        w1_bf[...] = w1_ref[...].astype(BF16)
        w2_bf[...] = w2_ref[...].astype(BF16)
        cbuf[CONV_HALO + tm:CONV_HALO + tm + CONV_TAIL, :] = jnp.zeros((CONV_TAIL, d), F32)

    @pl.when(i % tiles_per_seq == 0)
    def _():
        cbuf[0:CONV_HALO, :] = jnp.zeros((CONV_HALO, d), F32)

    x = xp_ref[...]
    ag = _dot(_rmsnorm(x, gn_ref[...]).astype(BF16), w1_bf[...])
    c = ag[:, :d] * jax.nn.sigmoid(ag[:, d:])
    cbuf[CONV_HALO:CONV_HALO + tm, :] = c
    first = CONV_HALO - (taps - 1)
    _depthwise_conv(cbuf, ybuf, wdw_ref, bdw_ref, tm, taps)
    t = jax.nn.silu(_layernorm(ybuf[...], lng_ref[...], lnb_ref[...])).astype(BF16)
    op_ref[...] = x + _dot(t, w2_bf[...])
    cbuf[0:CONV_HALO, :] = cbuf[tm:tm + CONV_HALO, :]

    @pl.when(i % tiles_per_seq == tiles_per_seq - 1)
    def _():
        st_ref[0] = cbuf[first:CONV_HALO, :]


def _conv_prompt(xp, n_seq, g_norm, w_pw1, w_dw, b_dw, ln_g, ln_b, w_pw2):
    tp, d = xp.shape
    taps = w_dw.shape[0]
    n_prompt = tp // TOKEN_TILE
    tiles_per_seq = n_prompt // n_seq
    assert tiles_per_seq * n_seq == n_prompt and taps - 1 <= CONV_HALO <= TOKEN_TILE
    assert TOKEN_TILE % (SUBLANES * CONV_UNROLL) == 0 and CONV_TAIL >= SUBLANES * 2
    row = lambda a: a.reshape(1, -1)
    return pl.pallas_call(
        functools.partial(_conv_prompt_kernel, tiles_per_seq=tiles_per_seq),
        grid=(n_prompt,),
        in_specs=[pl.BlockSpec((TOKEN_TILE, d), lambda i: (i, 0)),
                  _resident((1, d)), _resident(w_pw1.shape), _resident(w_dw.shape),
                  _resident((1, d)), _resident((1, d)), _resident((1, d)), _resident(w_pw2.shape)],
        out_specs=(pl.BlockSpec((TOKEN_TILE, d), lambda i: (i, 0)),
                   pl.BlockSpec((1, taps - 1, d), lambda i: (i // tiles_per_seq, 0, 0))),
        out_shape=(jax.ShapeDtypeStruct((tp, d), F32),
                   jax.ShapeDtypeStruct((n_seq, taps - 1, d), F32)),
        scratch_shapes=[pltpu.VMEM(w_pw1.shape, BF16), pltpu.VMEM(w_pw2.shape, BF16),
                        pltpu.VMEM((CONV_HALO + TOKEN_TILE + CONV_TAIL, d), F32),
                        pltpu.VMEM((TOKEN_TILE, d), F32)],
        compiler_params=_params(1),
        name="conv_prompt",
    )(xp, row(g_norm), w_pw1, w_dw, row(b_dw), row(ln_g), row(ln_b), w_pw2)


def _conv_sample_kernel(xs_ref, gn_ref, w1_ref, wdw_ref, bdw_ref, lng_ref, lnb_ref, w2_ref, st_ref,
                        os_ref, sto_ref, c_all, t_all, *, block):
    i = pl.program_id(0)
    d = w2_ref.shape[0]
    taps = wdw_ref.shape[0]

    @pl.when(i == 0)
    def _():
        ag = _dot(_rmsnorm(xs_ref[...], gn_ref[...]).astype(BF16), w1_ref[...].astype(BF16))
        c_all[...] = ag[:, :d] * jax.nn.sigmoid(ag[:, d:])

    w_hist = wdw_ref[0:taps - 1, :]
    w_last = wdw_ref[taps - 1:taps, :]
    for b in range(block):
        r = i * block + b
        hist = st_ref[0, b]
        cb = c_all[pl.ds(r, 1), :]
        y = jnp.sum(hist * w_hist, axis=0, keepdims=True) + cb * w_last + bdw_ref[...]
        t_all[pl.ds(r, 1), :] = jax.nn.silu(_layernorm(y, lng_ref[...], lnb_ref[...]))
        sto_ref[0, b, 0:taps - 2, :] = hist[1:taps - 1, :]
        sto_ref[0, b, taps - 2:taps - 1, :] = cb

    @pl.when(i == pl.num_programs(0) - 1)
    def _():
        os_ref[...] = xs_ref[...] + _dot(t_all[...].astype(BF16), w2_ref[...].astype(BF16))


def _conv_sample(xs, state, g_norm, w_pw1, w_dw, b_dw, ln_g, ln_b, w_pw2):
    ns, d = xs.shape
    taps = w_dw.shape[0]
    block = SAMPLE_CONV_BLOCK
    assert ns % block == 0
    row = lambda a: a.reshape(1, -1)
    st = pl.BlockSpec((1, block, taps - 1, d), lambda i: (0, i, 0, 0))
    return pl.pallas_call(
        functools.partial(_conv_sample_kernel, block=block),
        grid=(ns // block,),
        in_specs=[_resident((ns, d)), _resident((1, d)), _resident(w_pw1.shape),
                  _resident(w_dw.shape), _resident((1, d)), _resident((1, d)), _resident((1, d)),
                  _resident(w_pw2.shape), st],
        out_specs=(pl.BlockSpec((ns, d), lambda i: (0, 0)), st),
        out_shape=(jax.ShapeDtypeStruct((ns, d), F32), jax.ShapeDtypeStruct(state.shape, F32)),
        scratch_shapes=[pltpu.VMEM((ns, d), F32), pltpu.VMEM((ns, d), F32)],
        compiler_params=_params(1),
        name="conv_sample",
    )(xs, row(g_norm), w_pw1, w_dw, row(b_dw), row(ln_g), row(ln_b), w_pw2, state)


RANK_BITS = 24
RANK_SPAN = 1 << RANK_BITS
NO_RANK = -float(1 << 20)


def _router_kernel(xp_ref, xs_ref, gn_ref, wr_ref, hn_ref, c1_ref, c2_ref, meta_ref, seen_ref, count_ref, seen,
                   *, n_prompt, n_experts):
    i = pl.program_id(0)

    def route(x):
        xn = _rmsnorm(x, gn_ref[...])
        xh = xn.astype(BF16)
        xl = (xn - xh.astype(F32)).astype(BF16)
        logits = _dot(jnp.concatenate([xh, xl, xh], axis=1), wr_ref[...])
        lane = lax.broadcasted_iota(jnp.int32, logits.shape, 1).astype(F32)
        neg = jnp.float32(-jnp.inf)
        logits = jnp.where(lane < n_experts, logits, neg)
        v1 = jnp.max(logits, axis=-1, keepdims=True)
        i1 = jnp.min(jnp.where(logits == v1, lane, float(LANES)), axis=-1, keepdims=True)
        rest = jnp.where(lane == i1, neg, logits)
        v2 = jnp.max(rest, axis=-1, keepdims=True)
        i2 = jnp.min(jnp.where(rest == v2, lane, float(LANES)), axis=-1, keepdims=True)
        e2 = jnp.exp(v2 - v1)
        denom = 1.0 + e2

        rows = x.shape[0]
        pick1 = jnp.where(lane == i1, 1.0, 0.0)
        pick2 = jnp.where(lane == i2, 1.0, 0.0)
        picks = pick1 + pick2
        r_i = lax.broadcasted_iota(jnp.int32, (rows, rows), 0)
        c_i = lax.broadcasted_iota(jnp.int32, (rows, rows), 1)
        earlier = jnp.where(c_i < r_i, 1.0, 0.0).astype(BF16)
        before = _dot(earlier, picks.astype(BF16)) + seen[0:1, :]
        rank1 = jnp.sum(pick1 * before, axis=-1, keepdims=True)
        rank2 = jnp.sum(pick2 * before, axis=-1, keepdims=True)
        total = seen[0:1, :] + jnp.sum(picks, axis=0, keepdims=True)

        meta = jnp.where(lane == 0.0, i1, jnp.where(lane == 1.0, i2, jnp.where(lane == 2.0, rank1,
               jnp.where(lane == 3.0, rank2, jnp.where(lane == 4.0, 1.0 / denom,
               jnp.where(lane == 5.0, e2 / denom, 0.0))))))
        wide = lambda a: jnp.broadcast_to(a, (rows, LANES)).astype(jnp.int32)
        code = lambda e, r: wide(e) * RANK_SPAN + wide(r)
        starts = [seen[0:1, :]]
        for w in range(1, TOKEN_TILE // DISPATCH_WIN):
            starts.append(before[w * DISPATCH_WIN:w * DISPATCH_WIN + 1, :] if w * DISPATCH_WIN < rows else total)
        pad = jnp.zeros((SUBLANES - len(starts), LANES), F32)
        seen_ref[0] = jnp.concatenate(starts + [pad], axis=0).astype(jnp.int32)
        seen[0:1, :] = total
        return xn.astype(BF16), code(i1, rank1), code(i2, rank2), meta

    def no_token(rows):
        lane_t = lax.broadcasted_iota(jnp.int32, (rows, LANES), 1)
        return jnp.where(lane_t == 2, NO_RANK, jnp.where(lane_t == 3, NO_RANK, 0.0))

    def put(hn, code1, code2, meta):
        hn_ref[...] = hn
        c1_ref[...] = code1
        c2_ref[...] = code2
        by_lane = jnp.transpose(meta)[0:SUBLANES, :]
        for w in range(TOKEN_TILE // DISPATCH_WIN):
            meta_ref[w] = by_lane[:, w * DISPATCH_WIN:(w + 1) * DISPATCH_WIN]

    @pl.when(i == 0)
    def _():
        seen[...] = jnp.zeros(seen.shape, F32)

    @pl.when(i < n_prompt)
    def _():
        put(*route(xp_ref[...]))

    @pl.when(i == n_prompt)
    def _():
        ns = xs_ref.shape[0]
        hn, code1, code2, meta = route(xs_ref[...])
        fill = lambda a, v: jnp.concatenate([a, jnp.full((TOKEN_TILE - ns, a.shape[1]), v, a.dtype)], axis=0)
        put(fill(hn, 0.0), fill(code1, -1), fill(code2, -1),
            jnp.concatenate([meta, no_token(TOKEN_TILE - ns)], axis=0))
        count_ref[...] = jnp.broadcast_to(seen[0:1, :], count_ref.shape).astype(jnp.int32)

    @pl.when(i > n_prompt)
    def _():
        none = jnp.full((TOKEN_TILE, LANES), -1, jnp.int32)
        put(jnp.zeros(hn_ref.shape, BF16), none, none, no_token(TOKEN_TILE))
        seen_ref[0] = jnp.broadcast_to(seen[0:1, :], (SUBLANES, LANES)).astype(jnp.int32)


def _router(xp, xs, g_norm, w_router):
    tp, d = xp.shape
    ns = xs.shape[0]
    n_experts = w_router.shape[1]
    n_prompt = tp // TOKEN_TILE
    n_tiles = n_prompt + 1 + -(-(DISPATCH_SPAN - 1) * DISPATCH_WIN // TOKEN_TILE)
    t_pad = n_tiles * TOKEN_TILE
    assert TOKEN_TILE % DISPATCH_WIN == 0 and TOKEN_TILE // DISPATCH_WIN <= SUBLANES
    assert t_pad * TOP_K < RANK_SPAN and n_experts * RANK_SPAN < 2 ** 31
    wins = TOKEN_TILE // DISPATCH_WIN
    wr = jnp.pad(w_router, ((0, 0), (0, LANES - n_experts)))
    wr_hi = wr.astype(BF16)
    wr_lo = (wr - wr_hi.astype(F32)).astype(BF16)
    wr = jnp.concatenate([wr_hi, wr_hi, wr_lo], axis=0)
    tile = lambda i: (jnp.minimum(i, n_prompt - 1), 0)
    lanes_spec = pl.BlockSpec((TOKEN_TILE, LANES), lambda i: (i, 0))
    return pl.pallas_call(
        functools.partial(_router_kernel, n_prompt=n_prompt, n_experts=n_experts),
        grid=(n_tiles,),
        in_specs=[pl.BlockSpec((TOKEN_TILE, d), tile),
                  _resident((ns, d)), _resident((1, d)), _resident(wr.shape)],
        out_specs=(pl.BlockSpec((TOKEN_TILE, d), lambda i: (i, 0)), lanes_spec, lanes_spec,
                   pl.BlockSpec((wins, SUBLANES, DISPATCH_WIN), lambda i: (i, 0, 0)),
                   pl.BlockSpec((1, SUBLANES, LANES), lambda i: (i, 0, 0)),
                   pl.BlockSpec((SUBLANES, LANES), lambda i: (0, 0))),
        out_shape=(jax.ShapeDtypeStruct((t_pad, d), BF16),
                   jax.ShapeDtypeStruct((t_pad, LANES), jnp.int32),
                   jax.ShapeDtypeStruct((t_pad, LANES), jnp.int32),
                   jax.ShapeDtypeStruct((n_tiles * wins, SUBLANES, DISPATCH_WIN), F32),
                   jax.ShapeDtypeStruct((n_tiles, SUBLANES, LANES), jnp.int32),
                   jax.ShapeDtypeStruct((SUBLANES, LANES), jnp.int32)),
        scratch_shapes=[pltpu.VMEM((SUBLANES, LANES), F32)],
        compiler_params=_params(1),
        name="moe_router",
    )(xp, xs, g_norm.reshape(1, d), wr)


def _lane_tile(a, width):
    return jnp.concatenate([a] * (width // LANES), axis=1)


def _dispatch_kernel(slot_nsub, win_lo, win_hi, first_row, meta_ref, hn_ref, x_ref, g_ref, *, n_experts):
    s = pl.program_id(0)
    n_sub = slot_nsub[s]
    subs_per_slot = SLOT_ROWS // SUB_ROWS
    d = hn_ref.shape[1]
    span = DISPATCH_SPAN * DISPATCH_WIN
    row_in_block = lax.broadcasted_iota(jnp.int32, (SUB_ROWS, span), 0)

    def sub_block(j, carry):
        rows = pl.ds(pl.multiple_of(j * SUB_ROWS, SUB_ROWS), SUB_ROWS)
        row_id = (row_in_block + (s * SLOT_ROWS + j * SUB_ROWS)).astype(F32)
        k = s * subs_per_slot + j

        def picked(c):
            w = win_lo[k] + c * DISPATCH_SPAN
            along = lambda row: jnp.concatenate(
                [meta_ref[w + n, row:row + 1, :] for n in range(DISPATCH_SPAN)], axis=1)

            def owned_row(expert, rank):
                first = jnp.zeros_like(expert)
                for e in range(n_experts):
                    first = jnp.where(expert == float(e), first_row[e].astype(F32), first)
                return first + rank

            hit1 = owned_row(along(0), along(2)) == row_id
            hit2 = owned_row(along(1), along(3)) == row_id
            sel = jnp.where(hit1, 1.0, jnp.where(hit2, 1.0, 0.0)).astype(BF16)
            gates = jnp.where(hit1, along(4), jnp.where(hit2, along(5), 0.0))
            base = pl.multiple_of(w * DISPATCH_WIN, DISPATCH_WIN)
            return (_dot(sel, hn_ref[pl.ds(base, span), :]).astype(BF16),
                    jnp.broadcast_to(jnp.sum(gates, axis=-1, keepdims=True), (SUB_ROWS, LANES)))

        x_ref[rows, :], g_ref[rows, :] = picked(0)

        def more(c, carry):
            xs, gs = picked(c)
            x_ref[rows, :] += xs
            g_ref[rows, :] += gs
            return carry

        n_win = win_hi[k] + 1 - win_lo[k]
        lax.fori_loop(1, (n_win + DISPATCH_SPAN - 1) // DISPATCH_SPAN, more, 0)
        return carry

    def zero_block(j, carry):
        rows = pl.ds(pl.multiple_of(j * SUB_ROWS, SUB_ROWS), SUB_ROWS)
        x_ref[rows, :] = jnp.zeros((SUB_ROWS, d), BF16)
        g_ref[rows, :] = jnp.zeros((SUB_ROWS, LANES), F32)
        return carry

    lax.fori_loop(0, n_sub, sub_block, 0)
    lax.fori_loop(n_sub, subs_per_slot, zero_block, 0)


def _dispatch(hn, meta, slot_nsub, win_lo, win_hi, first_row):
    t_pad, d = hn.shape
    n_slots = slot_nsub.shape[0]
    assert t_pad % DISPATCH_WIN == 0
    grid_spec = pltpu.PrefetchScalarGridSpec(
        num_scalar_prefetch=4,
        grid=(n_slots,),
        in_specs=[pl.BlockSpec(meta.shape, lambda s, *_: (0, 0, 0)),
                  pl.BlockSpec((t_pad, d), lambda s, *_: (0, 0), pipeline_mode=pl.Buffered(1))],
        out_specs=(pl.BlockSpec((SLOT_ROWS, d), lambda s, *_: (s, 0)),
                   pl.BlockSpec((SLOT_ROWS, LANES), lambda s, *_: (s, 0))),
    )
    return pl.pallas_call(
        functools.partial(_dispatch_kernel, n_experts=first_row.shape[0]),
        grid_spec=grid_spec,
        out_shape=(jax.ShapeDtypeStruct((n_slots * SLOT_ROWS, d), BF16),
                   jax.ShapeDtypeStruct((n_slots * SLOT_ROWS, LANES), F32)),
        compiler_params=_params(1),
        name="moe_dispatch",
    )(slot_nsub, win_lo, win_hi, first_row, meta, hn)


def _expert_kernel(slot_expert, slot_nsub, x_ref, gate_ref, wg_ref, wu_ref, wd_ref, y_ref, acc):
    s = pl.program_id(0)
    f = pl.program_id(1)
    n_sub = slot_nsub[s]
    last_f = f == pl.num_programs(1) - 1
    d = y_ref.shape[1]

    def zero_acc(j, carry):
        rows = pl.ds(pl.multiple_of(j * SUB_ROWS, SUB_ROWS), SUB_ROWS)
        acc[rows, :] = jnp.zeros((SUB_ROWS, d), F32)
        return carry

    @pl.when(f == 0)
    def _():
        lax.fori_loop(0, n_sub, zero_acc, 0)

    def block(start, size):
        rows = pl.ds(pl.multiple_of(start, SUB_ROWS), size)
        xg = x_ref[rows, :]
        hid = jax.nn.silu(_dot(xg, wg_ref[0].astype(BF16))) * _dot(xg, wu_ref[0].astype(BF16))
        acc[rows, :] += _dot(hid.astype(BF16), wd_ref[0].astype(BF16))

        @pl.when(last_f)
        def _():
            y_ref[rows, :] = (acc[rows, :] * _lane_tile(gate_ref[rows, :], d)).astype(y_ref.dtype)

    def quad(j, carry):
        block(j * (4 * SUB_ROWS), 4 * SUB_ROWS)
        return carry

    lax.fori_loop(0, n_sub // 4, quad, 0)

    @pl.when(n_sub % 4 >= 2)
    def _():
        block((n_sub // 4) * (4 * SUB_ROWS), 2 * SUB_ROWS)

    @pl.when(n_sub % 2 == 1)
    def _():
        block((n_sub - 1) * SUB_ROWS, SUB_ROWS)

    def zero_block(j, carry):
        rows = pl.ds(pl.multiple_of(j * SUB_ROWS, SUB_ROWS), SUB_ROWS)
        y_ref[rows, :] = jnp.zeros((SUB_ROWS, d), y_ref.dtype)
        return carry

    @pl.when(last_f)
    def _():
        lax.fori_loop(n_sub, SLOT_ROWS // SUB_ROWS, zero_block, 0)


def _expert_ffn(x_slots, gate_rep, slot_expert, slot_nsub, w_gate, w_up, w_down):
    n_slots = slot_expert.shape[0]
    d = x_slots.shape[1]
    f_dim = w_gate.shape[2]
    tf = EXPERT_F_TILE
    assert f_dim % tf == 0 and SLOT_ROWS % SUB_ROWS == 0
    n_f = f_dim // tf
    f_tile = lambda s, f, sn: jnp.where(sn[s] > 0, f, n_f - 1)
    grid_spec = pltpu.PrefetchScalarGridSpec(
        num_scalar_prefetch=2,
        grid=(n_slots, f_dim // tf),
        in_specs=[pl.BlockSpec((SLOT_ROWS, d), lambda s, f, se, sn: (s, 0)),
                  pl.BlockSpec((SLOT_ROWS, LANES), lambda s, f, se, sn: (s, 0)),
                  pl.BlockSpec((1, d, tf), lambda s, f, se, sn: (se[s], 0, f_tile(s, f, sn))),
                  pl.BlockSpec((1, d, tf), lambda s, f, se, sn: (se[s], 0, f_tile(s, f, sn))),
                  pl.BlockSpec((1, tf, d), lambda s, f, se, sn: (se[s], f_tile(s, f, sn), 0))],
        out_specs=pl.BlockSpec((SLOT_ROWS, d), lambda s, f, se, sn: (s, 0)),
        scratch_shapes=[pltpu.VMEM((SLOT_ROWS, d), F32)],
    )
    return pl.pallas_call(
        _expert_kernel,
        grid_spec=grid_spec,
        out_shape=jax.ShapeDtypeStruct((n_slots * SLOT_ROWS, d), BF16),
        compiler_params=_params(2),
        name="expert_swiglu",
    )(slot_expert, slot_nsub, x_slots, gate_rep, w_gate, w_up, w_down)


def _combine_kernel(n_blocks, block_ids, first_row, xp_ref, xs_ref, c1_ref, c2_ref, gf_ref, *rest,
                    n_prompt, max_blocks, n_experts):
    y_refs = rest[:max_blocks]
    op_ref, os_ref = rest[max_blocks:]
    i = pl.program_id(0)

    def owned_row(code):
        expert = code >> RANK_BITS
        first = jnp.zeros_like(code)
        for e in range(n_experts):
            first = jnp.where(expert == e, first_row[e], first)
        return jnp.where(code < 0, -1, first + (code & (RANK_SPAN - 1)))

    def combine(x, o_ref):
        rows = x.shape[0]
        p1 = _lane_tile(owned_row(c1_ref[0:rows, :]), COMBINE_BLOCK)
        p2 = _lane_tile(owned_row(c2_ref[0:rows, :]), COMBINE_BLOCK)
        lane = lax.broadcasted_iota(jnp.int32, (rows, COMBINE_BLOCK), 1)

        def picked(blocks):
            total = None
            for b in blocks:
                first_row = jnp.where(b < n_blocks[i], block_ids[i * max_blocks + b] * COMBINE_BLOCK,
                                      -2 * COMBINE_BLOCK)
                row_id = lane + first_row
                sel = jnp.where(p1 == row_id, 1.0, jnp.where(p2 == row_id, 1.0, 0.0)).astype(BF16)
                part = _dot(sel, y_refs[b][...])
                total = part if total is None else total + part
            return total

        o_ref[...] = x + picked(range(0, min(COMBINE_ALWAYS, max_blocks)))
        for g in range(COMBINE_ALWAYS, max_blocks, COMBINE_GROUP):
            @pl.when(g < n_blocks[i])
            def _():
                o_ref[...] += picked(range(g, min(g + COMBINE_GROUP, max_blocks)))
        o_ref[...] = _rmsnorm(o_ref[...], gf_ref[...])

    @pl.when(i < n_prompt)
    def _():
        combine(xp_ref[...], op_ref)

    @pl.when(i == n_prompt)
    def _():
        combine(xs_ref[...], os_ref)


def _combine(xp, xs, codes, y, n_blocks, block_ids, first_row, g_final, max_blocks):
    tp, d = xp.shape
    ns = xs.shape[0]
    n_prompt = tp // TOKEN_TILE
    tile = lambda i, *_: (jnp.minimum(i, n_prompt - 1), 0)
    pos_spec = pl.BlockSpec((TOKEN_TILE, LANES), lambda i, *_: (i, 0))

    def y_spec(b):
        return pl.BlockSpec((COMBINE_BLOCK, d), lambda i, nb, ids, fr: (ids[i * max_blocks + b], 0))

    grid_spec = pltpu.PrefetchScalarGridSpec(
        num_scalar_prefetch=3,
        grid=(n_prompt + 1,),
        in_specs=[pl.BlockSpec((TOKEN_TILE, d), tile),
                  pl.BlockSpec((ns, d), lambda i, *_: (0, 0)),
                  pos_spec, pos_spec,
                  pl.BlockSpec((1, d), lambda i, *_: (0, 0))] + [y_spec(b) for b in range(max_blocks)],
        out_specs=(pl.BlockSpec((TOKEN_TILE, d), tile),
                   pl.BlockSpec((ns, d), lambda i, *_: (0, 0))),
    )
    return pl.pallas_call(
        functools.partial(_combine_kernel, n_prompt=n_prompt, max_blocks=max_blocks,
                          n_experts=first_row.shape[0]),
        grid_spec=grid_spec,
        out_shape=(jax.ShapeDtypeStruct((tp, d), F32), jax.ShapeDtypeStruct((ns, d), F32)),
        compiler_params=_params(1),
        name="moe_combine",
    )(n_blocks, block_ids, first_row, xp, xs, codes[0], codes[1], g_final.reshape(1, d),
      *([y] * max_blocks))


def _routing_tables(seen, counts, t_valid, n_experts):
    i32 = jnp.int32
    n_tiles = seen.shape[0]
    wins_per_tile = TOKEN_TILE // DISPATCH_WIN
    n_win = n_tiles * wins_per_tile
    subs_per_slot = SLOT_ROWS // SUB_ROWS
    n_slots = t_valid * TOP_K // SLOT_ROWS + n_experts
    experts = jnp.arange(n_experts, dtype=i32)

    cnt = counts[0, :n_experts]
    n_chunks = (cnt + SLOT_ROWS - 1) // SLOT_ROWS
    chunk_end = jnp.cumsum(n_chunks)
    slot_base = chunk_end - n_chunks
    n_used = chunk_end[-1]
    first_row = slot_base * SLOT_ROWS

    sid = jnp.arange(n_slots, dtype=i32)
    expert_of = lambda s: jnp.sum((s[..., None] >= chunk_end).astype(i32), axis=-1)
    slot_expert = jnp.where(sid < n_used, expert_of(sid), expert_of(n_used - 1)).astype(i32)
    of_slot = lambda tab: jnp.sum(jnp.where(slot_expert[:, None] == experts, tab, 0), axis=-1)
    slot_rank = (sid - of_slot(slot_base)) * SLOT_ROWS
    slot_rows = jnp.where(sid < n_used, jnp.clip(of_slot(cnt) - slot_rank, 0, SLOT_ROWS), 0)
    slot_nsub = ((slot_rows + SUB_ROWS - 1) // SUB_ROWS).astype(i32)

    win_seen = seen[:, :wins_per_tile, :n_experts].reshape(n_win, n_experts)
    seen_slot = jnp.sum(jnp.where(slot_expert[:, None, None] == experts, win_seen[None], 0), axis=-1)
    j = jnp.arange(subs_per_slot, dtype=i32)[None, :]
    rank_lo = slot_rank[:, None] + j * SUB_ROWS
    rank_hi = jnp.minimum(rank_lo + SUB_ROWS, of_slot(cnt)[:, None]) - 1
    window_of = lambda r: jnp.sum((seen_slot[:, None, :] <= r[:, :, None]).astype(i32), axis=-1) - 1
    active = j < slot_nsub[:, None]
    win_lo = jnp.where(active, window_of(rank_lo), 1).reshape(-1)
    win_hi = jnp.where(active, window_of(rank_hi), 0).reshape(-1)

    tile_lo = seen[:, 0, :n_experts]
    tile_hi = jnp.concatenate([tile_lo[1:], cnt[None, :]], axis=0)
    blk_lo = tile_lo // COMBINE_BLOCK
    per_expert = jnp.where(tile_hi > tile_lo, (tile_hi - 1) // COMBINE_BLOCK - blk_lo + 1, 0)
    ends = jnp.cumsum(per_expert, axis=1)
    n_blocks = ends[:, -1]
    max_blocks = TOKEN_TILE * TOP_K // COMBINE_BLOCK + 2 * n_experts
    b = jnp.arange(max_blocks, dtype=i32)[None, :]
    e_of_b = jnp.minimum(jnp.sum((ends[:, None, :] <= b[:, :, None]).astype(i32), axis=-1), n_experts - 1)
    take = lambda tab: jnp.sum(jnp.where(e_of_b[:, :, None] == experts, tab[:, None, :], 0), axis=-1)
    rank_block = take(blk_lo) + b - (take(ends) - take(per_expert))
    ids = jnp.sum(jnp.where(e_of_b[:, :, None] == experts, first_row // COMBINE_BLOCK, 0), axis=-1) + rank_block
    valid = b < n_blocks[:, None]
    id_bits = 1 << 16
    assert n_slots * subs_per_slot < id_bits
    keyed = jnp.where(valid, jnp.arange(n_tiles, dtype=i32)[:, None] * id_bits + ids, 0)
    block_ids = (lax.cummax(keyed, axis=0) % id_bits).reshape(-1)

    return dict(slot_expert=slot_expert, slot_nsub=slot_nsub, first_row=first_row.astype(i32),
                win_lo=win_lo.astype(i32), win_hi=win_hi.astype(i32),
                n_blocks=n_blocks.astype(i32), block_ids=block_ids.astype(i32), max_blocks=max_blocks)


def _moe(xp, xs, g_norm, w_router, w_gate, w_up, w_down, g_final):
    tp, d = xp.shape
    ns = xs.shape[0]
    n_experts = w_router.shape[1]
    assert ns <= TOKEN_TILE and SLOT_ROWS % COMBINE_BLOCK == 0
    hn, code1, code2, meta, seen, counts = _router(xp, xs, g_norm, w_router)
    t = _routing_tables(seen, counts, tp + ns, n_experts)
    x_slots, gate_rep = _dispatch(hn, meta, t["slot_nsub"], t["win_lo"], t["win_hi"], t["first_row"])
    y = _expert_ffn(x_slots, gate_rep, t["slot_expert"], t["slot_nsub"], w_gate, w_up, w_down)
    return _combine(xp, xs, (code1, code2), y, t["n_blocks"], t["block_ids"], t["first_row"], g_final,
                    t["max_blocks"])


def kernel(x_prompt, x_sample, mem_prompt, cache_mem_k, cache_mem_v, state_conv, norm_mix, norm_xattn, norm_ffn, norm_mem, norm_final, w_xq, w_xk, w_xv, w_xo, a_w_in, a_ln_g, a_ln_b, a_w_s, a_b_s, a_w_out, b_w_pw1, b_w_dw, b_b_dw, b_ln_g, b_ln_b, b_w_pw2, ffn_w_gate, ffn_w_up, ffn_w_down, moe_w_router, moe_w_gate, moe_w_up, moe_w_down):
    nb, seq, d = x_prompt.shape
    ns = x_sample.shape[0]
    depth = norm_mix.shape[0]
    n_mem = mem_prompt.shape[1]
    assert depth == 2 and x_sample.shape[1] == 1

    mem_k, mem_v, mem_k_heads, mem_v_heads = _memory_kv(mem_prompt, norm_mem, w_xk, w_xv)

    hp = x_prompt.reshape(nb * seq, d)
    hs = x_sample.reshape(ns, d)

    def cross_attention(hp, hs, layer):
        hp_new, qs = _xattn_prompt(hp, hs, norm_xattn, w_xq, w_xo, mem_k, mem_v, layer)
        o = _xattn_sample(qs, cache_mem_k, cache_mem_v, layer)
        hs_new = _attn_out_sample(o, hs, w_xo, layer)
        return hp_new, hs_new

    hp, hs, v_sample = _gmlp(hp, hs, norm_mix[0], a_w_in[0], a_ln_g[0], a_ln_b[0],
                             a_w_s[0], a_b_s[0], a_w_out[0])
    hp, hs = cross_attention(hp, hs, 0)
    hp, hs = _ffn(hp, hs, norm_ffn[0], ffn_w_gate[0], ffn_w_up[0], ffn_w_down[0])

    conv_w = (norm_mix[1], b_w_pw1[0], b_w_dw[0], b_b_dw[0], b_ln_g[0], b_ln_b[0], b_w_pw2[0])
    hp, conv_state_prompt = _conv_prompt(hp, nb, *conv_w)
    hs, conv_state_sample = _conv_sample(hs, state_conv, *conv_w)
    hp, hs = cross_attention(hp, hs, 1)
    yp, ys = _moe(hp, hs, norm_ffn[1], moe_w_router[0], moe_w_gate[0], moe_w_up[0], moe_w_down[0],
                  norm_final)

    return (yp.reshape(nb, seq, d),
            ys.reshape(ns, 1, d),
            mem_k_heads,
            mem_v_heads,
            conv_state_prompt[None],
            conv_state_sample,
            v_sample.reshape(1, ns, 1, -1))
```

```python
import functools

import jax
import jax.numpy as jnp
from jax import lax
from jax.experimental import pallas as pl
from jax.experimental.pallas import tpu as pltpu

F32 = jnp.float32
BF16 = jnp.bfloat16

RMS_EPS = 1e-6
LN_EPS = 1e-5
CHUNK = 128
GROUPS = 8
HEADS = 4
TOP_K = 2
LANES = 128
V7X_VMEM_LIMIT = 56 * 1024 * 1024

TOKEN_TILE = 512
WIDE_TILE = 1024
TILE_PARTS = 2
SAMPLE_ATTN_BLOCK = 8
MEMKV_SEQS = 2
SAMPLE_CONV_BLOCK = 16
SLOT_ROWS = 2304
SUB_ROWS = 256
EXPERT_F_TILE = 512
DISPATCH_WIN = 256
DISPATCH_SPAN = 5
COMBINE_BLOCK = 256
COMBINE_ALWAYS = 10
COMBINE_GROUP = 2
FFN_F_CHUNK = 256


def _params(n_axes, vmem=V7X_VMEM_LIMIT):
    return pltpu.CompilerParams(dimension_semantics=("arbitrary",) * n_axes,
                                vmem_limit_bytes=vmem)


def _resident(shape):
    nd = len(shape)
    return pl.BlockSpec(shape, lambda *_: (0,) * nd, pipeline_mode=pl.Buffered(1))


def _rmsnorm(x, g):
    return x * lax.rsqrt(jnp.mean(x * x, axis=-1, keepdims=True) + RMS_EPS) * g


def _layernorm(x, g, b):
    xc = x - jnp.mean(x, axis=-1, keepdims=True)
    var = jnp.mean(xc * xc, axis=-1, keepdims=True)
    return xc * lax.rsqrt(var + LN_EPS) * g + b


def _dot(a, b):
    return jnp.dot(a, b, preferred_element_type=F32)


def _row_ranges(n_rows, multiple):
    part = n_rows // TILE_PARTS
    assert part % multiple == 0
    return [slice(p * part, (p + 1) * part) for p in range(TILE_PARTS)]


def _cast_kernel(x_ref, o_ref):
    o_ref[...] = x_ref[...].astype(o_ref.dtype)


def _cast_bf16(w, rows_per_step):
    r, c = w.shape
    return pl.pallas_call(
        _cast_kernel,
        grid=(r // rows_per_step,),
        in_specs=[pl.BlockSpec((rows_per_step, c), lambda i: (i, 0))],
        out_specs=pl.BlockSpec((rows_per_step, c), lambda i: (i, 0)),
        out_shape=jax.ShapeDtypeStruct((r, c), BF16),
        compiler_params=_params(1),
        name="cast_bf16",
    )(w)


def _memkv_kernel(mem_ref, g_ref, wk_ref, wv_ref, k_ref, v_ref, kh_ref, vh_ref, wk_bf, wv_bf):
    @pl.when(pl.program_id(1) == 0)
    def _():
        wk_bf[...] = wk_ref[0].astype(BF16)
        wv_bf[...] = wv_ref[0].astype(BF16)

    hd = kh_ref.shape[-1]
    nb_step, n_mem, d = mem_ref.shape
    mn = _rmsnorm(mem_ref[...].reshape(nb_step * n_mem, d), g_ref[0]).astype(BF16)
    k = _dot(mn, wk_bf[...])
    v = _dot(mn, wv_bf[...])
    for b in range(nb_step):
        rows = slice(b * n_mem, (b + 1) * n_mem)
        k_ref[0, b] = k[rows]
        v_ref[0, b] = v[rows]
        for h in range(HEADS):
            kh_ref[0, b, :, h, :] = k[rows, h * hd:(h + 1) * hd]
            vh_ref[0, b, :, h, :] = v[rows, h * hd:(h + 1) * hd]


def _memory_kv(mem, norm_mem, w_xk, w_xv):
    depth, d, _ = w_xk.shape
    nb, n_mem, _ = mem.shape
    hd = d // HEADS
    flat = jax.ShapeDtypeStruct((depth, nb, n_mem, d), F32)
    heads = jax.ShapeDtypeStruct((depth, nb, n_mem, HEADS, hd), F32)
    step = MEMKV_SEQS if nb % MEMKV_SEQS == 0 else 1
    w_spec = pl.BlockSpec((1, d, d), lambda l, b: (l, 0, 0))
    flat_spec = pl.BlockSpec((1, step, n_mem, d), lambda l, b: (l, b, 0, 0))
    heads_spec = pl.BlockSpec((1, step, n_mem, HEADS, hd), lambda l, b: (l, b, 0, 0, 0))
    return pl.pallas_call(
        _memkv_kernel,
        grid=(depth, nb // step),
        in_specs=[pl.BlockSpec((step, n_mem, d), lambda l, b: (b, 0, 0)),
                  pl.BlockSpec((1, 1, d), lambda l, b: (l, 0, 0)),
                  w_spec, w_spec],
        out_specs=(flat_spec, flat_spec, heads_spec, heads_spec),
        out_shape=(flat, flat, heads, heads),
        scratch_shapes=[pltpu.VMEM((d, d), BF16), pltpu.VMEM((d, d), BF16)],
        compiler_params=_params(2),
        name="memory_kv",
    )(mem, norm_mem.reshape(depth, 1, d), w_xk, w_xv)


def _gmlp_kernel(xp_ref, xs_ref, gn_ref, win_ref, lng_ref, lnb_ref, wmix_ref, bias_ref, wout_ref,
                 op_ref, os_ref, vs_ref, win_bf, wout_bf, *, n_prompt):
    i = pl.program_id(0)
    width = wout_ref.shape[0]
    gdim = width // GROUPS

    @pl.when(i == 0)
    def _():
        win_bf[...] = win_ref[...].astype(BF16)
        wout_bf[...] = wout_ref[...].astype(BF16)

    def spatial(vb, mode):
        chunks = []
        for c in range(vb.shape[0] // CHUNK):
            cols = [_dot(wmix_ref[mode, g], vb[c * CHUNK:(c + 1) * CHUNK, g * gdim:(g + 1) * gdim])
                    for g in range(GROUPS)]
            chunks.append(jnp.concatenate(cols, axis=1) + bias_ref[mode])
        return chunks[0] if len(chunks) == 1 else jnp.concatenate(chunks, axis=0)

    def mixer(parts, mode):
        zs = [_dot(_rmsnorm(x, gn_ref[...]).astype(BF16), win_bf[...]) for x in parts]
        zs = [jax.nn.gelu(z) for z in zs]
        vs = [_layernorm(z[:, width:], lng_ref[...], lnb_ref[...]) for z in zs]
        mixed = [spatial(v.astype(BF16), mode) for v in vs]
        gated = [(z[:, :width] * m).astype(BF16) for z, m in zip(zs, mixed)]
        outs = [x + _dot(g, wout_bf[...]) for x, g in zip(parts, gated)]
        return outs, vs

    @pl.when(i < n_prompt)
    def _():
        ranges = _row_ranges(xp_ref.shape[0], CHUNK)
        outs, _ = mixer([xp_ref[r, :] for r in ranges], 0)
        for r, out in zip(ranges, outs):
            op_ref[r, :] = out

    @pl.when(i == n_prompt)
    def _():
        outs, vs = mixer([xs_ref[...]], 1)
        os_ref[...] = outs[0]
        vs_ref[...] = vs[0]


def _gmlp(xp, xs, g_norm, w_in, ln_g, ln_b, w_s, b_s, w_out):
    tp, d = xp.shape
    ns = xs.shape[0]
    width = w_out.shape[0]
    gdim = width // GROUPS
    assert tp % TOKEN_TILE == 0 and TOKEN_TILE % CHUNK == 0 and ns == CHUNK
    n_prompt = tp // TOKEN_TILE
    causal = jnp.tril(jnp.ones((CHUNK, CHUNK), dtype=bool))
    w_prompt = jnp.where(causal[None], w_s, 0.0)
    w_sample = w_s[:, 0, 0][:, None, None] * jnp.eye(CHUNK, dtype=F32)[None]
    wmix = jnp.stack([w_prompt, w_sample]).astype(BF16)
    b_prompt = jnp.repeat(jnp.transpose(b_s), gdim, axis=1)
    b_sample = jnp.broadcast_to(jnp.repeat(b_s[:, 0], gdim)[None], (CHUNK, width))
    bias = jnp.stack([b_prompt, b_sample])

    tile = lambda i: (jnp.minimum(i, n_prompt - 1), 0)
    row = lambda a: a.reshape(1, -1)
    return pl.pallas_call(
        functools.partial(_gmlp_kernel, n_prompt=n_prompt),
        grid=(n_prompt + 1,),
        in_specs=[pl.BlockSpec((TOKEN_TILE, d), tile),
                  _resident((ns, d)), _resident((1, d)), _resident(w_in.shape),
                  _resident((1, width)), _resident((1, width)),
                  _resident(wmix.shape), _resident(bias.shape), _resident(w_out.shape)],
        out_specs=(pl.BlockSpec((TOKEN_TILE, d), tile),
                   pl.BlockSpec((ns, d), lambda i: (0, 0)),
                   pl.BlockSpec((ns, width), lambda i: (0, 0))),
        out_shape=(jax.ShapeDtypeStruct((tp, d), F32),
                   jax.ShapeDtypeStruct((ns, d), F32),
                   jax.ShapeDtypeStruct((ns, width), F32)),
        scratch_shapes=[pltpu.VMEM(w_in.shape, BF16), pltpu.VMEM(w_out.shape, BF16)],
        compiler_params=_params(1),
        name="gmlp_mixer",
    )(xp, xs, row(g_norm), w_in, row(ln_g), row(ln_b), wmix, bias, w_out)


def _softmax_rows(s, axis):
    m = jnp.max(s, axis=axis, keepdims=True)
    e = jnp.exp(s - m)
    return e / jnp.sum(e, axis=axis, keepdims=True)


def _xattn_kernel(xp_ref, xs_ref, gn_ref, wq_ref, wo_ref, k_ref, v_ref,
                  op_ref, qs_ref, wq_bf, wo_bf, *, n_prompt):
    i = pl.program_id(0)
    d = wq_ref.shape[-1]
    hd = d // HEADS
    scale = hd ** -0.5

    @pl.when(i == 0)
    def _():
        wq_bf[...] = wq_ref[0].astype(BF16)
        wo_bf[...] = wo_ref[0].astype(BF16)

    def attend(parts):
        kb = k_ref[0, 0].astype(BF16)
        vb = v_ref[0, 0].astype(BF16)
        qs = [_dot(_rmsnorm(x, gn_ref[0]).astype(BF16), wq_bf[...]).astype(BF16) for x in parts]
        outs = []
        for x, q in zip(parts, qs):
            heads = []
            for h in range(HEADS):
                cols = slice(h * hd, (h + 1) * hd)
                s = lax.dot_general(q[:, cols], kb[:, cols], (((1,), (1,)), ((), ())),
                                    preferred_element_type=F32) * scale
                heads.append(_dot(_softmax_rows(s, -1).astype(BF16), vb[:, cols]))
            outs.append(jnp.concatenate(heads, axis=1).astype(BF16))
        return [x + _dot(o, wo_bf[...]) for x, o in zip(parts, outs)]

    @pl.when(i < n_prompt)
    def _():
        ranges = _row_ranges(xp_ref.shape[0], SUBLANES)
        for r, out in zip(ranges, attend([xp_ref[r, :] for r in ranges])):
            op_ref[r, :] = out

    @pl.when(i == n_prompt)
    def _():
        qs_ref[...] = _dot(_rmsnorm(xs_ref[...], gn_ref[0]).astype(BF16), wq_bf[...])


def _xattn_prompt(xp, xs, norm_xattn, w_xq, w_xo, mem_k, mem_v, layer):
    tp, d = xp.shape
    ns = xs.shape[0]
    depth, nb, n_mem, _ = mem_k.shape
    n_prompt = tp // WIDE_TILE
    tiles_per_seq = n_prompt // nb
    assert tiles_per_seq * nb == n_prompt and n_prompt * WIDE_TILE == tp
    tile = lambda i: (jnp.minimum(i, n_prompt - 1), 0)
    kv = lambda i: (layer, jnp.minimum(i, n_prompt - 1) // tiles_per_seq, 0, 0)
    of_layer = lambda shape: pl.BlockSpec((1,) + shape, lambda i: (layer,) + (0,) * len(shape),
                                          pipeline_mode=pl.Buffered(1))
    return pl.pallas_call(
        functools.partial(_xattn_kernel, n_prompt=n_prompt),
        grid=(n_prompt + 1,),
        in_specs=[pl.BlockSpec((WIDE_TILE, d), tile),
                  _resident((ns, d)), of_layer((1, d)), of_layer((d, d)), of_layer((d, d)),
                  pl.BlockSpec((1, 1, n_mem, d), kv), pl.BlockSpec((1, 1, n_mem, d), kv)],
        out_specs=(pl.BlockSpec((WIDE_TILE, d), tile),
                   pl.BlockSpec((ns, d), lambda i: (0, 0))),
        out_shape=(jax.ShapeDtypeStruct((tp, d), F32), jax.ShapeDtypeStruct((ns, d), F32)),
        scratch_shapes=[pltpu.VMEM((d, d), BF16), pltpu.VMEM((d, d), BF16)],
        compiler_params=_params(1),
        name="xattn_prompt",
    )(xp, xs, norm_xattn.reshape(depth, 1, d), w_xq, w_xo, mem_k, mem_v)


def _xattn_sample_kernel(q_ref, k_ref, v_ref, o_ref, *, block, scale):
    i = pl.program_id(0)
    for b in range(block):
        r = i * block + b
        s = jnp.sum(k_ref[0, b] * (q_ref[r] * scale)[None], axis=-1, keepdims=True)
        e = jnp.exp(s - jnp.max(s, axis=0, keepdims=True))
        o_ref[r] = jnp.sum(e * v_ref[0, b], axis=0) / jnp.sum(e, axis=0)


def _xattn_sample(qs, cache_k, cache_v, layer):
    _, ns, n_mem, heads, hd = cache_k.shape
    block = SAMPLE_ATTN_BLOCK
    assert ns % block == 0
    kv = pl.BlockSpec((1, block, n_mem, heads, hd), lambda i: (layer, i, 0, 0, 0))
    return pl.pallas_call(
        functools.partial(_xattn_sample_kernel, block=block, scale=hd ** -0.5),
        grid=(ns // block,),
        in_specs=[_resident((ns, heads, hd)), kv, kv],
        out_specs=pl.BlockSpec((ns, heads, hd), lambda i: (0, 0, 0)),
        out_shape=jax.ShapeDtypeStruct((ns, heads, hd), F32),
        compiler_params=_params(1),
        name="xattn_sample",
    )(qs.reshape(ns, heads, hd), cache_k, cache_v).reshape(qs.shape)


def _attn_out_kernel(o_ref, xs_ref, wo_ref, os_ref):
    os_ref[...] = xs_ref[...] + _dot(o_ref[...].astype(BF16), wo_ref[0].astype(BF16))


def _attn_out_sample(o, xs, w_xo, layer):
    ns, d = xs.shape
    return pl.pallas_call(
        _attn_out_kernel,
        grid=(1,),
        in_specs=[_resident((ns, d)), _resident((ns, d)),
                  pl.BlockSpec((1, d, d), lambda i: (layer, 0, 0))],
        out_specs=pl.BlockSpec((ns, d), lambda i: (0, 0)),
        out_shape=jax.ShapeDtypeStruct((ns, d), F32),
        compiler_params=_params(1),
        name="attn_out_sample",
    )(o, xs, w_xo)


def _ffn_kernel(xp_ref, xs_ref, gn_ref, wg_ref, wu_ref, wd_ref, op_ref, os_ref, *, n_prompt):
    i = pl.program_id(0)
    f_dim = wg_ref.shape[1]

    def ffn(x):
        xn = _rmsnorm(x, gn_ref[...]).astype(BF16)
        acc = x
        for f in range(0, f_dim, FFN_F_CHUNK):
            cols = slice(f, f + FFN_F_CHUNK)
            hid = jax.nn.silu(_dot(xn, wg_ref[:, cols])) * _dot(xn, wu_ref[:, cols])
            acc = acc + _dot(hid.astype(BF16), wd_ref[cols, :])
        return acc

    @pl.when(i < n_prompt)
    def _():
        op_ref[...] = ffn(xp_ref[...])

    @pl.when(i == n_prompt)
    def _():
        os_ref[...] = ffn(xs_ref[...])


def _ffn(xp, xs, g_norm, w_gate, w_up, w_down):
    tp, d = xp.shape
    ns = xs.shape[0]
    f_dim = w_gate.shape[1]
    assert f_dim % FFN_F_CHUNK == 0 and tp % WIDE_TILE == 0
    n_prompt = tp // WIDE_TILE
    wg = _cast_bf16(w_gate, d // 4)
    wu = _cast_bf16(w_up, d // 4)
    wd = _cast_bf16(w_down, f_dim // 4)
    tile = lambda i: (jnp.minimum(i, n_prompt - 1), 0)
    return pl.pallas_call(
        functools.partial(_ffn_kernel, n_prompt=n_prompt),
        grid=(n_prompt + 1,),
        in_specs=[pl.BlockSpec((WIDE_TILE, d), tile),
                  _resident((ns, d)), _resident((1, d)),
                  _resident(wg.shape), _resident(wu.shape), _resident(wd.shape)],
        out_specs=(pl.BlockSpec((WIDE_TILE, d), tile),
                   pl.BlockSpec((ns, d), lambda i: (0, 0))),
        out_shape=(jax.ShapeDtypeStruct((tp, d), F32), jax.ShapeDtypeStruct((ns, d), F32)),
        compiler_params=_params(1),
        name="dense_swiglu",
    )(xp, xs, g_norm.reshape(1, d), wg, wu, wd)


CONV_HALO = 32


SUBLANES = 8
CONV_UNROLL = 8
CONV_TAIL = 16


def _depthwise_conv(cbuf, y_ref, wdw_ref, bdw_ref, tm, taps):
    first = CONV_HALO - (taps - 1)
    d = y_ref.shape[1]
    max_a = (first + taps - 1) // SUBLANES
    classes = [[(a, SUBLANES * a + r - first) for a in range(max_a + 1)
                if 0 <= SUBLANES * a + r - first < taps] for r in range(SUBLANES)]
    row_i = lax.broadcasted_iota(jnp.int32, (SUBLANES, LANES), 0)

    for l in range(d // LANES):
        lanes = slice(l * LANES, (l + 1) * LANES)
        w = [jnp.broadcast_to(wdw_ref[k:k + 1, lanes], (SUBLANES, LANES)) for k in range(taps)]
        bias = jnp.broadcast_to(bdw_ref[:, lanes], (SUBLANES, LANES))

        def rotated_q(groups, j, w=w):
            out = []
            for r in range(SUBLANES):
                q = None
                for a, k in classes[r]:
                    term = w[k] * groups[j + a]
                    q = term if q is None else q + term
                out.append(q if r == 0 else pltpu.roll(q, SUBLANES - r, axis=0))
            return out

        head = {a: cbuf[SUBLANES * a:SUBLANES * (a + 1), lanes] for a in range(max_a + 1)}

        prev = rotated_q(head, 0)
        for blk in range(tm // (SUBLANES * CONV_UNROLL)):
            base = blk * SUBLANES * CONV_UNROLL
            groups = {j: cbuf[base + SUBLANES * j:base + SUBLANES * (j + 1), lanes]
                      for j in range(1, CONV_UNROLL + max_a + 1)}
            for u in range(CONV_UNROLL):
                nxt = rotated_q(groups, u + 1)
                y = prev[0] + bias
                for r in range(1, SUBLANES):
                    y = y + jnp.where(row_i < SUBLANES - r, prev[r], nxt[r])
                y_ref[base + SUBLANES * u:base + SUBLANES * (u + 1), lanes] = y
                prev = nxt


def _conv_prompt_kernel(xp_ref, gn_ref, w1_ref, wdw_ref, bdw_ref, lng_ref, lnb_ref, w2_ref,
                        op_ref, st_ref, w1_bf, w2_bf, cbuf, ybuf, *, tiles_per_seq):
    i = pl.program_id(0)
    d = w2_ref.shape[0]
    taps = wdw_ref.shape[0]
    tm = xp_ref.shape[0]

    @pl.when(i == 0)
    def _():
        w1_bf[...] = w1_ref[...].astype(BF16)
        w2_bf[...] = w2_ref[...].astype(BF16)
        cbuf[CONV_HALO + tm:CONV_HALO + tm + CONV_TAIL, :] = jnp.zeros((CONV_TAIL, d), F32)

    @pl.when(i % tiles_per_seq == 0)
    def _():
        cbuf[0:CONV_HALO, :] = jnp.zeros((CONV_HALO, d), F32)

    x = xp_ref[...]
    ag = _dot(_rmsnorm(x, gn_ref[...]).astype(BF16), w1_bf[...])
    c = ag[:, :d] * jax.nn.sigmoid(ag[:, d:])
    cbuf[CONV_HALO:CONV_HALO + tm, :] = c
    first = CONV_HALO - (taps - 1)
    _depthwise_conv(cbuf, ybuf, wdw_ref, bdw_ref, tm, taps)
    t = jax.nn.silu(_layernorm(ybuf[...], lng_ref[...], lnb_ref[...])).astype(BF16)
    op_ref[...] = x + _dot(t, w2_bf[...])
    cbuf[0:CONV_HALO, :] = cbuf[tm:tm + CONV_HALO, :]

    @pl.when(i % tiles_per_seq == tiles_per_seq - 1)
    def _():
        st_ref[0] = cbuf[first:CONV_HALO, :]


def _conv_prompt(xp, n_seq, g_norm, w_pw1, w_dw, b_dw, ln_g, ln_b, w_pw2):
    tp, d = xp.shape
    taps = w_dw.shape[0]
    n_prompt = tp // TOKEN_TILE
    tiles_per_seq = n_prompt // n_seq
    assert tiles_per_seq * n_seq == n_prompt and taps - 1 <= CONV_HALO <= TOKEN_TILE
    assert TOKEN_TILE % (SUBLANES * CONV_UNROLL) == 0 and CONV_TAIL >= SUBLANES * 2
    row = lambda a: a.reshape(1, -1)
    return pl.pallas_call(
        functools.partial(_conv_prompt_kernel, tiles_per_seq=tiles_per_seq),
        grid=(n_prompt,),
        in_specs=[pl.BlockSpec((TOKEN_TILE, d), lambda i: (i, 0)),
                  _resident((1, d)), _resident(w_pw1.shape), _resident(w_dw.shape),
                  _resident((1, d)), _resident((1, d)), _resident((1, d)), _resident(w_pw2.shape)],
        out_specs=(pl.BlockSpec((TOKEN_TILE, d), lambda i: (i, 0)),
                   pl.BlockSpec((1, taps - 1, d), lambda i: (i // tiles_per_seq, 0, 0))),
        out_shape=(jax.ShapeDtypeStruct((tp, d), F32),
                   jax.ShapeDtypeStruct((n_seq, taps - 1, d), F32)),
        scratch_shapes=[pltpu.VMEM(w_pw1.shape, BF16), pltpu.VMEM(w_pw2.shape, BF16),
                        pltpu.VMEM((CONV_HALO + TOKEN_TILE + CONV_TAIL, d), F32),
                        pltpu.VMEM((TOKEN_TILE, d), F32)],
        compiler_params=_params(1),
        name="conv_prompt",
    )(xp, row(g_norm), w_pw1, w_dw, row(b_dw), row(ln_g), row(ln_b), w_pw2)


def _conv_sample_kernel(xs_ref, gn_ref, w1_ref, wdw_ref, bdw_ref, lng_ref, lnb_ref, w2_ref, st_ref,
                        os_ref, sto_ref, c_all, t_all, *, block):
    i = pl.program_id(0)
    d = w2_ref.shape[0]
    taps = wdw_ref.shape[0]

    @pl.when(i == 0)
    def _():
        ag = _dot(_rmsnorm(xs_ref[...], gn_ref[...]).astype(BF16), w1_ref[...].astype(BF16))
        c_all[...] = ag[:, :d] * jax.nn.sigmoid(ag[:, d:])

    w_hist = wdw_ref[0:taps - 1, :]
    w_last = wdw_ref[taps - 1:taps, :]
    for b in range(block):
        r = i * block + b
        hist = st_ref[0, b]
        cb = c_all[pl.ds(r, 1), :]
        y = jnp.sum(hist * w_hist, axis=0, keepdims=True) + cb * w_last + bdw_ref[...]
        t_all[pl.ds(r, 1), :] = jax.nn.silu(_layernorm(y, lng_ref[...], lnb_ref[...]))
        sto_ref[0, b, 0:taps - 2, :] = hist[1:taps - 1, :]
        sto_ref[0, b, taps - 2:taps - 1, :] = cb

    @pl.when(i == pl.num_programs(0) - 1)
    def _():
        os_ref[...] = xs_ref[...] + _dot(t_all[...].astype(BF16), w2_ref[...].astype(BF16))


def _conv_sample(xs, state, g_norm, w_pw1, w_dw, b_dw, ln_g, ln_b, w_pw2):
    ns, d = xs.shape
    taps = w_dw.shape[0]
    block = SAMPLE_CONV_BLOCK
    assert ns % block == 0
    row = lambda a: a.reshape(1, -1)
    st = pl.BlockSpec((1, block, taps - 1, d), lambda i: (0, i, 0, 0))
    return pl.pallas_call(
        functools.partial(_conv_sample_kernel, block=block),
        grid=(ns // block,),
        in_specs=[_resident((ns, d)), _resident((1, d)), _resident(w_pw1.shape),
                  _resident(w_dw.shape), _resident((1, d)), _resident((1, d)), _resident((1, d)),
                  _resident(w_pw2.shape), st],
        out_specs=(pl.BlockSpec((ns, d), lambda i: (0, 0)), st),
        out_shape=(jax.ShapeDtypeStruct((ns, d), F32), jax.ShapeDtypeStruct(state.shape, F32)),
        scratch_shapes=[pltpu.VMEM((ns, d), F32), pltpu.VMEM((ns, d), F32)],
        compiler_params=_params(1),
        name="conv_sample",
    )(xs, row(g_norm), w_pw1, w_dw, row(b_dw), row(ln_g), row(ln_b), w_pw2, state)


RANK_BITS = 24
RANK_SPAN = 1 << RANK_BITS
NO_RANK = -float(1 << 20)


def _router_kernel(xp_ref, xs_ref, gn_ref, wr_ref, hn_ref, c1_ref, c2_ref, meta_ref, seen_ref, count_ref, seen,
                   *, n_prompt, n_experts):
    i = pl.program_id(0)

    def route(x):
        xn = _rmsnorm(x, gn_ref[...])
        xh = xn.astype(BF16)
        xl = (xn - xh.astype(F32)).astype(BF16)
        logits = _dot(jnp.concatenate([xh, xl, xh], axis=1), wr_ref[...])
        lane = lax.broadcasted_iota(jnp.int32, logits.shape, 1).astype(F32)
        neg = jnp.float32(-jnp.inf)
        logits = jnp.where(lane < n_experts, logits, neg)
        v1 = jnp.max(logits, axis=-1, keepdims=True)
        i1 = jnp.min(jnp.where(logits == v1, lane, float(LANES)), axis=-1, keepdims=True)
        rest = jnp.where(lane == i1, neg, logits)
        v2 = jnp.max(rest, axis=-1, keepdims=True)
        i2 = jnp.min(jnp.where(rest == v2, lane, float(LANES)), axis=-1, keepdims=True)
        e2 = jnp.exp(v2 - v1)
        denom = 1.0 + e2

        rows = x.shape[0]
        pick1 = jnp.where(lane == i1, 1.0, 0.0)
        pick2 = jnp.where(lane == i2, 1.0, 0.0)
        picks = pick1 + pick2
        r_i = lax.broadcasted_iota(jnp.int32, (rows, rows), 0)
        c_i = lax.broadcasted_iota(jnp.int32, (rows, rows), 1)
        earlier = jnp.where(c_i < r_i, 1.0, 0.0).astype(BF16)
        before = _dot(earlier, picks.astype(BF16)) + seen[0:1, :]
        rank1 = jnp.sum(pick1 * before, axis=-1, keepdims=True)
        rank2 = jnp.sum(pick2 * before, axis=-1, keepdims=True)
        total = seen[0:1, :] + jnp.sum(picks, axis=0, keepdims=True)

        meta = jnp.where(lane == 0.0, i1, jnp.where(lane == 1.0, i2, jnp.where(lane == 2.0, rank1,
               jnp.where(lane == 3.0, rank2, jnp.where(lane == 4.0, 1.0 / denom,
               jnp.where(lane == 5.0, e2 / denom, 0.0))))))
        wide = lambda a: jnp.broadcast_to(a, (rows, LANES)).astype(jnp.int32)
        code = lambda e, r: wide(e) * RANK_SPAN + wide(r)
        starts = [seen[0:1, :]]
        for w in range(1, TOKEN_TILE // DISPATCH_WIN):
            starts.append(before[w * DISPATCH_WIN:w * DISPATCH_WIN + 1, :] if w * DISPATCH_WIN < rows else total)
        pad = jnp.zeros((SUBLANES - len(starts), LANES), F32)
        seen_ref[0] = jnp.concatenate(starts + [pad], axis=0).astype(jnp.int32)
        seen[0:1, :] = total
        return xn.astype(BF16), code(i1, rank1), code(i2, rank2), meta

    def no_token(rows):
        lane_t = lax.broadcasted_iota(jnp.int32, (rows, LANES), 1)
        return jnp.where(lane_t == 2, NO_RANK, jnp.where(lane_t == 3, NO_RANK, 0.0))

    def put(hn, code1, code2, meta):
        hn_ref[...] = hn
        c1_ref[...] = code1
        c2_ref[...] = code2
        by_lane = jnp.transpose(meta)[0:SUBLANES, :]
        for w in range(TOKEN_TILE // DISPATCH_WIN):
            meta_ref[w] = by_lane[:, w * DISPATCH_WIN:(w + 1) * DISPATCH_WIN]

    @pl.when(i == 0)
    def _():
        seen[...] = jnp.zeros(seen.shape, F32)

    @pl.when(i < n_prompt)
    def _():
        put(*route(xp_ref[...]))

    @pl.when(i == n_prompt)
    def _():
        ns = xs_ref.shape[0]
        hn, code1, code2, meta = route(xs_ref[...])
        fill = lambda a, v: jnp.concatenate([a, jnp.full((TOKEN_TILE - ns, a.shape[1]), v, a.dtype)], axis=0)
        put(fill(hn, 0.0), fill(code1, -1), fill(code2, -1),
            jnp.concatenate([meta, no_token(TOKEN_TILE - ns)], axis=0))
        count_ref[...] = jnp.broadcast_to(seen[0:1, :], count_ref.shape).astype(jnp.int32)

    @pl.when(i > n_prompt)
    def _():
        none = jnp.full((TOKEN_TILE, LANES), -1, jnp.int32)
        put(jnp.zeros(hn_ref.shape, BF16), none, none, no_token(TOKEN_TILE))
        seen_ref[0] = jnp.broadcast_to(seen[0:1, :], (SUBLANES, LANES)).astype(jnp.int32)


def _router(xp, xs, g_norm, w_router):
    tp, d = xp.shape
    ns = xs.shape[0]
    n_experts = w_router.shape[1]
    n_prompt = tp // TOKEN_TILE
    n_tiles = n_prompt + 1 + -(-(DISPATCH_SPAN - 1) * DISPATCH_WIN // TOKEN_TILE)
    t_pad = n_tiles * TOKEN_TILE
    assert TOKEN_TILE % DISPATCH_WIN == 0 and TOKEN_TILE // DISPATCH_WIN <= SUBLANES
    assert t_pad * TOP_K < RANK_SPAN and n_experts * RANK_SPAN < 2 ** 31
    wins = TOKEN_TILE // DISPATCH_WIN
    wr = jnp.pad(w_router, ((0, 0), (0, LANES - n_experts)))
    wr_hi = wr.astype(BF16)
    wr_lo = (wr - wr_hi.astype(F32)).astype(BF16)
    wr = jnp.concatenate([wr_hi, wr_hi, wr_lo], axis=0)
    tile = lambda i: (jnp.minimum(i, n_prompt - 1), 0)
    lanes_spec = pl.BlockSpec((TOKEN_TILE, LANES), lambda i: (i, 0))
    return pl.pallas_call(
        functools.partial(_router_kernel, n_prompt=n_prompt, n_experts=n_experts),
        grid=(n_tiles,),
        in_specs=[pl.BlockSpec((TOKEN_TILE, d), tile),
                  _resident((ns, d)), _resident((1, d)), _resident(wr.shape)],
        out_specs=(pl.BlockSpec((TOKEN_TILE, d), lambda i: (i, 0)), lanes_spec, lanes_spec,
                   pl.BlockSpec((wins, SUBLANES, DISPATCH_WIN), lambda i: (i, 0, 0)),
                   pl.BlockSpec((1, SUBLANES, LANES), lambda i: (i, 0, 0)),
                   pl.BlockSpec((SUBLANES, LANES), lambda i: (0, 0))),
        out_shape=(jax.ShapeDtypeStruct((t_pad, d), BF16),
                   jax.ShapeDtypeStruct((t_pad, LANES), jnp.int32),
                   jax.ShapeDtypeStruct((t_pad, LANES), jnp.int32),
                   jax.ShapeDtypeStruct((n_tiles * wins, SUBLANES, DISPATCH_WIN), F32),
                   jax.ShapeDtypeStruct((n_tiles, SUBLANES, LANES), jnp.int32),
                   jax.ShapeDtypeStruct((SUBLANES, LANES), jnp.int32)),
        scratch_shapes=[pltpu.VMEM((SUBLANES, LANES), F32)],
        compiler_params=_params(1),
        name="moe_router",
    )(xp, xs, g_norm.reshape(1, d), wr)


def _lane_tile(a, width):
    return jnp.concatenate([a] * (width // LANES), axis=1)


def _dispatch_kernel(slot_nsub, win_lo, win_hi, first_row, meta_ref, hn_ref, x_ref, g_ref, *, n_experts):
    s = pl.program_id(0)
    n_sub = slot_nsub[s]
    subs_per_slot = SLOT_ROWS // SUB_ROWS
    d = hn_ref.shape[1]
    span = DISPATCH_SPAN * DISPATCH_WIN
    row_in_block = lax.broadcasted_iota(jnp.int32, (SUB_ROWS, span), 0)

    def sub_block(j, carry):
        rows = pl.ds(pl.multiple_of(j * SUB_ROWS, SUB_ROWS), SUB_ROWS)
        row_id = (row_in_block + (s * SLOT_ROWS + j * SUB_ROWS)).astype(F32)
        k = s * subs_per_slot + j

        def picked(c):
            w = win_lo[k] + c * DISPATCH_SPAN
            along = lambda row: jnp.concatenate(
                [meta_ref[w + n, row:row + 1, :] for n in range(DISPATCH_SPAN)], axis=1)

            def owned_row(expert, rank):
                first = jnp.zeros_like(expert)
                for e in range(n_experts):
                    first = jnp.where(expert == float(e), first_row[e].astype(F32), first)
                return first + rank

            hit1 = owned_row(along(0), along(2)) == row_id
            hit2 = owned_row(along(1), along(3)) == row_id
            sel = jnp.where(hit1, 1.0, jnp.where(hit2, 1.0, 0.0)).astype(BF16)
            gates = jnp.where(hit1, along(4), jnp.where(hit2, along(5), 0.0))
            base = pl.multiple_of(w * DISPATCH_WIN, DISPATCH_WIN)
            return (_dot(sel, hn_ref[pl.ds(base, span), :]).astype(BF16),
                    jnp.broadcast_to(jnp.sum(gates, axis=-1, keepdims=True), (SUB_ROWS, LANES)))

        x_ref[rows, :], g_ref[rows, :] = picked(0)

        def more(c, carry):
            xs, gs = picked(c)
            x_ref[rows, :] += xs
            g_ref[rows, :] += gs
            return carry

        n_win = win_hi[k] + 1 - win_lo[k]
        lax.fori_loop(1, (n_win + DISPATCH_SPAN - 1) // DISPATCH_SPAN, more, 0)
        return carry

    def zero_block(j, carry):
        rows = pl.ds(pl.multiple_of(j * SUB_ROWS, SUB_ROWS), SUB_ROWS)
        x_ref[rows, :] = jnp.zeros((SUB_ROWS, d), BF16)
        g_ref[rows, :] = jnp.zeros((SUB_ROWS, LANES), F32)
        return carry

    lax.fori_loop(0, n_sub, sub_block, 0)
    lax.fori_loop(n_sub, subs_per_slot, zero_block, 0)


def _dispatch(hn, meta, slot_nsub, win_lo, win_hi, first_row):
    t_pad, d = hn.shape
    n_slots = slot_nsub.shape[0]
    assert t_pad % DISPATCH_WIN == 0
    grid_spec = pltpu.PrefetchScalarGridSpec(
        num_scalar_prefetch=4,
        grid=(n_slots,),
        in_specs=[pl.BlockSpec(meta.shape, lambda s, *_: (0, 0, 0)),
                  pl.BlockSpec((t_pad, d), lambda s, *_: (0, 0), pipeline_mode=pl.Buffered(1))],
        out_specs=(pl.BlockSpec((SLOT_ROWS, d), lambda s, *_: (s, 0)),
                   pl.BlockSpec((SLOT_ROWS, LANES), lambda s, *_: (s, 0))),
    )
    return pl.pallas_call(
        functools.partial(_dispatch_kernel, n_experts=first_row.shape[0]),
        grid_spec=grid_spec,
        out_shape=(jax.ShapeDtypeStruct((n_slots * SLOT_ROWS, d), BF16),
                   jax.ShapeDtypeStruct((n_slots * SLOT_ROWS, LANES), F32)),
        compiler_params=_params(1),
        name="moe_dispatch",
    )(slot_nsub, win_lo, win_hi, first_row, meta, hn)


def _expert_kernel(slot_expert, slot_nsub, x_ref, gate_ref, wg_ref, wu_ref, wd_ref, y_ref, acc):
    s = pl.program_id(0)
    f = pl.program_id(1)
    n_sub = slot_nsub[s]
    last_f = f == pl.num_programs(1) - 1
    d = y_ref.shape[1]

    def zero_acc(j, carry):
        rows = pl.ds(pl.multiple_of(j * SUB_ROWS, SUB_ROWS), SUB_ROWS)
        acc[rows, :] = jnp.zeros((SUB_ROWS, d), F32)
        return carry

    @pl.when(f == 0)
    def _():
        lax.fori_loop(0, n_sub, zero_acc, 0)

    def block(start, size):
        rows = pl.ds(pl.multiple_of(start, SUB_ROWS), size)
        xg = x_ref[rows, :]
        hid = jax.nn.silu(_dot(xg, wg_ref[0].astype(BF16))) * _dot(xg, wu_ref[0].astype(BF16))
        acc[rows, :] += _dot(hid.astype(BF16), wd_ref[0].astype(BF16))

        @pl.when(last_f)
        def _():
            y_ref[rows, :] = (acc[rows, :] * _lane_tile(gate_ref[rows, :], d)).astype(y_ref.dtype)

    def quad(j, carry):
        block(j * (4 * SUB_ROWS), 4 * SUB_ROWS)
        return carry

    lax.fori_loop(0, n_sub // 4, quad, 0)

    @pl.when(n_sub % 4 >= 2)
    def _():
        block((n_sub // 4) * (4 * SUB_ROWS), 2 * SUB_ROWS)

    @pl.when(n_sub % 2 == 1)
    def _():
        block((n_sub - 1) * SUB_ROWS, SUB_ROWS)

    def zero_block(j, carry):
        rows = pl.ds(pl.multiple_of(j * SUB_ROWS, SUB_ROWS), SUB_ROWS)
        y_ref[rows, :] = jnp.zeros((SUB_ROWS, d), y_ref.dtype)
        return carry

    @pl.when(last_f)
    def _():
        lax.fori_loop(n_sub, SLOT_ROWS // SUB_ROWS, zero_block, 0)


def _expert_ffn(x_slots, gate_rep, slot_expert, slot_nsub, w_gate, w_up, w_down):
    n_slots = slot_expert.shape[0]
    d = x_slots.shape[1]
    f_dim = w_gate.shape[2]
    tf = EXPERT_F_TILE
    assert f_dim % tf == 0 and SLOT_ROWS % SUB_ROWS == 0
    n_f = f_dim // tf
    f_tile = lambda s, f, sn: jnp.where(sn[s] > 0, f, n_f - 1)
    grid_spec = pltpu.PrefetchScalarGridSpec(
        num_scalar_prefetch=2,
        grid=(n_slots, f_dim // tf),
        in_specs=[pl.BlockSpec((SLOT_ROWS, d), lambda s, f, se, sn: (s, 0)),
                  pl.BlockSpec((SLOT_ROWS, LANES), lambda s, f, se, sn: (s, 0)),
                  pl.BlockSpec((1, d, tf), lambda s, f, se, sn: (se[s], 0, f_tile(s, f, sn))),
                  pl.BlockSpec((1, d, tf), lambda s, f, se, sn: (se[s], 0, f_tile(s, f, sn))),
                  pl.BlockSpec((1, tf, d), lambda s, f, se, sn: (se[s], f_tile(s, f, sn), 0))],
        out_specs=pl.BlockSpec((SLOT_ROWS, d), lambda s, f, se, sn: (s, 0)),
        scratch_shapes=[pltpu.VMEM((SLOT_ROWS, d), F32)],
    )
    return pl.pallas_call(
        _expert_kernel,
        grid_spec=grid_spec,
        out_shape=jax.ShapeDtypeStruct((n_slots * SLOT_ROWS, d), BF16),
        compiler_params=_params(2),
        name="expert_swiglu",
    )(slot_expert, slot_nsub, x_slots, gate_rep, w_gate, w_up, w_down)


def _combine_kernel(n_blocks, block_ids, first_row, xp_ref, xs_ref, c1_ref, c2_ref, gf_ref, *rest,
                    n_prompt, max_blocks, n_experts):
    y_refs = rest[:max_blocks]
    op_ref, os_ref = rest[max_blocks:]
    i = pl.program_id(0)

    def owned_row(code):
        expert = code >> RANK_BITS
        first = jnp.zeros_like(code)
        for e in range(n_experts):
            first = jnp.where(expert == e, first_row[e], first)
        return jnp.where(code < 0, -1, first + (code & (RANK_SPAN - 1)))

    def combine(x, o_ref):
        rows = x.shape[0]
        p1 = _lane_tile(owned_row(c1_ref[0:rows, :]), COMBINE_BLOCK)
        p2 = _lane_tile(owned_row(c2_ref[0:rows, :]), COMBINE_BLOCK)
        lane = lax.broadcasted_iota(jnp.int32, (rows, COMBINE_BLOCK), 1)

        def picked(blocks):
            total = None
            for b in blocks:
                first_row = jnp.where(b < n_blocks[i], block_ids[i * max_blocks + b] * COMBINE_BLOCK,
                                      -2 * COMBINE_BLOCK)
                row_id = lane + first_row
                sel = jnp.where(p1 == row_id, 1.0, jnp.where(p2 == row_id, 1.0, 0.0)).astype(BF16)
                part = _dot(sel, y_refs[b][...])
                total = part if total is None else total + part
            return total

        o_ref[...] = x + picked(range(0, min(COMBINE_ALWAYS, max_blocks)))
        for g in range(COMBINE_ALWAYS, max_blocks, COMBINE_GROUP):
            @pl.when(g < n_blocks[i])
            def _():
                o_ref[...] += picked(range(g, min(g + COMBINE_GROUP, max_blocks)))
        o_ref[...] = _rmsnorm(o_ref[...], gf_ref[...])

    @pl.when(i < n_prompt)
    def _():
        combine(xp_ref[...], op_ref)

    @pl.when(i == n_prompt)
    def _():
        combine(xs_ref[...], os_ref)


def _combine(xp, xs, codes, y, n_blocks, block_ids, first_row, g_final, max_blocks):
    tp, d = xp.shape
    ns = xs.shape[0]
    n_prompt = tp // TOKEN_TILE
    tile = lambda i, *_: (jnp.minimum(i, n_prompt - 1), 0)
    pos_spec = pl.BlockSpec((TOKEN_TILE, LANES), lambda i, *_: (i, 0))

    def y_spec(b):
        return pl.BlockSpec((COMBINE_BLOCK, d), lambda i, nb, ids, fr: (ids[i * max_blocks + b], 0))

    grid_spec = pltpu.PrefetchScalarGridSpec(
        num_scalar_prefetch=3,
        grid=(n_prompt + 1,),
        in_specs=[pl.BlockSpec((TOKEN_TILE, d), tile),
                  pl.BlockSpec((ns, d), lambda i, *_: (0, 0)),
                  pos_spec, pos_spec,
                  pl.BlockSpec((1, d), lambda i, *_: (0, 0))] + [y_spec(b) for b in range(max_blocks)],
        out_specs=(pl.BlockSpec((TOKEN_TILE, d), tile),
                   pl.BlockSpec((ns, d), lambda i, *_: (0, 0))),
    )
    return pl.pallas_call(
        functools.partial(_combine_kernel, n_prompt=n_prompt, max_blocks=max_blocks,
                          n_experts=first_row.shape[0]),
        grid_spec=grid_spec,
        out_shape=(jax.ShapeDtypeStruct((tp, d), F32), jax.ShapeDtypeStruct((ns, d), F32)),
        compiler_params=_params(1),
        name="moe_combine",
    )(n_blocks, block_ids, first_row, xp, xs, codes[0], codes[1], g_final.reshape(1, d),
      *([y] * max_blocks))


def _routing_tables(seen, counts, t_valid, n_experts):
    i32 = jnp.int32
    n_tiles = seen.shape[0]
    wins_per_tile = TOKEN_TILE // DISPATCH_WIN
    n_win = n_tiles * wins_per_tile
    subs_per_slot = SLOT_ROWS // SUB_ROWS
    n_slots = t_valid * TOP_K // SLOT_ROWS + n_experts
    experts = jnp.arange(n_experts, dtype=i32)

    cnt = counts[0, :n_experts]
    n_chunks = (cnt + SLOT_ROWS - 1) // SLOT_ROWS
    chunk_end = jnp.cumsum(n_chunks)
    slot_base = chunk_end - n_chunks
    n_used = chunk_end[-1]
    first_row = slot_base * SLOT_ROWS

    sid = jnp.arange(n_slots, dtype=i32)
    expert_of = lambda s: jnp.sum((s[..., None] >= chunk_end).astype(i32), axis=-1)
    slot_expert = jnp.where(sid < n_used, expert_of(sid), expert_of(n_used - 1)).astype(i32)
    of_slot = lambda tab: jnp.sum(jnp.where(slot_expert[:, None] == experts, tab, 0), axis=-1)
    slot_rank = (sid - of_slot(slot_base)) * SLOT_ROWS
    slot_rows = jnp.where(sid < n_used, jnp.clip(of_slot(cnt) - slot_rank, 0, SLOT_ROWS), 0)
    slot_nsub = ((slot_rows + SUB_ROWS - 1) // SUB_ROWS).astype(i32)

    win_seen = seen[:, :wins_per_tile, :n_experts].reshape(n_win, n_experts)
    seen_slot = jnp.sum(jnp.where(slot_expert[:, None, None] == experts, win_seen[None], 0), axis=-1)
    j = jnp.arange(subs_per_slot, dtype=i32)[None, :]
    rank_lo = slot_rank[:, None] + j * SUB_ROWS
    rank_hi = jnp.minimum(rank_lo + SUB_ROWS, of_slot(cnt)[:, None]) - 1
    window_of = lambda r: jnp.sum((seen_slot[:, None, :] <= r[:, :, None]).astype(i32), axis=-1) - 1
    active = j < slot_nsub[:, None]
    win_lo = jnp.where(active, window_of(rank_lo), 1).reshape(-1)
    win_hi = jnp.where(active, window_of(rank_hi), 0).reshape(-1)

    tile_lo = seen[:, 0, :n_experts]
    tile_hi = jnp.concatenate([tile_lo[1:], cnt[None, :]], axis=0)
    blk_lo = tile_lo // COMBINE_BLOCK
    per_expert = jnp.where(tile_hi > tile_lo, (tile_hi - 1) // COMBINE_BLOCK - blk_lo + 1, 0)
    ends = jnp.cumsum(per_expert, axis=1)
    n_blocks = ends[:, -1]
    max_blocks = TOKEN_TILE * TOP_K // COMBINE_BLOCK + 2 * n_experts
    b = jnp.arange(max_blocks, dtype=i32)[None, :]
    e_of_b = jnp.minimum(jnp.sum((ends[:, None, :] <= b[:, :, None]).astype(i32), axis=-1), n_experts - 1)
    take = lambda tab: jnp.sum(jnp.where(e_of_b[:, :, None] == experts, tab[:, None, :], 0), axis=-1)
    rank_block = take(blk_lo) + b - (take(ends) - take(per_expert))
    ids = jnp.sum(jnp.where(e_of_b[:, :, None] == experts, first_row // COMBINE_BLOCK, 0), axis=-1) + rank_block
    valid = b < n_blocks[:, None]
    id_bits = 1 << 16
    assert n_slots * subs_per_slot < id_bits
    keyed = jnp.where(valid, jnp.arange(n_tiles, dtype=i32)[:, None] * id_bits + ids, 0)
    block_ids = (lax.cummax(keyed, axis=0) % id_bits).reshape(-1)

    return dict(slot_expert=slot_expert, slot_nsub=slot_nsub, first_row=first_row.astype(i32),
                win_lo=win_lo.astype(i32), win_hi=win_hi.astype(i32),
                n_blocks=n_blocks.astype(i32), block_ids=block_ids.astype(i32), max_blocks=max_blocks)


def _moe(xp, xs, g_norm, w_router, w_gate, w_up, w_down, g_final):
    tp, d = xp.shape
    ns = xs.shape[0]
    n_experts = w_router.shape[1]
    assert ns <= TOKEN_TILE and SLOT_ROWS % COMBINE_BLOCK == 0
    hn, code1, code2, meta, seen, counts = _router(xp, xs, g_norm, w_router)
    t = _routing_tables(seen, counts, tp + ns, n_experts)
    x_slots, gate_rep = _dispatch(hn, meta, t["slot_nsub"], t["win_lo"], t["win_hi"], t["first_row"])
    y = _expert_ffn(x_slots, gate_rep, t["slot_expert"], t["slot_nsub"], w_gate, w_up, w_down)
    return _combine(xp, xs, (code1, code2), y, t["n_blocks"], t["block_ids"], t["first_row"], g_final,
                    t["max_blocks"])


def kernel(x_prompt, x_sample, mem_prompt, cache_mem_k, cache_mem_v, state_conv, norm_mix, norm_xattn, norm_ffn, norm_mem, norm_final, w_xq, w_xk, w_xv, w_xo, a_w_in, a_ln_g, a_ln_b, a_w_s, a_b_s, a_w_out, b_w_pw1, b_w_dw, b_b_dw, b_ln_g, b_ln_b, b_w_pw2, ffn_w_gate, ffn_w_up, ffn_w_down, moe_w_router, moe_w_gate, moe_w_up, moe_w_down):
    nb, seq, d = x_prompt.shape
    ns = x_sample.shape[0]
    depth = norm_mix.shape[0]
    n_mem = mem_prompt.shape[1]
    assert depth == 2 and x_sample.shape[1] == 1

    mem_k, mem_v, mem_k_heads, mem_v_heads = _memory_kv(mem_prompt, norm_mem, w_xk, w_xv)

    hp = x_prompt.reshape(nb * seq, d)
    hs = x_sample.reshape(ns, d)

    def cross_attention(hp, hs, layer):
        hp_new, qs = _xattn_prompt(hp, hs, norm_xattn, w_xq, w_xo, mem_k, mem_v, layer)
        o = _xattn_sample(qs, cache_mem_k, cache_mem_v, layer)
        hs_new = _attn_out_sample(o, hs, w_xo, layer)
        return hp_new, hs_new

    hp, hs, v_sample = _gmlp(hp, hs, norm_mix[0], a_w_in[0], a_ln_g[0], a_ln_b[0],
                             a_w_s[0], a_b_s[0], a_w_out[0])
    hp, hs = cross_attention(hp, hs, 0)
    hp, hs = _ffn(hp, hs, norm_ffn[0], ffn_w_gate[0], ffn_w_up[0], ffn_w_down[0])

    conv_w = (norm_mix[1], b_w_pw1[0], b_w_dw[0], b_b_dw[0], b_ln_g[0], b_ln_b[0], b_w_pw2[0])
    hp, conv_state_prompt = _conv_prompt(hp, nb, *conv_w)
    hs, conv_state_sample = _conv_sample(hs, state_conv, *conv_w)
    hp, hs = cross_attention(hp, hs, 1)
    yp, ys = _moe(hp, hs, norm_ffn[1], moe_w_router[0], moe_w_gate[0], moe_w_up[0], moe_w_down[0],
                  norm_final)

    return (yp.reshape(nb, seq, d),
            ys.reshape(ns, 1, d),
            mem_k_heads,
            mem_v_heads,
            conv_state_prompt[None],
            conv_state_sample,
            v_sample.reshape(1, ns, 1, -1))
```

```python
import functools

import jax
import jax.numpy as jnp
from jax import lax
from jax.experimental import pallas as pl
from jax.experimental.pallas import tpu as pltpu

F32 = jnp.float32
BF16 = jnp.bfloat16

RMS_EPS = 1e-6
LN_EPS = 1e-5
CHUNK = 128
GROUPS = 8
HEADS = 4
TOP_K = 2
LANES = 128
V7X_VMEM_LIMIT = 56 * 1024 * 1024

TOKEN_TILE = 512
WIDE_TILE = 1024
TILE_PARTS = 2
SAMPLE_ATTN_BLOCK = 8
MEMKV_SEQS = 2
SAMPLE_CONV_BLOCK = 16
SLOT_ROWS = 2304
SUB_ROWS = 256
EXPERT_F_TILE = 512
DISPATCH_WIN = 256
DISPATCH_SPAN = 5
COMBINE_BLOCK = 256
ROW_ALIGN = 16
COMBINE_ALWAYS = 8
COMBINE_GROUP = 2
FFN_F_CHUNK = 256


def _params(n_axes, vmem=V7X_VMEM_LIMIT):
    return pltpu.CompilerParams(dimension_semantics=("arbitrary",) * n_axes,
                                vmem_limit_bytes=vmem)


def _resident(shape):
    nd = len(shape)
    return pl.BlockSpec(shape, lambda *_: (0,) * nd, pipeline_mode=pl.Buffered(1))


def _rmsnorm(x, g):
    return x * lax.rsqrt(jnp.mean(x * x, axis=-1, keepdims=True) + RMS_EPS) * g


def _layernorm(x, g, b):
    xc = x - jnp.mean(x, axis=-1, keepdims=True)
    var = jnp.mean(xc * xc, axis=-1, keepdims=True)
    return xc * lax.rsqrt(var + LN_EPS) * g + b


def _dot(a, b):
    return jnp.dot(a, b, preferred_element_type=F32)


def _row_ranges(n_rows, multiple):
    part = n_rows // TILE_PARTS
    assert part % multiple == 0
    return [slice(p * part, (p + 1) * part) for p in range(TILE_PARTS)]


def _cast_kernel(x_ref, o_ref):
    o_ref[...] = x_ref[...].astype(o_ref.dtype)


def _cast_bf16(w, rows_per_step):
    r, c = w.shape
    return pl.pallas_call(
        _cast_kernel,
        grid=(r // rows_per_step,),
        in_specs=[pl.BlockSpec((rows_per_step, c), lambda i: (i, 0))],
        out_specs=pl.BlockSpec((rows_per_step, c), lambda i: (i, 0)),
        out_shape=jax.ShapeDtypeStruct((r, c), BF16),
        compiler_params=_params(1),
        name="cast_bf16",
    )(w)


def _memkv_kernel(mem_ref, g_ref, wk_ref, wv_ref, k_ref, v_ref, kh_ref, vh_ref, wk_bf, wv_bf):
    @pl.when(pl.program_id(1) == 0)
    def _():
        wk_bf[...] = wk_ref[0].astype(BF16)
        wv_bf[...] = wv_ref[0].astype(BF16)

    hd = kh_ref.shape[-1]
    nb_step, n_mem, d = mem_ref.shape
    mn = _rmsnorm(mem_ref[...].reshape(nb_step * n_mem, d), g_ref[0]).astype(BF16)
    k = _dot(mn, wk_bf[...])
    v = _dot(mn, wv_bf[...])
    for b in range(nb_step):
        rows = slice(b * n_mem, (b + 1) * n_mem)
        k_ref[0, b] = k[rows]
        v_ref[0, b] = v[rows]
        for h in range(HEADS):
            kh_ref[0, b, :, h, :] = k[rows, h * hd:(h + 1) * hd]
            vh_ref[0, b, :, h, :] = v[rows, h * hd:(h + 1) * hd]


def _memory_kv(mem, norm_mem, w_xk, w_xv):
    depth, d, _ = w_xk.shape
    nb, n_mem, _ = mem.shape
    hd = d // HEADS
    flat = jax.ShapeDtypeStruct((depth, nb, n_mem, d), F32)
    heads = jax.ShapeDtypeStruct((depth, nb, n_mem, HEADS, hd), F32)
    step = MEMKV_SEQS if nb % MEMKV_SEQS == 0 else 1
    w_spec = pl.BlockSpec((1, d, d), lambda l, b: (l, 0, 0))
    flat_spec = pl.BlockSpec((1, step, n_mem, d), lambda l, b: (l, b, 0, 0))
    heads_spec = pl.BlockSpec((1, step, n_mem, HEADS, hd), lambda l, b: (l, b, 0, 0, 0))
    return pl.pallas_call(
        _memkv_kernel,
        grid=(depth, nb // step),
        in_specs=[pl.BlockSpec((step, n_mem, d), lambda l, b: (b, 0, 0)),
                  pl.BlockSpec((1, 1, d), lambda l, b: (l, 0, 0)),
                  w_spec, w_spec],
        out_specs=(flat_spec, flat_spec, heads_spec, heads_spec),
        out_shape=(flat, flat, heads, heads),
        scratch_shapes=[pltpu.VMEM((d, d), BF16), pltpu.VMEM((d, d), BF16)],
        compiler_params=_params(2),
        name="memory_kv",
    )(mem, norm_mem.reshape(depth, 1, d), w_xk, w_xv)


def _gmlp_kernel(xp_ref, xs_ref, gn_ref, win_ref, lng_ref, lnb_ref, wmix_ref, bias_ref, wout_ref,
                 op_ref, os_ref, vs_ref, win_bf, wout_bf, *, n_prompt):
    i = pl.program_id(0)
    width = wout_ref.shape[0]
    gdim = width // GROUPS

    @pl.when(i == 0)
    def _():
        win_bf[...] = win_ref[...].astype(BF16)
        wout_bf[...] = wout_ref[...].astype(BF16)

    def spatial(vb, mode):
        chunks = []
        for c in range(vb.shape[0] // CHUNK):
            cols = [_dot(wmix_ref[mode, g], vb[c * CHUNK:(c + 1) * CHUNK, g * gdim:(g + 1) * gdim])
                    for g in range(GROUPS)]
            chunks.append(jnp.concatenate(cols, axis=1) + bias_ref[mode])
        return chunks[0] if len(chunks) == 1 else jnp.concatenate(chunks, axis=0)

    def mixer(parts, mode):
        zs = [_dot(_rmsnorm(x, gn_ref[...]).astype(BF16), win_bf[...]) for x in parts]
        zs = [jax.nn.gelu(z) for z in zs]
        vs = [_layernorm(z[:, width:], lng_ref[...], lnb_ref[...]) for z in zs]
        mixed = [spatial(v.astype(BF16), mode) for v in vs]
        gated = [(z[:, :width] * m).astype(BF16) for z, m in zip(zs, mixed)]
        outs = [x + _dot(g, wout_bf[...]) for x, g in zip(parts, gated)]
        return outs, vs

    @pl.when(i < n_prompt)
    def _():
        ranges = _row_ranges(xp_ref.shape[0], CHUNK)
        outs, _ = mixer([xp_ref[r, :] for r in ranges], 0)
        for r, out in zip(ranges, outs):
            op_ref[r, :] = out

    @pl.when(i == n_prompt)
    def _():
        outs, vs = mixer([xs_ref[...]], 1)
        os_ref[...] = outs[0]
        vs_ref[...] = vs[0]


def _gmlp(xp, xs, g_norm, w_in, ln_g, ln_b, w_s, b_s, w_out):
    tp, d = xp.shape
    ns = xs.shape[0]
    width = w_out.shape[0]
    gdim = width // GROUPS
    assert tp % TOKEN_TILE == 0 and TOKEN_TILE % CHUNK == 0 and ns == CHUNK
    n_prompt = tp // TOKEN_TILE
    causal = jnp.tril(jnp.ones((CHUNK, CHUNK), dtype=bool))
    w_prompt = jnp.where(causal[None], w_s, 0.0)
    w_sample = w_s[:, 0, 0][:, None, None] * jnp.eye(CHUNK, dtype=F32)[None]
    wmix = jnp.stack([w_prompt, w_sample]).astype(BF16)
    b_prompt = jnp.repeat(jnp.transpose(b_s), gdim, axis=1)
    b_sample = jnp.broadcast_to(jnp.repeat(b_s[:, 0], gdim)[None], (CHUNK, width))
    bias = jnp.stack([b_prompt, b_sample])

    tile = lambda i: (jnp.minimum(i, n_prompt - 1), 0)
    row = lambda a: a.reshape(1, -1)
    return pl.pallas_call(
        functools.partial(_gmlp_kernel, n_prompt=n_prompt),
        grid=(n_prompt + 1,),
        in_specs=[pl.BlockSpec((TOKEN_TILE, d), tile),
                  _resident((ns, d)), _resident((1, d)), _resident(w_in.shape),
                  _resident((1, width)), _resident((1, width)),
                  _resident(wmix.shape), _resident(bias.shape), _resident(w_out.shape)],
        out_specs=(pl.BlockSpec((TOKEN_TILE, d), tile),
                   pl.BlockSpec((ns, d), lambda i: (0, 0)),
                   pl.BlockSpec((ns, width), lambda i: (0, 0))),
        out_shape=(jax.ShapeDtypeStruct((tp, d), F32),
                   jax.ShapeDtypeStruct((ns, d), F32),
                   jax.ShapeDtypeStruct((ns, width), F32)),
        scratch_shapes=[pltpu.VMEM(w_in.shape, BF16), pltpu.VMEM(w_out.shape, BF16)],
        compiler_params=_params(1),
        name="gmlp_mixer",
    )(xp, xs, row(g_norm), w_in, row(ln_g), row(ln_b), wmix, bias, w_out)


def _softmax_rows(s, axis):
    m = jnp.max(s, axis=axis, keepdims=True)
    e = jnp.exp(s - m)
    return e / jnp.sum(e, axis=axis, keepdims=True)


def _xattn_kernel(xp_ref, xs_ref, gn_ref, wq_ref, wo_ref, k_ref, v_ref,
                  op_ref, qs_ref, wq_bf, wo_bf, *, n_prompt):
    i = pl.program_id(0)
    d = wq_ref.shape[-1]
    hd = d // HEADS
    scale = hd ** -0.5

    @pl.when(i == 0)
    def _():
        wq_bf[...] = wq_ref[0].astype(BF16)
        wo_bf[...] = wo_ref[0].astype(BF16)

    def attend(parts):
        kb = k_ref[0, 0].astype(BF16)
        vb = v_ref[0, 0].astype(BF16)
        qs = [_dot(_rmsnorm(x, gn_ref[0]).astype(BF16), wq_bf[...]).astype(BF16) for x in parts]
        outs = []
        for x, q in zip(parts, qs):
            heads = []
            for h in range(HEADS):
                cols = slice(h * hd, (h + 1) * hd)
                s = lax.dot_general(q[:, cols], kb[:, cols], (((1,), (1,)), ((), ())),
                                    preferred_element_type=F32) * scale
                heads.append(_dot(_softmax_rows(s, -1).astype(BF16), vb[:, cols]))
            outs.append(jnp.concatenate(heads, axis=1).astype(BF16))
        return [x + _dot(o, wo_bf[...]) for x, o in zip(parts, outs)]

    @pl.when(i < n_prompt)
    def _():
        ranges = _row_ranges(xp_ref.shape[0], SUBLANES)
        for r, out in zip(ranges, attend([xp_ref[r, :] for r in ranges])):
            op_ref[r, :] = out

    @pl.when(i == n_prompt)
    def _():
        qs_ref[...] = _dot(_rmsnorm(xs_ref[...], gn_ref[0]).astype(BF16), wq_bf[...])


def _xattn_prompt(xp, xs, norm_xattn, w_xq, w_xo, mem_k, mem_v, layer):
    tp, d = xp.shape
    ns = xs.shape[0]
    depth, nb, n_mem, _ = mem_k.shape
    n_prompt = tp // WIDE_TILE
    tiles_per_seq = n_prompt // nb
    assert tiles_per_seq * nb == n_prompt and n_prompt * WIDE_TILE == tp
    tile = lambda i: (jnp.minimum(i, n_prompt - 1), 0)
    kv = lambda i: (layer, jnp.minimum(i, n_prompt - 1) // tiles_per_seq, 0, 0)
    of_layer = lambda shape: pl.BlockSpec((1,) + shape, lambda i: (layer,) + (0,) * len(shape),
                                          pipeline_mode=pl.Buffered(1))
    return pl.pallas_call(
        functools.partial(_xattn_kernel, n_prompt=n_prompt),
        grid=(n_prompt + 1,),
        in_specs=[pl.BlockSpec((WIDE_TILE, d), tile),
                  _resident((ns, d)), of_layer((1, d)), of_layer((d, d)), of_layer((d, d)),
                  pl.BlockSpec((1, 1, n_mem, d), kv), pl.BlockSpec((1, 1, n_mem, d), kv)],
        out_specs=(pl.BlockSpec((WIDE_TILE, d), tile),
                   pl.BlockSpec((ns, d), lambda i: (0, 0))),
        out_shape=(jax.ShapeDtypeStruct((tp, d), F32), jax.ShapeDtypeStruct((ns, d), F32)),
        scratch_shapes=[pltpu.VMEM((d, d), BF16), pltpu.VMEM((d, d), BF16)],
        compiler_params=_params(1),
        name="xattn_prompt",
    )(xp, xs, norm_xattn.reshape(depth, 1, d), w_xq, w_xo, mem_k, mem_v)


def _xattn_sample_kernel(q_ref, k_ref, v_ref, o_ref, *, block, scale):
    i = pl.program_id(0)
    for b in range(block):
        r = i * block + b
        s = jnp.sum(k_ref[0, b] * (q_ref[r] * scale)[None], axis=-1, keepdims=True)
        e = jnp.exp(s - jnp.max(s, axis=0, keepdims=True))
        o_ref[r] = jnp.sum(e * v_ref[0, b], axis=0) / jnp.sum(e, axis=0)


def _xattn_sample(qs, cache_k, cache_v, layer):
    _, ns, n_mem, heads, hd = cache_k.shape
    block = SAMPLE_ATTN_BLOCK
    assert ns % block == 0
    kv = pl.BlockSpec((1, block, n_mem, heads, hd), lambda i: (layer, i, 0, 0, 0))
    return pl.pallas_call(
        functools.partial(_xattn_sample_kernel, block=block, scale=hd ** -0.5),
        grid=(ns // block,),
        in_specs=[_resident((ns, heads, hd)), kv, kv],
        out_specs=pl.BlockSpec((ns, heads, hd), lambda i: (0, 0, 0)),
        out_shape=jax.ShapeDtypeStruct((ns, heads, hd), F32),
        compiler_params=_params(1),
        name="xattn_sample",
    )(qs.reshape(ns, heads, hd), cache_k, cache_v).reshape(qs.shape)


def _attn_out_kernel(o_ref, xs_ref, wo_ref, os_ref):
    os_ref[...] = xs_ref[...] + _dot(o_ref[...].astype(BF16), wo_ref[0].astype(BF16))


def _attn_out_sample(o, xs, w_xo, layer):
    ns, d = xs.shape
    return pl.pallas_call(
        _attn_out_kernel,
        grid=(1,),
        in_specs=[_resident((ns, d)), _resident((ns, d)),
                  pl.BlockSpec((1, d, d), lambda i: (layer, 0, 0))],
        out_specs=pl.BlockSpec((ns, d), lambda i: (0, 0)),
        out_shape=jax.ShapeDtypeStruct((ns, d), F32),
        compiler_params=_params(1),
        name="attn_out_sample",
    )(o, xs, w_xo)


def _ffn_kernel(xp_ref, xs_ref, gn_ref, wg_ref, wu_ref, wd_ref, op_ref, os_ref, *, n_prompt):
    i = pl.program_id(0)
    f_dim = wg_ref.shape[1]

    def ffn(x):
        xn = _rmsnorm(x, gn_ref[...]).astype(BF16)
        acc = x
        for f in range(0, f_dim, FFN_F_CHUNK):
            cols = slice(f, f + FFN_F_CHUNK)
            hid = jax.nn.silu(_dot(xn, wg_ref[:, cols])) * _dot(xn, wu_ref[:, cols])
            acc = acc + _dot(hid.astype(BF16), wd_ref[cols, :])
        return acc

    @pl.when(i < n_prompt)
    def _():
        op_ref[...] = ffn(xp_ref[...])

    @pl.when(i == n_prompt)
    def _():
        os_ref[...] = ffn(xs_ref[...])


def _ffn(xp, xs, g_norm, w_gate, w_up, w_down):
    tp, d = xp.shape
    ns = xs.shape[0]
    f_dim = w_gate.shape[1]
    assert f_dim % FFN_F_CHUNK == 0 and tp % WIDE_TILE == 0
    n_prompt = tp // WIDE_TILE
    wg = _cast_bf16(w_gate, d // 4)
    wu = _cast_bf16(w_up, d // 4)
    wd = _cast_bf16(w_down, f_dim // 4)
    tile = lambda i: (jnp.minimum(i, n_prompt - 1), 0)
    return pl.pallas_call(
        functools.partial(_ffn_kernel, n_prompt=n_prompt),
        grid=(n_prompt + 1,),
        in_specs=[pl.BlockSpec((WIDE_TILE, d), tile),
                  _resident((ns, d)), _resident((1, d)),
                  _resident(wg.shape), _resident(wu.shape), _resident(wd.shape)],
        out_specs=(pl.BlockSpec((WIDE_TILE, d), tile),
                   pl.BlockSpec((ns, d), lambda i: (0, 0))),
        out_shape=(jax.ShapeDtypeStruct((tp, d), F32), jax.ShapeDtypeStruct((ns, d), F32)),
        compiler_params=_params(1),
        name="dense_swiglu",
    )(xp, xs, g_norm.reshape(1, d), wg, wu, wd)


CONV_HALO = 32


SUBLANES = 8
CONV_UNROLL = 8
CONV_TAIL = 16


def _depthwise_conv(cbuf, y_ref, wdw_ref, bdw_ref, tm, taps):
    first = CONV_HALO - (taps - 1)
    d = y_ref.shape[1]
    max_a = (first + taps - 1) // SUBLANES
    classes = [[(a, SUBLANES * a + r - first) for a in range(max_a + 1)
                if 0 <= SUBLANES * a + r - first < taps] for r in range(SUBLANES)]
    row_i = lax.broadcasted_iota(jnp.int32, (SUBLANES, LANES), 0)

    for l in range(d // LANES):
        lanes = slice(l * LANES, (l + 1) * LANES)
        w = [jnp.broadcast_to(wdw_ref[k:k + 1, lanes], (SUBLANES, LANES)) for k in range(taps)]
        bias = jnp.broadcast_to(bdw_ref[:, lanes], (SUBLANES, LANES))

        def rotated_q(groups, j, w=w):
            out = []
            for r in range(SUBLANES):
                q = None
                for a, k in classes[r]:
                    term = w[k] * groups[j + a]
                    q = term if q is None else q + term
                out.append(q if r == 0 else pltpu.roll(q, SUBLANES - r, axis=0))
            return out

        head = {a: cbuf[SUBLANES * a:SUBLANES * (a + 1), lanes] for a in range(max_a + 1)}

        prev = rotated_q(head, 0)
        for blk in range(tm // (SUBLANES * CONV_UNROLL)):
            base = blk * SUBLANES * CONV_UNROLL
            groups = {j: cbuf[base + SUBLANES * j:base + SUBLANES * (j + 1), lanes]
                      for j in range(1, CONV_UNROLL + max_a + 1)}
            for u in range(CONV_UNROLL):
                nxt = rotated_q(groups, u + 1)
                y = prev[0] + bias
                for r in range(1, SUBLANES):
                    y = y + jnp.where(row_i < SUBLANES - r, prev[r], nxt[r])
                y_ref[base + SUBLANES * u:base + SUBLANES * (u + 1), lanes] = y
                prev = nxt


def _conv_prompt_kernel(xp_ref, gn_ref, w1_ref, wdw_ref, bdw_ref, lng_ref, lnb_ref, w2_ref,
                        op_ref, st_ref, w1_bf, w2_bf, cbuf, ybuf, *, tiles_per_seq):
    i = pl.program_id(0)
    d = w2_ref.shape[0]
    taps = wdw_ref.shape[0]
    tm = xp_ref.shape[0]

    @pl.when(i == 0)
    def _():
        w1_bf[...] = w1_ref[...].astype(BF16)
        w2_bf[...] = w2_ref[...].astype(BF16)
        cbuf[CONV_HALO + tm:CONV_HALO + tm + CONV_TAIL, :] = jnp.zeros((CONV_TAIL, d), F32)

    @pl.when(i % tiles_per_seq == 0)
    def _():
        cbuf[0:CONV_HALO, :] = jnp.zeros((CONV_HALO, d), F32)

    x = xp_ref[...]
    ag = _dot(_rmsnorm(x, gn_ref[...]).astype(BF16), w1_bf[...])
    c = ag[:, :d] * jax.nn.sigmoid(ag[:, d:])
    cbuf[CONV_HALO:CONV_HALO + tm, :] = c
    first = CONV_HALO - (taps - 1)
    _depthwise_conv(cbuf, ybuf, wdw_ref, bdw_ref, tm, taps)
    t = jax.nn.silu(_layernorm(ybuf[...], lng_ref[...], lnb_ref[...])).astype(BF16)
    op_ref[...] = x + _dot(t, w2_bf[...])
    cbuf[0:CONV_HALO, :] = cbuf[tm:tm + CONV_HALO, :]

    @pl.when(i % tiles_per_seq == tiles_per_seq - 1)
    def _():
        st_ref[0] = cbuf[first:CONV_HALO, :]


def _conv_prompt(xp, n_seq, g_norm, w_pw1, w_dw, b_dw, ln_g, ln_b, w_pw2):
    tp, d = xp.shape
    taps = w_dw.shape[0]
    n_prompt = tp // TOKEN_TILE
    tiles_per_seq = n_prompt // n_seq
    assert tiles_per_seq * n_seq == n_prompt and taps - 1 <= CONV_HALO <= TOKEN_TILE
    assert TOKEN_TILE % (SUBLANES * CONV_UNROLL) == 0 and CONV_TAIL >= SUBLANES * 2
    row = lambda a: a.reshape(1, -1)
    return pl.pallas_call(
        functools.partial(_conv_prompt_kernel, tiles_per_seq=tiles_per_seq),
        grid=(n_prompt,),
        in_specs=[pl.BlockSpec((TOKEN_TILE, d), lambda i: (i, 0)),
                  _resident((1, d)), _resident(w_pw1.shape), _resident(w_dw.shape),
                  _resident((1, d)), _resident((1, d)), _resident((1, d)), _resident(w_pw2.shape)],
        out_specs=(pl.BlockSpec((TOKEN_TILE, d), lambda i: (i, 0)),
                   pl.BlockSpec((1, taps - 1, d), lambda i: (i // tiles_per_seq, 0, 0))),
        out_shape=(jax.ShapeDtypeStruct((tp, d), F32),
                   jax.ShapeDtypeStruct((n_seq, taps - 1, d), F32)),
        scratch_shapes=[pltpu.VMEM(w_pw1.shape, BF16), pltpu.VMEM(w_pw2.shape, BF16),
                        pltpu.VMEM((CONV_HALO + TOKEN_TILE + CONV_TAIL, d), F32),
                        pltpu.VMEM((TOKEN_TILE, d), F32)],
        compiler_params=_params(1),
        name="conv_prompt",
    )(xp, row(g_norm), w_pw1, w_dw, row(b_dw), row(ln_g), row(ln_b), w_pw2)


def _conv_sample_kernel(xs_ref, gn_ref, w1_ref, wdw_ref, bdw_ref, lng_ref, lnb_ref, w2_ref, st_ref,
                        os_ref, sto_ref, c_all, t_all, *, block):
    i = pl.program_id(0)
    d = w2_ref.shape[0]
    taps = wdw_ref.shape[0]

    @pl.when(i == 0)
    def _():
        ag = _dot(_rmsnorm(xs_ref[...], gn_ref[...]).astype(BF16), w1_ref[...].astype(BF16))
        c_all[...] = ag[:, :d] * jax.nn.sigmoid(ag[:, d:])

    w_hist = wdw_ref[0:taps - 1, :]
    w_last = wdw_ref[taps - 1:taps, :]
    for b in range(block):
        r = i * block + b
        hist = st_ref[0, b]
        cb = c_all[pl.ds(r, 1), :]
        y = jnp.sum(hist * w_hist, axis=0, keepdims=True) + cb * w_last + bdw_ref[...]
        t_all[pl.ds(r, 1), :] = jax.nn.silu(_layernorm(y, lng_ref[...], lnb_ref[...]))
        sto_ref[0, b, 0:taps - 2, :] = hist[1:taps - 1, :]
        sto_ref[0, b, taps - 2:taps - 1, :] = cb

    @pl.when(i == pl.num_programs(0) - 1)
    def _():
        os_ref[...] = xs_ref[...] + _dot(t_all[...].astype(BF16), w2_ref[...].astype(BF16))


def _conv_sample(xs, state, g_norm, w_pw1, w_dw, b_dw, ln_g, ln_b, w_pw2):
    ns, d = xs.shape
    taps = w_dw.shape[0]
    block = SAMPLE_CONV_BLOCK
    assert ns % block == 0
    row = lambda a: a.reshape(1, -1)
    st = pl.BlockSpec((1, block, taps - 1, d), lambda i: (0, i, 0, 0))
    return pl.pallas_call(
        functools.partial(_conv_sample_kernel, block=block),
        grid=(ns // block,),
        in_specs=[_resident((ns, d)), _resident((1, d)), _resident(w_pw1.shape),
                  _resident(w_dw.shape), _resident((1, d)), _resident((1, d)), _resident((1, d)),
                  _resident(w_pw2.shape), st],
        out_specs=(pl.BlockSpec((ns, d), lambda i: (0, 0)), st),
        out_shape=(jax.ShapeDtypeStruct((ns, d), F32), jax.ShapeDtypeStruct(state.shape, F32)),
        scratch_shapes=[pltpu.VMEM((ns, d), F32), pltpu.VMEM((ns, d), F32)],
        compiler_params=_params(1),
        name="conv_sample",
    )(xs, row(g_norm), w_pw1, w_dw, row(b_dw), row(ln_g), row(ln_b), w_pw2, state)


RANK_BITS = 24
RANK_SPAN = 1 << RANK_BITS
NO_RANK = -float(1 << 20)


def _router_kernel(xp_ref, xs_ref, gn_ref, wr_ref, hn_ref, c1_ref, c2_ref, meta_ref, seen_ref, count_ref, seen,
                   *, n_prompt, n_experts):
    i = pl.program_id(0)

    def route(x):
        xn = _rmsnorm(x, gn_ref[...])
        xh = xn.astype(BF16)
        xl = (xn - xh.astype(F32)).astype(BF16)
        logits = _dot(jnp.concatenate([xh, xl, xh], axis=1), wr_ref[...])
        lane = lax.broadcasted_iota(jnp.int32, logits.shape, 1).astype(F32)
        neg = jnp.float32(-jnp.inf)
        logits = jnp.where(lane < n_experts, logits, neg)
        v1 = jnp.max(logits, axis=-1, keepdims=True)
        i1 = jnp.min(jnp.where(logits == v1, lane, float(LANES)), axis=-1, keepdims=True)
        rest = jnp.where(lane == i1, neg, logits)
        v2 = jnp.max(rest, axis=-1, keepdims=True)
        i2 = jnp.min(jnp.where(rest == v2, lane, float(LANES)), axis=-1, keepdims=True)
        e2 = jnp.exp(v2 - v1)
        denom = 1.0 + e2

        rows = x.shape[0]
        pick1 = jnp.where(lane == i1, 1.0, 0.0)
        pick2 = jnp.where(lane == i2, 1.0, 0.0)
        picks = pick1 + pick2
        r_i = lax.broadcasted_iota(jnp.int32, (rows, rows), 0)
        c_i = lax.broadcasted_iota(jnp.int32, (rows, rows), 1)
        earlier = jnp.where(c_i < r_i, 1.0, 0.0).astype(BF16)
        before = _dot(earlier, picks.astype(BF16)) + seen[0:1, :]
        rank1 = jnp.sum(pick1 * before, axis=-1, keepdims=True)
        rank2 = jnp.sum(pick2 * before, axis=-1, keepdims=True)
        total = seen[0:1, :] + jnp.sum(picks, axis=0, keepdims=True)

        meta = jnp.where(lane == 0.0, i1, jnp.where(lane == 1.0, i2, jnp.where(lane == 2.0, rank1,
               jnp.where(lane == 3.0, rank2, jnp.where(lane == 4.0, 1.0 / denom,
               jnp.where(lane == 5.0, e2 / denom, 0.0))))))
        wide = lambda a: jnp.broadcast_to(a, (rows, LANES)).astype(jnp.int32)
        code = lambda e, r: wide(e) * RANK_SPAN + wide(r)
        starts = [seen[0:1, :]]
        for w in range(1, TOKEN_TILE // DISPATCH_WIN):
            starts.append(before[w * DISPATCH_WIN:w * DISPATCH_WIN + 1, :] if w * DISPATCH_WIN < rows else total)
        pad = jnp.zeros((SUBLANES - len(starts), LANES), F32)
        seen_ref[0] = jnp.concatenate(starts + [pad], axis=0).astype(jnp.int32)
        seen[0:1, :] = total
        return xn.astype(BF16), code(i1, rank1), code(i2, rank2), meta

    def no_token(rows):
        lane_t = lax.broadcasted_iota(jnp.int32, (rows, LANES), 1)
        return jnp.where(lane_t == 2, NO_RANK, jnp.where(lane_t == 3, NO_RANK, 0.0))

    def put(hn, code1, code2, meta):
        hn_ref[...] = hn
        c1_ref[...] = code1
        c2_ref[...] = code2
        by_lane = jnp.transpose(meta)[0:SUBLANES, :]
        for w in range(TOKEN_TILE // DISPATCH_WIN):
            meta_ref[w] = by_lane[:, w * DISPATCH_WIN:(w + 1) * DISPATCH_WIN]

    @pl.when(i == 0)
    def _():
        seen[...] = jnp.zeros(seen.shape, F32)

    @pl.when(i < n_prompt)
    def _():
        put(*route(xp_ref[...]))

    @pl.when(i == n_prompt)
    def _():
        ns = xs_ref.shape[0]
        hn, code1, code2, meta = route(xs_ref[...])
        fill = lambda a, v: jnp.concatenate([a, jnp.full((TOKEN_TILE - ns, a.shape[1]), v, a.dtype)], axis=0)
        put(fill(hn, 0.0), fill(code1, -1), fill(code2, -1),
            jnp.concatenate([meta, no_token(TOKEN_TILE - ns)], axis=0))
        count_ref[...] = jnp.broadcast_to(seen[0:1, :], count_ref.shape).astype(jnp.int32)

    @pl.when(i > n_prompt)
    def _():
        none = jnp.full((TOKEN_TILE, LANES), -1, jnp.int32)
        put(jnp.zeros(hn_ref.shape, BF16), none, none, no_token(TOKEN_TILE))
        seen_ref[0] = jnp.broadcast_to(seen[0:1, :], (SUBLANES, LANES)).astype(jnp.int32)


def _router(xp, xs, g_norm, w_router):
    tp, d = xp.shape
    ns = xs.shape[0]
    n_experts = w_router.shape[1]
    n_prompt = tp // TOKEN_TILE
    n_tiles = n_prompt + 1 + -(-(DISPATCH_SPAN - 1) * DISPATCH_WIN // TOKEN_TILE)
    t_pad = n_tiles * TOKEN_TILE
    assert TOKEN_TILE % DISPATCH_WIN == 0 and TOKEN_TILE // DISPATCH_WIN <= SUBLANES
    assert t_pad * TOP_K < RANK_SPAN and n_experts * RANK_SPAN < 2 ** 31
    wins = TOKEN_TILE // DISPATCH_WIN
    wr = jnp.pad(w_router, ((0, 0), (0, LANES - n_experts)))
    wr_hi = wr.astype(BF16)
    wr_lo = (wr - wr_hi.astype(F32)).astype(BF16)
    wr = jnp.concatenate([wr_hi, wr_hi, wr_lo], axis=0)
    tile = lambda i: (jnp.minimum(i, n_prompt - 1), 0)
    lanes_spec = pl.BlockSpec((TOKEN_TILE, LANES), lambda i: (i, 0))
    return pl.pallas_call(
        functools.partial(_router_kernel, n_prompt=n_prompt, n_experts=n_experts),
        grid=(n_tiles,),
        in_specs=[pl.BlockSpec((TOKEN_TILE, d), tile),
                  _resident((ns, d)), _resident((1, d)), _resident(wr.shape)],
        out_specs=(pl.BlockSpec((TOKEN_TILE, d), lambda i: (i, 0)), lanes_spec, lanes_spec,
                   pl.BlockSpec((wins, SUBLANES, DISPATCH_WIN), lambda i: (i, 0, 0)),
                   pl.BlockSpec((1, SUBLANES, LANES), lambda i: (i, 0, 0)),
                   pl.BlockSpec((SUBLANES, LANES), lambda i: (0, 0))),
        out_shape=(jax.ShapeDtypeStruct((t_pad, d), BF16),
                   jax.ShapeDtypeStruct((t_pad, LANES), jnp.int32),
                   jax.ShapeDtypeStruct((t_pad, LANES), jnp.int32),
                   jax.ShapeDtypeStruct((n_tiles * wins, SUBLANES, DISPATCH_WIN), F32),
                   jax.ShapeDtypeStruct((n_tiles, SUBLANES, LANES), jnp.int32),
                   jax.ShapeDtypeStruct((SUBLANES, LANES), jnp.int32)),
        scratch_shapes=[pltpu.VMEM((SUBLANES, LANES), F32)],
        compiler_params=_params(1),
        name="moe_router",
    )(xp, xs, g_norm.reshape(1, d), wr)


def _lane_tile(a, width):
    return jnp.concatenate([a] * (width // LANES), axis=1)


def _dispatch_kernel(slot_nsub, win_lo, win_hi, first_row, meta_ref, hn_ref, x_ref, g_ref, *, n_experts):
    s = pl.program_id(0)
    n_sub = slot_nsub[s]
    subs_per_slot = SLOT_ROWS // SUB_ROWS
    d = hn_ref.shape[1]
    span = DISPATCH_SPAN * DISPATCH_WIN
    row_in_block = lax.broadcasted_iota(jnp.int32, (SUB_ROWS, span), 0)

    def sub_block(j, carry):
        rows = pl.ds(pl.multiple_of(j * SUB_ROWS, SUB_ROWS), SUB_ROWS)
        row_id = (row_in_block + (s * SLOT_ROWS + j * SUB_ROWS)).astype(F32)
        k = s * subs_per_slot + j

        def picked(c):
            w = win_lo[k] + c * DISPATCH_SPAN
            along = lambda row: jnp.concatenate(
                [meta_ref[w + n, row:row + 1, :] for n in range(DISPATCH_SPAN)], axis=1)

            def owned_row(expert, rank):
                first = jnp.zeros_like(expert)
                for e in range(n_experts):
                    first = jnp.where(expert == float(e), first_row[e].astype(F32), first)
                return first + rank

            hit1 = owned_row(along(0), along(2)) == row_id
            hit2 = owned_row(along(1), along(3)) == row_id
            sel = jnp.where(hit1, 1.0, jnp.where(hit2, 1.0, 0.0)).astype(BF16)
            gates = jnp.where(hit1, along(4), jnp.where(hit2, along(5), 0.0))
            base = pl.multiple_of(w * DISPATCH_WIN, DISPATCH_WIN)
            return (_dot(sel, hn_ref[pl.ds(base, span), :]).astype(BF16),
                    jnp.broadcast_to(jnp.sum(gates, axis=-1, keepdims=True), (SUB_ROWS, LANES)))

        x_ref[rows, :], g_ref[rows, :] = picked(0)

        def more(c, carry):
            xs, gs = picked(c)
            x_ref[rows, :] += xs
            g_ref[rows, :] += gs
            return carry

        n_win = win_hi[k] + 1 - win_lo[k]
        lax.fori_loop(1, (n_win + DISPATCH_SPAN - 1) // DISPATCH_SPAN, more, 0)
        return carry

    def zero_block(j, carry):
        rows = pl.ds(pl.multiple_of(j * SUB_ROWS, SUB_ROWS), SUB_ROWS)
        x_ref[rows, :] = jnp.zeros((SUB_ROWS, d), BF16)
        g_ref[rows, :] = jnp.zeros((SUB_ROWS, LANES), F32)
        return carry

    lax.fori_loop(0, n_sub, sub_block, 0)
    lax.fori_loop(n_sub, subs_per_slot, zero_block, 0)


def _dispatch(hn, meta, slot_nsub, win_lo, win_hi, first_row):
    t_pad, d = hn.shape
    n_slots = slot_nsub.shape[0]
    assert t_pad % DISPATCH_WIN == 0
    grid_spec = pltpu.PrefetchScalarGridSpec(
        num_scalar_prefetch=4,
        grid=(n_slots,),
        in_specs=[pl.BlockSpec(meta.shape, lambda s, *_: (0, 0, 0)),
                  pl.BlockSpec((t_pad, d), lambda s, *_: (0, 0), pipeline_mode=pl.Buffered(1))],
        out_specs=(pl.BlockSpec((SLOT_ROWS, d), lambda s, *_: (s, 0)),
                   pl.BlockSpec((SLOT_ROWS, LANES), lambda s, *_: (s, 0))),
    )
    return pl.pallas_call(
        functools.partial(_dispatch_kernel, n_experts=first_row.shape[0]),
        grid_spec=grid_spec,
        out_shape=(jax.ShapeDtypeStruct((n_slots * SLOT_ROWS, d), BF16),
                   jax.ShapeDtypeStruct((n_slots * SLOT_ROWS, LANES), F32)),
        compiler_params=_params(1),
        name="moe_dispatch",
    )(slot_nsub, win_lo, win_hi, first_row, meta, hn)


def _expert_kernel(slot_expert, slot_nsub, x_ref, gate_ref, wg_ref, wu_ref, wd_ref, y_ref, acc):
    s = pl.program_id(0)
    f = pl.program_id(1)
    n_sub = slot_nsub[s]
    last_f = f == pl.num_programs(1) - 1
    d = y_ref.shape[1]

    def zero_acc(j, carry):
        rows = pl.ds(pl.multiple_of(j * SUB_ROWS, SUB_ROWS), SUB_ROWS)
        acc[rows, :] = jnp.zeros((SUB_ROWS, d), F32)
        return carry

    @pl.when(f == 0)
    def _():
        lax.fori_loop(0, n_sub, zero_acc, 0)

    def block(start, size):
        rows = pl.ds(pl.multiple_of(start, SUB_ROWS), size)
        xg = x_ref[rows, :]
        hid = jax.nn.silu(_dot(xg, wg_ref[0].astype(BF16))) * _dot(xg, wu_ref[0].astype(BF16))
        acc[rows, :] += _dot(hid.astype(BF16), wd_ref[0].astype(BF16))

        @pl.when(last_f)
        def _():
            y_ref[rows, :] = (acc[rows, :] * _lane_tile(gate_ref[rows, :], d)).astype(y_ref.dtype)

    def quad(j, carry):
        block(j * (4 * SUB_ROWS), 4 * SUB_ROWS)
        return carry

    lax.fori_loop(0, n_sub // 4, quad, 0)

    @pl.when(n_sub % 4 >= 2)
    def _():
        block((n_sub // 4) * (4 * SUB_ROWS), 2 * SUB_ROWS)

    @pl.when(n_sub % 2 == 1)
    def _():
        block((n_sub - 1) * SUB_ROWS, SUB_ROWS)

    def zero_block(j, carry):
        rows = pl.ds(pl.multiple_of(j * SUB_ROWS, SUB_ROWS), SUB_ROWS)
        y_ref[rows, :] = jnp.zeros((SUB_ROWS, d), y_ref.dtype)
        return carry

    @pl.when(last_f)
    def _():
        lax.fori_loop(n_sub, SLOT_ROWS // SUB_ROWS, zero_block, 0)


def _expert_ffn(x_slots, gate_rep, slot_expert, slot_nsub, w_gate, w_up, w_down):
    n_slots = slot_expert.shape[0]
    d = x_slots.shape[1]
    f_dim = w_gate.shape[2]
    tf = EXPERT_F_TILE
    assert f_dim % tf == 0 and SLOT_ROWS % SUB_ROWS == 0
    n_f = f_dim // tf
    f_tile = lambda s, f, sn: jnp.where(sn[s] > 0, f, n_f - 1)
    grid_spec = pltpu.PrefetchScalarGridSpec(
        num_scalar_prefetch=2,
        grid=(n_slots, f_dim // tf),
        in_specs=[pl.BlockSpec((SLOT_ROWS, d), lambda s, f, se, sn: (s, 0)),
                  pl.BlockSpec((SLOT_ROWS, LANES), lambda s, f, se, sn: (s, 0)),
                  pl.BlockSpec((1, d, tf), lambda s, f, se, sn: (se[s], 0, f_tile(s, f, sn))),
                  pl.BlockSpec((1, d, tf), lambda s, f, se, sn: (se[s], 0, f_tile(s, f, sn))),
                  pl.BlockSpec((1, tf, d), lambda s, f, se, sn: (se[s], f_tile(s, f, sn), 0))],
        out_specs=pl.BlockSpec((SLOT_ROWS, d), lambda s, f, se, sn: (s, 0)),
        scratch_shapes=[pltpu.VMEM((SLOT_ROWS, d), F32)],
    )
    return pl.pallas_call(
        _expert_kernel,
        grid_spec=grid_spec,
        out_shape=jax.ShapeDtypeStruct((n_slots * SLOT_ROWS, d), BF16),
        compiler_params=_params(2),
        name="expert_swiglu",
    )(slot_expert, slot_nsub, x_slots, gate_rep, w_gate, w_up, w_down)


def _combine_kernel(n_blocks, block_ids, row_end, first_row, xp_ref, xs_ref, c1_ref, c2_ref, gf_ref, *rest,
                    n_prompt, max_blocks, n_experts):
    y_refs = rest[:max_blocks]
    op_ref, os_ref = rest[max_blocks:]
    i = pl.program_id(0)

    def owned_row(code):
        expert = code >> RANK_BITS
        first = jnp.zeros_like(code)
        for e in range(n_experts):
            first = jnp.where(expert == e, first_row[e], first)
        return jnp.where(code < 0, -1, first + (code & (RANK_SPAN - 1)))

    def combine(x, o_ref):
        rows = x.shape[0]
        p1 = _lane_tile(owned_row(c1_ref[0:rows, :]), COMBINE_BLOCK)
        p2 = _lane_tile(owned_row(c2_ref[0:rows, :]), COMBINE_BLOCK)
        lane = lax.broadcasted_iota(jnp.int32, (rows, COMBINE_BLOCK), 1)

        def picked(blocks):
            total = None
            for b in blocks:
                used = b < n_blocks[i]
                first = jnp.where(used, block_ids[i * max_blocks + b], -2 * COMBINE_BLOCK)
                row_id = lane + first
                row_id = jnp.where(row_id < jnp.where(used, row_end[i * max_blocks + b], 0), row_id,
                                   -2 * COMBINE_BLOCK)
                sel = jnp.where(p1 == row_id, 1.0, jnp.where(p2 == row_id, 1.0, 0.0)).astype(BF16)
                part = _dot(sel, y_refs[b][...])
                total = part if total is None else total + part
            return total

        o_ref[...] = x + picked(range(0, min(COMBINE_ALWAYS, max_blocks)))
        for g in range(COMBINE_ALWAYS, max_blocks, COMBINE_GROUP):
            @pl.when(g < n_blocks[i])
            def _():
                o_ref[...] += picked(range(g, min(g + COMBINE_GROUP, max_blocks)))
        o_ref[...] = _rmsnorm(o_ref[...], gf_ref[...])

    @pl.when(i < n_prompt)
    def _():
        combine(xp_ref[...], op_ref)

    @pl.when(i == n_prompt)
    def _():
        combine(xs_ref[...], os_ref)


def _combine(xp, xs, codes, y, n_blocks, block_ids, row_end, first_row, g_final, max_blocks):
    tp, d = xp.shape
    ns = xs.shape[0]
    n_prompt = tp // TOKEN_TILE
    tile = lambda i, *_: (jnp.minimum(i, n_prompt - 1), 0)
    pos_spec = pl.BlockSpec((TOKEN_TILE, LANES), lambda i, *_: (i, 0))

    def y_spec(b):
        return pl.BlockSpec((pl.Element(COMBINE_BLOCK), pl.Element(d)),
                            lambda i, nb, ids, re, fr: (pl.multiple_of(ids[i * max_blocks + b], ROW_ALIGN), 0))

    grid_spec = pltpu.PrefetchScalarGridSpec(
        num_scalar_prefetch=4,
        grid=(n_prompt + 1,),
        in_specs=[pl.BlockSpec((TOKEN_TILE, d), tile),
                  pl.BlockSpec((ns, d), lambda i, *_: (0, 0)),
                  pos_spec, pos_spec,
                  pl.BlockSpec((1, d), lambda i, *_: (0, 0))] + [y_spec(b) for b in range(max_blocks)],
        out_specs=(pl.BlockSpec((TOKEN_TILE, d), tile),
                   pl.BlockSpec((ns, d), lambda i, *_: (0, 0))),
    )
    return pl.pallas_call(
        functools.partial(_combine_kernel, n_prompt=n_prompt, max_blocks=max_blocks,
                          n_experts=first_row.shape[0]),
        grid_spec=grid_spec,
        out_shape=(jax.ShapeDtypeStruct((tp, d), F32), jax.ShapeDtypeStruct((ns, d), F32)),
        compiler_params=_params(1),
        name="moe_combine",
    )(n_blocks, block_ids, row_end, first_row, xp, xs, codes[0], codes[1], g_final.reshape(1, d),
      *([y] * max_blocks))


def _routing_tables(seen, counts, t_valid, n_experts):
    i32 = jnp.int32
    n_tiles = seen.shape[0]
    wins_per_tile = TOKEN_TILE // DISPATCH_WIN
    n_win = n_tiles * wins_per_tile
    subs_per_slot = SLOT_ROWS // SUB_ROWS
    n_slots = t_valid * TOP_K // SLOT_ROWS + n_experts + 1
    experts = jnp.arange(n_experts, dtype=i32)

    cnt = counts[0, :n_experts]
    n_chunks = (cnt + SLOT_ROWS - 1) // SLOT_ROWS
    chunk_end = jnp.cumsum(n_chunks)
    slot_base = chunk_end - n_chunks
    n_used = chunk_end[-1]
    first_row = slot_base * SLOT_ROWS

    sid = jnp.arange(n_slots, dtype=i32)
    expert_of = lambda s: jnp.sum((s[..., None] >= chunk_end).astype(i32), axis=-1)
    slot_expert = jnp.where(sid < n_used, expert_of(sid), expert_of(n_used - 1)).astype(i32)
    of_slot = lambda tab: jnp.sum(jnp.where(slot_expert[:, None] == experts, tab, 0), axis=-1)
    slot_rank = (sid - of_slot(slot_base)) * SLOT_ROWS
    slot_rows = jnp.where(sid < n_used, jnp.clip(of_slot(cnt) - slot_rank, 0, SLOT_ROWS), 0)
    slot_nsub = ((slot_rows + SUB_ROWS - 1) // SUB_ROWS).astype(i32)

    win_seen = seen[:, :wins_per_tile, :n_experts].reshape(n_win, n_experts)
    seen_slot = jnp.sum(jnp.where(slot_expert[:, None, None] == experts, win_seen[None], 0), axis=-1)
    j = jnp.arange(subs_per_slot, dtype=i32)[None, :]
    rank_lo = slot_rank[:, None] + j * SUB_ROWS
    rank_hi = jnp.minimum(rank_lo + SUB_ROWS, of_slot(cnt)[:, None]) - 1
    window_of = lambda r: jnp.sum((seen_slot[:, None, :] <= r[:, :, None]).astype(i32), axis=-1) - 1
    active = j < slot_nsub[:, None]
    win_lo = jnp.where(active, window_of(rank_lo), 1).reshape(-1)
    win_hi = jnp.where(active, window_of(rank_hi), 0).reshape(-1)

    tile_lo = seen[:, 0, :n_experts]
    tile_hi = jnp.concatenate([tile_lo[1:], cnt[None, :]], axis=0)
    start = (first_row[None, :] + tile_lo) // ROW_ALIGN * ROW_ALIGN
    per_expert = jnp.where(tile_hi > tile_lo,
                           (first_row[None, :] + tile_hi - start + COMBINE_BLOCK - 1) // COMBINE_BLOCK, 0)
    ends = jnp.cumsum(per_expert, axis=1)
    n_blocks = ends[:, -1]
    max_blocks = TOKEN_TILE * TOP_K // COMBINE_BLOCK + 2 * n_experts
    b = jnp.arange(max_blocks, dtype=i32)[None, :]
    e_of_b = jnp.minimum(jnp.sum((ends[:, None, :] <= b[:, :, None]).astype(i32), axis=-1), n_experts - 1)
    take = lambda tab: jnp.sum(jnp.where(e_of_b[:, :, None] == experts, tab[:, None, :], 0), axis=-1)
    ids = take(start) + (b - (take(ends) - take(per_expert))) * COMBINE_BLOCK
    row_end = take(first_row[None, :] + tile_hi).reshape(-1)
    valid = b < n_blocks[:, None]
    id_bits = 1 << 16
    assert n_slots * SLOT_ROWS < id_bits
    keyed = jnp.where(valid, jnp.arange(n_tiles, dtype=i32)[:, None] * id_bits + ids, 0)
    block_ids = (lax.cummax(keyed, axis=0) % id_bits).reshape(-1)

    return dict(slot_expert=slot_expert, slot_nsub=slot_nsub, first_row=first_row.astype(i32),
                win_lo=win_lo.astype(i32), win_hi=win_hi.astype(i32),
                n_blocks=n_blocks.astype(i32), block_ids=block_ids.astype(i32), row_end=row_end.astype(i32),
                max_blocks=max_blocks)


def _moe(xp, xs, g_norm, w_router, w_gate, w_up, w_down, g_final):
    tp, d = xp.shape
    ns = xs.shape[0]
    n_experts = w_router.shape[1]
    assert ns <= TOKEN_TILE and SLOT_ROWS % COMBINE_BLOCK == 0
    hn, code1, code2, meta, seen, counts = _router(xp, xs, g_norm, w_router)
    t = _routing_tables(seen, counts, tp + ns, n_experts)
    x_slots, gate_rep = _dispatch(hn, meta, t["slot_nsub"], t["win_lo"], t["win_hi"], t["first_row"])
    y = _expert_ffn(x_slots, gate_rep, t["slot_expert"], t["slot_nsub"], w_gate, w_up, w_down)
    return _combine(xp, xs, (code1, code2), y, t["n_blocks"], t["block_ids"], t["row_end"], t["first_row"],
                    g_final, t["max_blocks"])


def kernel(x_prompt, x_sample, mem_prompt, cache_mem_k, cache_mem_v, state_conv, norm_mix, norm_xattn, norm_ffn, norm_mem, norm_final, w_xq, w_xk, w_xv, w_xo, a_w_in, a_ln_g, a_ln_b, a_w_s, a_b_s, a_w_out, b_w_pw1, b_w_dw, b_b_dw, b_ln_g, b_ln_b, b_w_pw2, ffn_w_gate, ffn_w_up, ffn_w_down, moe_w_router, moe_w_gate, moe_w_up, moe_w_down):
    nb, seq, d = x_prompt.shape
    ns = x_sample.shape[0]
    depth = norm_mix.shape[0]
    n_mem = mem_prompt.shape[1]
    assert depth == 2 and x_sample.shape[1] == 1

    mem_k, mem_v, mem_k_heads, mem_v_heads = _memory_kv(mem_prompt, norm_mem, w_xk, w_xv)

    hp = x_prompt.reshape(nb * seq, d)
    hs = x_sample.reshape(ns, d)

    def cross_attention(hp, hs, layer):
        hp_new, qs = _xattn_prompt(hp, hs, norm_xattn, w_xq, w_xo, mem_k, mem_v, layer)
        o = _xattn_sample(qs, cache_mem_k, cache_mem_v, layer)
        hs_new = _attn_out_sample(o, hs, w_xo, layer)
        return hp_new, hs_new

    hp, hs, v_sample = _gmlp(hp, hs, norm_mix[0], a_w_in[0], a_ln_g[0], a_ln_b[0],
                             a_w_s[0], a_b_s[0], a_w_out[0])
    hp, hs = cross_attention(hp, hs, 0)
    hp, hs = _ffn(hp, hs, norm_ffn[0], ffn_w_gate[0], ffn_w_up[0], ffn_w_down[0])

    conv_w = (norm_mix[1], b_w_pw1[0], b_w_dw[0], b_b_dw[0], b_ln_g[0], b_ln_b[0], b_w_pw2[0])
    hp, conv_state_prompt = _conv_prompt(hp, nb, *conv_w)
    hs, conv_state_sample = _conv_sample(hs, state_conv, *conv_w)
    hp, hs = cross_attention(hp, hs, 1)
    yp, ys = _moe(hp, hs, norm_ffn[1], moe_w_router[0], moe_w_gate[0], moe_w_up[0], moe_w_down[0],
                  norm_final)

    return (yp.reshape(nb, seq, d),
            ys.reshape(ns, 1, d),
            mem_k_heads,
            mem_v_heads,
            conv_state_prompt[None],
            conv_state_sample,
            v_sample.reshape(1, ns, 1, -1))
```
